```python
import jax, jax.numpy as jnp
from jax import lax
import numpy as np

D_MODEL = 1024
BATCH = 8
SEQ = 4096
DEPTH = 2
DEC_BATCH = 16
DEC_SEQ = 32
PAST_LEN = 4096

CHUNK = 64
N_MEM = 256
MEM_HEADS = 4
MEM_HEAD_DIM = D_MODEL // MEM_HEADS
D_FF = 2816
LRU_WIDTH = D_MODEL // 2
LRU_BLOCKS = 8
LRU_BLOCK = LRU_WIDTH // LRU_BLOCKS
CONV_WIDTH = 4
LRU_C = 8.0
RET_HEADS = 4
RET_WIDTH = D_MODEL // 2
RET_HEAD_DIM = RET_WIDTH // RET_HEADS
ROPE_BASE = 10000.0
L0_IN = 2 * LRU_WIDTH + 4 * RET_WIDTH
RWKV_HEAD = 64
RWKV_HEADS = D_MODEL // RWKV_HEAD
LORA_W = 64
LORA_A = 64
LORA_G = 128
LN_EPS = 1e-5
RWKV_GN_EPS = 64e-5
ALPHA = (2 * DEPTH) ** 0.25
BETA = (8 * DEPTH) ** -0.25

kernel_name = "hybrid_streaming_encoder_step"


def layer_norm(x, g, b, eps=LN_EPS):
    xf = x.astype(jnp.float32)
    mu = xf.mean(-1, keepdims=True)
    var = jnp.square(xf - mu).mean(-1, keepdims=True)
    return ((xf - mu) * lax.rsqrt(var + eps) * g + b).astype(x.dtype)


def post_norm(x, sub, g, b):
    return layer_norm(ALPHA * x + sub, g, b)


def swiglu(x, w_up, w_down):
    gate, up = jnp.split(x @ w_up, 2, axis=-1)
    return (jax.nn.silu(gate) * up) @ w_down


def causal_conv(x, buf, w, b):
    T = x.shape[1]
    xpad = jnp.concatenate([buf.astype(x.dtype), x], axis=1)
    y = b + sum(xpad[:, j:j + T] * w[j] for j in range(CONV_WIDTH))
    return y, xpad[:, T:]


def rg_lru(x, h0, w_a, b_a, w_x, b_x, lam):
    B, T, _ = x.shape
    xb = x.reshape(B, T, LRU_BLOCKS, LRU_BLOCK)
    r = jax.nn.sigmoid(jnp.einsum('btgi,gij->btgj', xb, w_a).reshape(B, T, LRU_WIDTH) + b_a)
    i = jax.nn.sigmoid(jnp.einsum('btgi,gij->btgj', xb, w_x).reshape(B, T, LRU_WIDTH) + b_x)
    log_a = -LRU_C * jax.nn.softplus(-lam.astype(jnp.float32)) * r.astype(jnp.float32)
    a = jnp.exp(log_a)
    u = jnp.sqrt(-jnp.expm1(2.0 * log_a)) * (i * x).astype(jnp.float32)
    u = u.at[:, 0].add(a[:, 0] * h0.astype(jnp.float32))

    def combine(c1, c2):
        a1, b1 = c1
        a2, b2 = c2
        return a1 * a2, a2 * b1 + b2

    _, h = lax.associative_scan(combine, (a, u), axis=1)
    return h.astype(x.dtype), h[:, -1].astype(h0.dtype)


def rotary(x, pos):
    half = x.shape[-1] // 2
    inv_freq = ROPE_BASE ** (-jnp.arange(half, dtype=jnp.float32) / half)
    ang = pos.astype(jnp.float32)[:, None] * inv_freq[None, :]
    cos = jnp.cos(ang)[None, :, None, :]
    sin = jnp.sin(ang)[None, :, None, :]
    xf = x.astype(jnp.float32)
    x1, x2 = xf[..., :half], xf[..., half:]
    return jnp.concatenate([x1 * cos - x2 * sin, x1 * sin + x2 * cos], axis=-1)


def retention(q, k, v, s0, chunk):
    B, T, H, dk = q.shape
    dv = v.shape[-1]
    n = T // chunk
    log_g = jnp.log1p(-(2.0 ** (-5.0 - jnp.arange(H, dtype=jnp.float32))))
    qc = q.astype(jnp.float32).reshape(B, n, chunk, H, dk)
    kc = (k.astype(jnp.float32) * dk ** -0.5).reshape(B, n, chunk, H, dk)
    vc = v.astype(jnp.float32).reshape(B, n, chunk, H, dv)
    idx = jnp.arange(chunk, dtype=jnp.float32)
    diff = idx[:, None] - idx[None, :]
    dmask = jnp.where(diff >= 0, jnp.exp(log_g[:, None, None] * jnp.maximum(diff, 0.0)), 0.0)
    scores = jnp.einsum('bnihd,bnjhd->bnhij', qc, kc) * dmask
    inner = jnp.einsum('bnhij,bnjhv->bnihv', scores, vc)
    zeta = jnp.exp(log_g[:, None] * (chunk - 1.0 - idx)[None, :])
    kv_chunk = jnp.einsum('bnjhd,hj,bnjhv->bnhdv', kc, zeta, vc)
    chunk_decay = jnp.exp(log_g * chunk)[None, :, None, None]

    def step(S, kv):
        return chunk_decay * S + kv, S

    s_last, s_prev = lax.scan(step, s0.astype(jnp.float32), jnp.moveaxis(kv_chunk, 1, 0))
    s_prev = jnp.moveaxis(s_prev, 0, 1)
    xi = jnp.exp(log_g[None, :] * (idx[:, None] + 1.0))[:, :, None]
    cross = jnp.einsum('bnihd,bnhdv->bnihv', qc, s_prev) * xi
    return (inner + cross).reshape(B, T, H, dv), s_last


def mixer_ab(x, pos, conv_buf, h0, s0, w_in, conv_w, conv_b, lru_wa, lru_ba, lru_wx, lru_bx,
             lru_lambda, ret_gn_g, ret_gn_b, w_out):
    B, T, _ = x.shape
    xa, ga, q, k, v, g = jnp.split(x @ w_in, 6, axis=-1)
    xc, new_buf = causal_conv(xa, conv_buf, conv_w, conv_b)
    h, h_last = rg_lru(xc, h0, lru_wa, lru_ba, lru_wx, lru_bx, lru_lambda)
    ya = h * jax.nn.gelu(ga)
    q = rotary(q.reshape(B, T, RET_HEADS, RET_HEAD_DIM), pos)
    k = rotary(k.reshape(B, T, RET_HEADS, RET_HEAD_DIM), pos)
    v = v.reshape(B, T, RET_HEADS, RET_HEAD_DIM)
    o, s_last = retention(q, k, v, s0, min(CHUNK, T))
    o = layer_norm(o, ret_gn_g.reshape(RET_HEADS, RET_HEAD_DIM), ret_gn_b.reshape(RET_HEADS, RET_HEAD_DIM))
    yb = o.reshape(B, T, RET_WIDTH).astype(x.dtype) * jax.nn.silu(g)
    y = jnp.concatenate([ya, yb], axis=-1) @ w_out
    return y, (new_buf, h_last, s_last.astype(s0.dtype))


def mixer_c(x, shift_buf, s0, mu, w_rkv, w0, w1, w2, a0, a1, a2, g1, g2, k_k, k_a, r_k,
            gn_g, gn_b, w_out):
    B, T, D = x.shape
    f32 = jnp.float32
    x_prev = jnp.concatenate([shift_buf.astype(x.dtype), x[:, :-1]], axis=1)
    xm = x[None] + (x_prev - x)[None] * mu[:, None, None, :]
    rkv = jnp.einsum('pbtd,pde->pbte', xm[:3], w_rkv)
    r, k, v = rkv[0], rkv[1], rkv[2]
    w = -jax.nn.softplus(-(w0 + jnp.tanh(xm[3] @ w1) @ w2)) - 0.5
    decay = jnp.exp(-jnp.exp(w.astype(f32)))
    iclr = jax.nn.sigmoid(a0 + (xm[4] @ a1) @ a2)
    g = jax.nn.sigmoid(xm[5] @ g1) @ g2
    hs = lambda t: t.reshape(B, T, RWKV_HEADS, RWKV_HEAD).astype(f32)
    r, k, v, decay, iclr = hs(r), hs(k), hs(v), hs(decay), hs(iclr)
    kk = k * k_k.reshape(RWKV_HEADS, RWKV_HEAD)
    kk = kk / jnp.maximum(jnp.linalg.norm(kk, axis=-1, keepdims=True), 1e-12)
    k = k * (1.0 + (iclr - 1.0) * k_a.reshape(RWKV_HEADS, RWKV_HEAD))

    def step(S, inp):
        r_t, w_t, k_t, v_t, kk_t, b_t = inp
        sa = jnp.einsum('bhvk,bhk->bhv', S, -kk_t)
        S = S * w_t[:, :, None, :] + sa[..., None] * b_t[:, :, None, :] + v_t[..., None] * k_t[:, :, None, :]
        return S, jnp.einsum('bhvk,bhk->bhv', S, r_t)

    tm = lambda t: jnp.moveaxis(t, 1, 0)
    s_last, o = lax.scan(step, s0.astype(f32), (tm(r), tm(decay), tm(k), tm(v), tm(kk), tm(kk * iclr)))
    o = jnp.moveaxis(o, 0, 1)
    o = layer_norm(o, gn_g.reshape(RWKV_HEADS, RWKV_HEAD), gn_b.reshape(RWKV_HEADS, RWKV_HEAD), RWKV_GN_EPS)
    o = o + (r * k * r_k).sum(-1, keepdims=True) * v
    y = (o.reshape(B, T, D).astype(x.dtype) * g) @ w_out
    return y, (x[:, -1:].astype(shift_buf.dtype), s_last.astype(s0.dtype))


def cross_attn(x, mem_k, mem_v, w_q, w_o):
    B, T, D = x.shape
    q = (x @ w_q).reshape(B, T, MEM_HEADS, MEM_HEAD_DIM)
    s = jnp.einsum('bthd,bmhd->bhtm', q, mem_k).astype(jnp.float32) * MEM_HEAD_DIM ** -0.5
    p = jax.nn.softmax(s, axis=-1).astype(x.dtype)
    o = jnp.einsum('bhtm,bmhd->bthd', p, mem_v).reshape(B, T, D)
    return o @ w_o


def run_trunk(x, pos, mem_k, mem_v, states, ln_g, ln_b, ffn_up, ffn_down, xa_q, xa_o, mixer_params):
    new_states = []
    for layer in range(DEPTH):
        x = post_norm(x, 0.5 * swiglu(x, ffn_up[layer, 0], ffn_down[layer, 0]), ln_g[layer, 0], ln_b[layer, 0])
        if layer % 2 == 0:
            y, st = mixer_ab(x, pos, *states[layer], *mixer_params[layer])
        else:
            y, st = mixer_c(x, *states[layer], *mixer_params[layer])
        x = post_norm(x, y, ln_g[layer, 1], ln_b[layer, 1])
        x = post_norm(x, cross_attn(x, mem_k[layer], mem_v[layer], xa_q[layer], xa_o[layer]),
                      ln_g[layer, 2], ln_b[layer, 2])
        x = post_norm(x, 0.5 * swiglu(x, ffn_up[layer, 1], ffn_down[layer, 1]), ln_g[layer, 3], ln_b[layer, 3])
        new_states.append(st)
    return x, new_states


def setup_inputs(seed: int = 0) -> dict:
    key = jax.random.key(seed)
    ks = iter(jax.random.split(key, 64))
    f32 = jnp.float32

    def nrm(shape, scale):
        return scale * jax.random.normal(next(ks), shape, f32)

    def uni(shape, lo, hi):
        return jax.random.uniform(next(ks), shape, f32, minval=lo, maxval=hi)

    D = D_MODEL
    lam_p = uni((LRU_WIDTH,), 0.9, 0.999)
    return {
        'x_prompt': nrm((BATCH, SEQ, D), 1.0),
        'x_sample': nrm((DEC_BATCH, DEC_SEQ, D), 1.0),
        'mem_prompt': nrm((BATCH, N_MEM, D), 1.0),
        'state_conv0': nrm((DEC_BATCH, CONV_WIDTH - 1, LRU_WIDTH), 1.0),
        'state_lru0': nrm((DEC_BATCH, LRU_WIDTH), 0.5),
        'state_ret0': nrm((DEC_BATCH, RET_HEADS, RET_HEAD_DIM, RET_HEAD_DIM), 0.5),
        'state_shift1': nrm((DEC_BATCH, 1, D), 1.0),
        'state_wkv1': nrm((DEC_BATCH, RWKV_HEADS, RWKV_HEAD, RWKV_HEAD), 0.5),
        'cache_mem_k': nrm((DEPTH, DEC_BATCH, N_MEM, MEM_HEADS, MEM_HEAD_DIM), 1.0),
        'cache_mem_v': nrm((DEPTH, DEC_BATCH, N_MEM, MEM_HEADS, MEM_HEAD_DIM), 1.0),
        'ln_g': 1.0 + nrm((DEPTH, 4, D), 0.02),
        'ln_b': nrm((DEPTH, 4, D), 0.02),
        'ffn_up': nrm((DEPTH, 2, D, 2 * D_FF), D ** -0.5),
        'ffn_down': nrm((DEPTH, 2, D_FF, D), BETA * D_FF ** -0.5),
        'xa_q': nrm((DEPTH, D, D), D ** -0.5),
        'xa_k': nrm((DEPTH, D, D), D ** -0.5),
        'xa_v': nrm((DEPTH, D, D), D ** -0.5),
        'xa_o': nrm((DEPTH, D, D), BETA * D ** -0.5),
        'l0_w_in': nrm((D, L0_IN), D ** -0.5),
        'l0_conv_w': nrm((CONV_WIDTH, LRU_WIDTH), CONV_WIDTH ** -0.5),
        'l0_conv_b': nrm((LRU_WIDTH,), 0.01),
        'l0_lru_wa': nrm((LRU_BLOCKS, LRU_BLOCK, LRU_BLOCK), LRU_BLOCK ** -0.5),
        'l0_lru_ba': nrm((LRU_WIDTH,), 0.01),
        'l0_lru_wx': nrm((LRU_BLOCKS, LRU_BLOCK, LRU_BLOCK), LRU_BLOCK ** -0.5),
        'l0_lru_bx': nrm((LRU_WIDTH,), 0.01),
        'l0_lru_lambda': jnp.log(lam_p) - jnp.log1p(-lam_p),
        'l0_ret_gn_g': 1.0 + nrm((RET_WIDTH,), 0.02),
        'l0_ret_gn_b': nrm((RET_WIDTH,), 0.02),
        'l0_w_out': nrm((LRU_WIDTH + RET_WIDTH, D), BETA * (LRU_WIDTH + RET_WIDTH) ** -0.5),
        'l1_mu': uni((6, D), 0.0, 1.0),
        'l1_w_rkv': nrm((3, D, D), D ** -0.5),
        'l1_w0': uni((D,), -6.5, -1.5),
        'l1_w1': nrm((D, LORA_W), D ** -0.5),
        'l1_w2': nrm((LORA_W, D), 0.1 * LORA_W ** -0.5),
        'l1_a0': nrm((D,), 0.1),
        'l1_a1': nrm((D, LORA_A), D ** -0.5),
        'l1_a2': nrm((LORA_A, D), 0.1 * LORA_A ** -0.5),
        'l1_g1': nrm((D, LORA_G), D ** -0.5),
        'l1_g2': nrm((LORA_G, D), LORA_G ** -0.5),
        'l1_k_k': 0.85 + nrm((D,), 0.05),
        'l1_k_a': 1.0 + nrm((D,), 0.05),
        'l1_r_k': nrm((RWKV_HEADS, RWKV_HEAD), 0.1),
        'l1_gn_g': 1.0 + nrm((D,), 0.02),
        'l1_gn_b': nrm((D,), 0.02),
        'l1_w_out': nrm((D, D), BETA * D ** -0.5),
    }


def reference(x_prompt, x_sample, mem_prompt, state_conv0, state_lru0, state_ret0, state_shift1, state_wkv1,
              cache_mem_k, cache_mem_v, ln_g, ln_b, ffn_up, ffn_down, xa_q, xa_k, xa_v, xa_o,
              l0_w_in, l0_conv_w, l0_conv_b, l0_lru_wa, l0_lru_ba, l0_lru_wx, l0_lru_bx, l0_lru_lambda,
              l0_ret_gn_g, l0_ret_gn_b, l0_w_out, l1_mu, l1_w_rkv, l1_w0, l1_w1, l1_w2, l1_a0, l1_a1, l1_a2,
              l1_g1, l1_g2, l1_k_k, l1_k_a, l1_r_k, l1_gn_g, l1_gn_b, l1_w_out):
    params_ab = (l0_w_in, l0_conv_w, l0_conv_b, l0_lru_wa, l0_lru_ba, l0_lru_wx, l0_lru_bx, l0_lru_lambda,
                 l0_ret_gn_g, l0_ret_gn_b, l0_w_out)
    params_c = (l1_mu, l1_w_rkv, l1_w0, l1_w1, l1_w2, l1_a0, l1_a1, l1_a2, l1_g1, l1_g2, l1_k_k, l1_k_a,
                l1_r_k, l1_gn_g, l1_gn_b, l1_w_out)
    mixer_params = (params_ab, params_c)

    Bp, Tp, _ = x_prompt.shape
    dt = x_prompt.dtype
    pos_p = jnp.arange(Tp, dtype=jnp.int32)
    mem_k_p = jnp.einsum('bmd,lde->lbme', mem_prompt, xa_k).reshape(DEPTH, Bp, N_MEM, MEM_HEADS, MEM_HEAD_DIM)
    mem_v_p = jnp.einsum('bmd,lde->lbme', mem_prompt, xa_v).reshape(DEPTH, Bp, N_MEM, MEM_HEADS, MEM_HEAD_DIM)
    zero_states = ((jnp.zeros((Bp, CONV_WIDTH - 1, LRU_WIDTH), dt), jnp.zeros((Bp, LRU_WIDTH), dt),
                    jnp.zeros((Bp, RET_HEADS, RET_HEAD_DIM, RET_HEAD_DIM), dt)),
                   (jnp.zeros((Bp, 1, D_MODEL), dt), jnp.zeros((Bp, RWKV_HEADS, RWKV_HEAD, RWKV_HEAD), dt)))
    y_prompt, st_p = run_trunk(x_prompt, pos_p, mem_k_p, mem_v_p, zero_states, ln_g, ln_b, ffn_up, ffn_down,
                               xa_q, xa_o, mixer_params)

    Ts = x_sample.shape[1]
    pos_s = PAST_LEN + jnp.arange(Ts, dtype=jnp.int32)
    sample_states = ((state_conv0, state_lru0, state_ret0), (state_shift1, state_wkv1))
    y_sample, st_s = run_trunk(x_sample, pos_s, cache_mem_k, cache_mem_v, sample_states, ln_g, ln_b, ffn_up,
                               ffn_down, xa_q, xa_o, mixer_params)

    (p_conv0, p_lru0, p_ret0), (p_shift1, p_wkv1) = st_p
    (s_conv0, s_lru0, s_ret0), (s_shift1, s_wkv1) = st_s
    return (y_prompt, y_sample, mem_k_p, mem_v_p, p_conv0, p_lru0, p_ret0, p_shift1, p_wkv1,
            s_conv0, s_lru0, s_ret0, s_shift1, s_wkv1)
```

```python
import functools
import math

import jax
import jax.numpy as jnp
from jax import lax
from jax.experimental import pallas as pl
from jax.experimental.pallas import tpu as pltpu

D_MODEL = 1024
DEPTH = 2
PAST_LEN = 4096
CHUNK = 64
N_MEM = 256
MEM_HEADS = 4
MEM_HEAD_DIM = D_MODEL // MEM_HEADS
D_FF = 2816
LRU_WIDTH = 512
LRU_BLOCKS = 8
LRU_BLOCK = LRU_WIDTH // LRU_BLOCKS
CONV_WIDTH = 4
LRU_C = 8.0
RET_HEADS = 4
RET_WIDTH = 512
RET_HEAD_DIM = RET_WIDTH // RET_HEADS
ROPE_BASE = 10000.0
RWKV_HEAD = 64
RWKV_HEADS = D_MODEL // RWKV_HEAD
LN_EPS = 1e-5
RWKV_GN_EPS = 64e-5
ALPHA = (2 * DEPTH) ** 0.25

F32 = jnp.float32
BF16 = jnp.bfloat16

V7X_SUBLANES = 8
V7X_LANES = 128
VMEM_LIMIT_BYTES = 56 * 1024 * 1024
FFN_CHUNK = 256
ROW_TILE = 512
SEL_WIDTH = V7X_LANES

_NT = (((1,), (1,)), ((), ()))
_TN = (((0,), (0,)), ((), ()))


def _dot(a, b):
    return jnp.dot(a.astype(BF16), b.astype(BF16), preferred_element_type=F32)


def _dot_nt(a, b):
    return lax.dot_general(a.astype(BF16), b.astype(BF16), _NT, preferred_element_type=F32)


def _dot_tn(a, b):
    return lax.dot_general(a.astype(BF16), b.astype(BF16), _TN, preferred_element_type=F32)


def _dot_split(z, w):
    hi = z.astype(BF16)
    lo = (z - hi.astype(F32)).astype(BF16)
    return (jnp.dot(hi, w, preferred_element_type=F32) + jnp.dot(lo, w, preferred_element_type=F32))


def _seg_sum(z, sel_ref, selt_ref):
    return _dot_split(_dot_split(z, sel_ref[...]), selt_ref[...])


def _layer_norm(y, g, b, eps):
    mu = jnp.mean(y, axis=-1, keepdims=True)
    yc = y - mu
    var = jnp.mean(yc * yc, axis=-1, keepdims=True)
    return yc * lax.rsqrt(var + eps) * g + b


def _sigmoid(x):
    return 1.0 / (1.0 + jnp.exp(-x))


def _softplus(x):
    return jnp.maximum(x, 0.0) + jnp.log1p(jnp.exp(-jnp.abs(x)))


def _gelu_tanh(x):
    return 0.5 * x * (1.0 + jnp.tanh(math.sqrt(2.0 / math.pi) * (x + 0.044715 * (x * x * x))))


def _params(sem):
    return pltpu.CompilerParams(dimension_semantics=sem, vmem_limit_bytes=VMEM_LIMIT_BYTES)


def _row_tile(n, pref=ROW_TILE):
    return pref if n % pref == 0 else n


def _ffn_kernel(x_ref, wg_ref, wu_ref, wd_ref, g_ref, b_ref, o_ref, *, nchunks):
    x = x_ref[...]
    xb = x.astype(BF16)
    acc = jnp.zeros(x.shape, F32)
    for c in range(nchunks):
        hg = jnp.dot(xb, wg_ref[c], preferred_element_type=F32)
        hu = jnp.dot(xb, wu_ref[c], preferred_element_type=F32)
        h = hg * _sigmoid(hg) * hu
        acc = acc + jnp.dot(h.astype(BF16), wd_ref[c], preferred_element_type=F32)
    o_ref[...] = _layer_norm(ALPHA * x + 0.5 * acc, g_ref[...], b_ref[...], LN_EPS)


def _ffn_postnorm(x, wg, wu, wd, g, b):
    n, d = x.shape
    nchunks, _, tf = wg.shape
    tm = _row_tile(n)
    const3 = lambda i: (0, 0, 0)
    return pl.pallas_call(
        functools.partial(_ffn_kernel, nchunks=nchunks),
        grid=(n // tm,),
        in_specs=[
            pl.BlockSpec((tm, d), lambda i: (i, 0)),
            pl.BlockSpec((nchunks, d, tf), const3),
            pl.BlockSpec((nchunks, d, tf), const3),
            pl.BlockSpec((nchunks, tf, d), const3),
            pl.BlockSpec((1, d), lambda i: (0, 0)),
            pl.BlockSpec((1, d), lambda i: (0, 0)),
        ],
        out_specs=pl.BlockSpec((tm, d), lambda i: (i, 0)),
        out_shape=jax.ShapeDtypeStruct((n, d), F32),
        compiler_params=_params(("parallel",)),
        name="ffn_postnorm",
    )(x, wg, wu, wd, g, b)


def _mm_kernel(x_ref, w_ref, o_ref):
    o_ref[...] = jnp.dot(x_ref[...].astype(BF16), w_ref[...], preferred_element_type=F32)


def _matmul(x, w):
    n, k = x.shape
    m = w.shape[1]
    tm = _row_tile(n)
    return pl.pallas_call(
        _mm_kernel,
        grid=(n // tm,),
        in_specs=[pl.BlockSpec((tm, k), lambda i: (i, 0)), pl.BlockSpec((k, m), lambda i: (0, 0))],
        out_specs=pl.BlockSpec((tm, m), lambda i: (i, 0)),
        out_shape=jax.ShapeDtypeStruct((n, m), F32),
        compiler_params=_params(("parallel",)),
        name="proj",
    )(x, w)


def _mm_stack_kernel(x_ref, w_ref, o_ref):
    o_ref[0] = jnp.dot(x_ref[...].astype(BF16), w_ref[0], preferred_element_type=F32)


def _matmul_stack(x, w):
    n, k = x.shape
    l, _, m = w.shape
    tm = _row_tile(n)
    return pl.pallas_call(
        _mm_stack_kernel,
        grid=(l, n // tm),
        in_specs=[pl.BlockSpec((tm, k), lambda j, i: (i, 0)), pl.BlockSpec((1, k, m), lambda j, i: (j, 0, 0))],
        out_specs=pl.BlockSpec((1, tm, m), lambda j, i: (j, i, 0)),
        out_shape=jax.ShapeDtypeStruct((l, n, m), F32),
        compiler_params=_params(("parallel", "parallel")),
        name="mem_proj",
    )(x, w)


def _ab_out_kernel(x_ref, ya_ref, yb_ref, wa_ref, wb_ref, g_ref, b_ref, o_ref):
    y = (jnp.dot(ya_ref[...].astype(BF16), wa_ref[...], preferred_element_type=F32)
         + jnp.dot(yb_ref[...].astype(BF16), wb_ref[...], preferred_element_type=F32))
    o_ref[...] = _layer_norm(ALPHA * x_ref[...] + y, g_ref[...], b_ref[...], LN_EPS)


def _ab_out_postnorm(x, ya, yb, wa, wb, g, b):
    n, d = x.shape
    ka = ya.shape[1]
    kb = yb.shape[1]
    tm = _row_tile(n)
    row = lambda i: (i, 0)
    const = lambda i: (0, 0)
    return pl.pallas_call(
        _ab_out_kernel,
        grid=(n // tm,),
        in_specs=[pl.BlockSpec((tm, d), row), pl.BlockSpec((tm, ka), row), pl.BlockSpec((tm, kb), row),
                  pl.BlockSpec((ka, d), const), pl.BlockSpec((kb, d), const),
                  pl.BlockSpec((1, d), const), pl.BlockSpec((1, d), const)],
        out_specs=pl.BlockSpec((tm, d), row),
        out_shape=jax.ShapeDtypeStruct((n, d), F32),
        compiler_params=_params(("parallel",)),
        name="ab_out_postnorm",
    )(x, ya, yb, wa, wb, g, b)


def _xattn_kernel(x_ref, k_ref, v_ref, wq_ref, wo_ref, g_ref, b_ref, o_ref):
    x = x_ref[0]
    q = jnp.dot(x.astype(BF16), wq_ref[...], preferred_element_type=F32)
    qb = q.astype(BF16)
    kb = k_ref[0].astype(BF16)
    vb = v_ref[0].astype(BF16)
    outs = []
    for h in range(MEM_HEADS):
        sl = slice(h * MEM_HEAD_DIM, (h + 1) * MEM_HEAD_DIM)
        s = lax.dot_general(qb[:, sl], kb[:, sl], _NT, preferred_element_type=F32) * (MEM_HEAD_DIM ** -0.5)
        m = jnp.max(s, axis=-1, keepdims=True)
        p = jnp.exp(s - m)
        p = p / jnp.sum(p, axis=-1, keepdims=True)
        outs.append(jnp.dot(p.astype(BF16), vb[:, sl], preferred_element_type=F32))
    o = jnp.concatenate(outs, axis=-1)
    y = jnp.dot(o.astype(BF16), wo_ref[...], preferred_element_type=F32)
    o_ref[0] = _layer_norm(ALPHA * x + y, g_ref[...], b_ref[...], LN_EPS)


def _xattn_postnorm(x, mem_k, mem_v, wq, wo, g, b):
    bsz, t, d = x.shape
    tm = _row_tile(t)
    const = lambda bi, ti: (0, 0)
    return pl.pallas_call(
        _xattn_kernel,
        grid=(bsz, t // tm),
        in_specs=[pl.BlockSpec((1, tm, d), lambda bi, ti: (bi, ti, 0)),
                  pl.BlockSpec((1, N_MEM, d), lambda bi, ti: (bi, 0, 0)),
                  pl.BlockSpec((1, N_MEM, d), lambda bi, ti: (bi, 0, 0)),
                  pl.BlockSpec((d, d), const), pl.BlockSpec((d, d), const),
                  pl.BlockSpec((1, d), const), pl.BlockSpec((1, d), const)],
        out_specs=pl.BlockSpec((1, tm, d), lambda bi, ti: (bi, ti, 0)),
        out_shape=jax.ShapeDtypeStruct((bsz, t, d), F32),
        compiler_params=_params(("parallel", "parallel")),
        name="xattn_postnorm",
    )(x, mem_k, mem_v, wq, wo, g, b)


def _lru_kernel(xa_ref, ga_ref, cbuf_ref, h0_ref, cw_ref, cb_ref, wa_ref, ba_ref, wx_ref, bx_ref, lam_ref,
                ya_ref, nbuf_ref, hlast_ref, prev_s, h_s, a_s, u_s, *, tc):
    t = pl.program_id(1)

    @pl.when(t == 0)
    def _():
        prev_s[...] = jnp.zeros(prev_s.shape, F32)
        prev_s[V7X_SUBLANES - (CONV_WIDTH - 1):, :] = cbuf_ref[0]
        h_s[...] = h0_ref[0]

    xa = xa_ref[0]
    ext = jnp.concatenate([prev_s[...], xa], axis=0)
    xc = cb_ref[...] + xa * cw_ref[CONV_WIDTH - 1:CONV_WIDTH, :]
    for s in range(1, CONV_WIDTH):
        shifted = pltpu.roll(ext, s, 0)[V7X_SUBLANES:, :]
        xc = xc + shifted * cw_ref[CONV_WIDTH - 1 - s:CONV_WIDTH - s, :]
    prev_s[...] = xa[tc - V7X_SUBLANES:, :]
    nbuf_ref[0] = xa[tc - (CONV_WIDTH - 1):, :]

    xcb = xc.astype(BF16)
    r = _sigmoid(jnp.dot(xcb, wa_ref[...], preferred_element_type=F32) + ba_ref[...])
    i = _sigmoid(jnp.dot(xcb, wx_ref[...], preferred_element_type=F32) + bx_ref[...])
    log_a = (-LRU_C * _softplus(-lam_ref[...])) * r
    a_s[...] = jnp.exp(log_a)
    th = jnp.tanh(log_a)
    u_s[...] = jnp.sqrt(-2.0 * th / (1.0 - th)) * (i * xc)

    def body(j, h):
        base = pl.multiple_of(j * V7X_SUBLANES, V7X_SUBLANES)
        a8 = a_s[pl.ds(base, V7X_SUBLANES), :]
        u8 = u_s[pl.ds(base, V7X_SUBLANES), :]
        rows = []
        for k in range(V7X_SUBLANES):
            h = a8[k:k + 1, :] * h + u8[k:k + 1, :]
            rows.append(h)
        u_s[pl.ds(base, V7X_SUBLANES), :] = jnp.concatenate(rows, axis=0)
        return h

    h = lax.fori_loop(0, tc // V7X_SUBLANES, body, h_s[...])
    h_s[...] = h
    hlast_ref[0] = h
    ya_ref[0] = u_s[...] * _gelu_tanh(ga_ref[0])


def _lru_branch(proj, conv_buf, h0, cw, cb, wa, ba, wx, bx, lam):
    bsz, t, _ = proj.shape
    w = LRU_WIDTH
    tc = _row_tile(t)
    const = lambda bi, ti: (0, 0)
    per_b = lambda bi, ti: (bi, 0, 0)
    return pl.pallas_call(
        functools.partial(_lru_kernel, tc=tc),
        grid=(bsz, t // tc),
        in_specs=[pl.BlockSpec((1, tc, w), lambda bi, ti: (bi, ti, 0)),
                  pl.BlockSpec((1, tc, w), lambda bi, ti: (bi, ti, 1)),
                  pl.BlockSpec((1, CONV_WIDTH - 1, w), per_b),
                  pl.BlockSpec((1, 1, w), per_b),
                  pl.BlockSpec((CONV_WIDTH, w), const), pl.BlockSpec((1, w), const),
                  pl.BlockSpec((w, w), const), pl.BlockSpec((1, w), const),
                  pl.BlockSpec((w, w), const), pl.BlockSpec((1, w), const),
                  pl.BlockSpec((1, w), const)],
        out_specs=[pl.BlockSpec((1, tc, w), lambda bi, ti: (bi, ti, 0)),
                   pl.BlockSpec((1, CONV_WIDTH - 1, w), per_b),
                   pl.BlockSpec((1, 1, w), per_b)],
        out_shape=[jax.ShapeDtypeStruct((bsz, t, w), F32),
                   jax.ShapeDtypeStruct((bsz, CONV_WIDTH - 1, w), F32),
                   jax.ShapeDtypeStruct((bsz, 1, w), F32)],
        scratch_shapes=[pltpu.VMEM((V7X_SUBLANES, w), F32), pltpu.VMEM((1, w), F32),
                        pltpu.VMEM((tc, w), F32), pltpu.VMEM((tc, w), F32)],
        compiler_params=_params(("parallel", "arbitrary")),
        name="lru_branch",
    )(proj, proj, conv_buf, h0, cw, cb, wa, ba, wx, bx, lam)


def _ret_kernel(q_ref, k_ref, v_ref, g_ref, cos_ref, sin_ref, s0_ref, gng_ref, gnb_ref,
                yb_ref, slast_ref, s_s, *, c):
    t = pl.program_id(1)

    @pl.when(t == 0)
    def _():
        s_s[...] = s0_ref[0]

    q = q_ref[0]
    k = k_ref[0]
    v = v_ref[0]
    g = g_ref[0]
    cos2 = cos_ref[...]
    sin2 = sin_ref[...]
    half = RET_HEAD_DIM // 2
    ri = lax.broadcasted_iota(jnp.int32, (c, c), 0)
    ci = lax.broadcasted_iota(jnp.int32, (c, c), 1)
    diff = (ri - ci).astype(F32)
    it = lax.broadcasted_iota(jnp.int32, (c, 1), 0).astype(F32)
    for h in range(RET_HEADS):
        log_g = math.log1p(-(2.0 ** (-5.0 - h)))
        sl = slice(h * RET_HEAD_DIM, (h + 1) * RET_HEAD_DIM)
        qh = q[:, sl]
        kh = k[:, sl]
        vh = v[:, sl]
        qr = qh * cos2 + pltpu.roll(qh, half, 1) * sin2
        kr = (kh * cos2 + pltpu.roll(kh, half, 1) * sin2) * (RET_HEAD_DIM ** -0.5)
        dmask = jnp.where(diff >= 0, jnp.exp(log_g * jnp.maximum(diff, 0.0)), 0.0)
        scores = _dot_nt(qr, kr) * dmask
        inner = _dot(scores, vh)
        s_prev = s_s[h]
        cross = _dot(qr, s_prev) * jnp.exp(log_g * (it + 1.0))
        o = inner + cross
        zeta = jnp.exp(log_g * (c - 1.0 - it))
        s_s[h] = math.exp(log_g * c) * s_prev + _dot_tn(kr * zeta, vh)
        on = _layer_norm(o, gng_ref[:, sl], gnb_ref[:, sl], LN_EPS)
        gh = g[:, sl]
        yb_ref[0, :, sl] = on * (gh * _sigmoid(gh))
    slast_ref[0] = s_s[...]


def _ret_branch(proj, cos2, sin2, s0, gn_g, gn_b, c):
    bsz, t, _ = proj.shape
    w = RET_WIDTH
    dk = RET_HEAD_DIM
    const = lambda bi, ti: (0, 0)
    per_b = lambda bi, ti: (bi, 0, 0, 0)
    col = lambda j: (lambda bi, ti: (bi, ti, j))
    return pl.pallas_call(
        functools.partial(_ret_kernel, c=c),
        grid=(bsz, t // c),
        in_specs=[pl.BlockSpec((1, c, w), col(2)), pl.BlockSpec((1, c, w), col(3)),
                  pl.BlockSpec((1, c, w), col(4)), pl.BlockSpec((1, c, w), col(5)),
                  pl.BlockSpec((c, dk), lambda bi, ti: (ti, 0)), pl.BlockSpec((c, dk), lambda bi, ti: (ti, 0)),
                  pl.BlockSpec((1, RET_HEADS, dk, dk), per_b),
                  pl.BlockSpec((1, w), const), pl.BlockSpec((1, w), const)],
        out_specs=[pl.BlockSpec((1, c, w), lambda bi, ti: (bi, ti, 0)),
                   pl.BlockSpec((1, RET_HEADS, dk, dk), per_b)],
        out_shape=[jax.ShapeDtypeStruct((bsz, t, w), F32),
                   jax.ShapeDtypeStruct((bsz, RET_HEADS, dk, dk), F32)],
        scratch_shapes=[pltpu.VMEM((RET_HEADS, dk, dk), F32)],
        compiler_params=_params(("parallel", "arbitrary")),
        name="ret_branch",
    )(proj, proj, proj, proj, cos2, sin2, s0, gn_g, gn_b)


def _rwkv_proj_kernel(x_ref, sh_ref, mu_ref, wrkv_ref, w0_ref, w1_ref, w2_ref, a0_ref, a1_ref, a2_ref,
                      g1_ref, g2_ref, kk_ref, ka_ref, rk_ref, sel_ref, selt_ref,
                      r_o, lw_o, k_o, v_o, kk_o, b_o, g_o, bonus_o, prev_s, *, tm):
    t = pl.program_id(1)

    @pl.when(t == 0)
    def _():
        prev_s[...] = jnp.zeros(prev_s.shape, F32)
        prev_s[V7X_SUBLANES - 1:, :] = sh_ref[0]

    x = x_ref[0]
    ext = jnp.concatenate([prev_s[...], x], axis=0)
    xp = pltpu.roll(ext, 1, 0)[V7X_SUBLANES:, :]
    prev_s[...] = x[tm - V7X_SUBLANES:, :]
    dx = xp - x

    def mix(p):
        return (x + dx * mu_ref[p:p + 1, :]).astype(BF16)

    r = jnp.dot(mix(0), wrkv_ref[0], preferred_element_type=F32)
    k = jnp.dot(mix(1), wrkv_ref[1], preferred_element_type=F32)
    v = jnp.dot(mix(2), wrkv_ref[2], preferred_element_type=F32)
    wl = _dot(jnp.tanh(jnp.dot(mix(3), w1_ref[...], preferred_element_type=F32)), w2_ref[...])
    w = -_softplus(-(w0_ref[...] + wl)) - 0.5
    al = _dot(jnp.dot(mix(4), a1_ref[...], preferred_element_type=F32), a2_ref[...])
    iclr = _sigmoid(a0_ref[...] + al)
    gate = _dot(_sigmoid(jnp.dot(mix(5), g1_ref[...], preferred_element_type=F32)), g2_ref[...])

    kk = k * kk_ref[...]
    norm = jnp.sqrt(_seg_sum(kk * kk, sel_ref, selt_ref))
    kk = kk / jnp.maximum(norm, 1e-12)
    k2 = k * (1.0 + (iclr - 1.0) * ka_ref[...])

    r_o[0] = r
    lw_o[0] = -jnp.exp(w)
    k_o[0] = k2
    v_o[0] = v
    kk_o[0] = kk
    b_o[0] = kk * iclr
    g_o[0] = gate
    bonus_o[0] = _seg_sum(r * k2 * rk_ref[...], sel_ref, selt_ref) * v


def _rwkv_proj(x, shift, mu, wrkv, w0, w1, w2, a0, a1, a2, g1, g2, k_k, k_a, r_k, sel, selt):
    bsz, t, d = x.shape
    tm = _row_tile(t, 256)
    const2 = lambda bi, ti: (0, 0)
    const3 = lambda bi, ti: (0, 0, 0)
    full2 = lambda a: pl.BlockSpec(a.shape, const2)
    tile = pl.BlockSpec((1, tm, d), lambda bi, ti: (bi, ti, 0))
    out = jax.ShapeDtypeStruct((bsz, t, d), F32)
    return pl.pallas_call(
        functools.partial(_rwkv_proj_kernel, tm=tm),
        grid=(bsz, t // tm),
        in_specs=[tile, pl.BlockSpec((1, 1, d), lambda bi, ti: (bi, 0, 0)), full2(mu),
                  pl.BlockSpec(wrkv.shape, const3), full2(w0), full2(w1), full2(w2), full2(a0), full2(a1),
                  full2(a2), full2(g1), full2(g2), full2(k_k), full2(k_a), full2(r_k), full2(sel), full2(selt)],
        out_specs=[tile] * 8,
        out_shape=[out] * 8,
        scratch_shapes=[pltpu.VMEM((V7X_SUBLANES, d), F32)],
        compiler_params=_params(("parallel", "arbitrary")),
        name="rwkv_proj",
    )(x, shift, mu, wrkv, w0, w1, w2, a0, a1, a2, g1, g2, k_k, k_a, r_k, sel, selt)


def _wkv_kernel(r_ref, lw_ref, k_ref, v_ref, kk_ref, b_ref, s0_ref, o_ref, slast_ref, s_s, *, c):
    t = pl.program_id(1)

    @pl.when(t == 0)
    def _():
        s_s[...] = s0_ref[0]

    ri = lax.broadcasted_iota(jnp.int32, (c, c), 0)
    ci = lax.broadcasted_iota(jnp.int32, (c, c), 1)
    lower = ri >= ci
    strict = ri > ci
    eye = jnp.where(ri == ci, 1.0, 0.0).astype(F32)
    n = RWKV_HEAD
    eye_n = (lax.broadcasted_iota(jnp.int32, (n, n), 0) == lax.broadcasted_iota(jnp.int32, (n, n), 1))

    lw = lw_ref[0]
    cum = jnp.dot(jnp.where(lower, 1.0, 0.0).astype(F32), lw, preferred_element_type=F32,
                  precision=lax.Precision.HIGHEST)
    cum_last = cum[c - 1:c, :]
    e_neg = jnp.exp(-cum)
    e_end = jnp.exp(cum_last - cum)
    kk = kk_ref[0]
    b = b_ref[0]
    k = k_ref[0]
    a_t = kk * jnp.exp(cum - lw)
    b_t = b * e_neg
    k_t = k * e_neg
    r_t = r_ref[0] * jnp.exp(cum)
    b_e = b * e_end
    k_e = k * e_end
    d_end = jnp.exp(cum_last)
    v = v_ref[0]

    for h in range(RWKV_HEADS):
        sl = slice(h * n, (h + 1) * n)
        ah = a_t[:, sl]
        vh = v[:, sl]
        x2 = jnp.concatenate([ah, r_t[:, sl]], axis=0)
        mb = _dot_nt(x2, b_t[:, sl])
        mk = _dot_nt(x2, k_t[:, sl])
        m_ab = jnp.where(strict, mb[:c], 0.0)
        p_rb = jnp.where(lower, mb[c:], 0.0)
        m_ak = jnp.where(strict, mk[:c], 0.0)
        p_rk = jnp.where(lower, mk[c:], 0.0)
        pw = -m_ab
        tinv = eye + pw
        for _ in range(int(math.log2(c)) - 1):
            pw = _dot(pw, pw)
            tinv = tinv + _dot(tinv, pw)
        w_h = _dot(m_ak, vh)
        a_hat = _dot(tinv, ah)
        w_hat = _dot(tinv, w_h)
        r_hat = r_t[:, sl] - _dot(p_rb, a_hat)
        o0 = _dot(p_rk, vh) - _dot(p_rb, w_hat)
        gmat = jnp.where(eye_n, d_end[:, sl], 0.0) - _dot_tn(a_hat, b_e[:, sl])
        hmat = _dot_tn(vh, k_e[:, sl]) - _dot_tn(w_hat, b_e[:, sl])
        s_prev = s_s[h]
        o_ref[0, :, sl] = o0 + _dot_nt(r_hat, s_prev)
        s_s[h] = _dot(s_prev, gmat) + hmat
    slast_ref[0] = s_s[...]


def _wkv_scan(r, lw, k, v, kk, b, s0, c):
    bsz, t, d = r.shape
    n = RWKV_HEAD
    tile = pl.BlockSpec((1, c, d), lambda bi, ti: (bi, ti, 0))
    st = pl.BlockSpec((1, RWKV_HEADS, n, n), lambda bi, ti: (bi, 0, 0, 0))
    return pl.pallas_call(
        functools.partial(_wkv_kernel, c=c),
        grid=(bsz, t // c),
        in_specs=[tile] * 6 + [st],
        out_specs=[tile, st],
        out_shape=[jax.ShapeDtypeStruct((bsz, t, d), F32), jax.ShapeDtypeStruct(s0.shape, F32)],
        scratch_shapes=[pltpu.VMEM((RWKV_HEADS, n, n), F32)],
        compiler_params=_params(("parallel", "arbitrary")),
        name="wkv_scan",
    )(r, lw, k, v, kk, b, s0)


def _wkv_out_kernel(x_ref, o_ref, bonus_ref, gate_ref, gng_ref, gnb_ref, sel_ref, selt_ref, w_ref,
                    g_ref, b_ref, y_ref):
    o = o_ref[...]
    inv_n = 1.0 / RWKV_HEAD
    mu = _seg_sum(o, sel_ref, selt_ref) * inv_n
    oc = o - mu
    var = _seg_sum(oc * oc, sel_ref, selt_ref) * inv_n
    on = oc * lax.rsqrt(var + RWKV_GN_EPS) * gng_ref[...] + gnb_ref[...]
    z = (on + bonus_ref[...]) * gate_ref[...]
    y = jnp.dot(z.astype(BF16), w_ref[...], preferred_element_type=F32)
    y_ref[...] = _layer_norm(ALPHA * x_ref[...] + y, g_ref[...], b_ref[...], LN_EPS)


def _wkv_out_postnorm(x, o, bonus, gate, gn_g, gn_b, sel, selt, w, g, b):
    n, d = x.shape
    tm = _row_tile(n)
    row = pl.BlockSpec((tm, d), lambda i: (i, 0))
    const = lambda i: (0, 0)
    full = lambda a: pl.BlockSpec(a.shape, const)
    return pl.pallas_call(
        _wkv_out_kernel,
        grid=(n // tm,),
        in_specs=[row, row, row, row, full(gn_g), full(gn_b), full(sel), full(selt), full(w), full(g), full(b)],
        out_specs=row,
        out_shape=jax.ShapeDtypeStruct((n, d), F32),
        compiler_params=_params(("parallel",)),
        name="wkv_out_postnorm",
    )(x, o, bonus, gate, gn_g, gn_b, sel, selt, w, g, b)


def _prep_weights(ln_g, ln_b, ffn_up, ffn_down, xa_q, xa_o, l0_w_in, l0_conv_w, l0_conv_b, l0_lru_wa,
                  l0_lru_ba, l0_lru_wx, l0_lru_bx, l0_lru_lambda, l0_ret_gn_g, l0_ret_gn_b, l0_w_out,
                  l1_mu, l1_w_rkv, l1_w0, l1_w1, l1_w2, l1_a0, l1_a1, l1_a2, l1_g1, l1_g2, l1_k_k, l1_k_a,
                  l1_r_k, l1_gn_g, l1_gn_b, l1_w_out):
    d = D_MODEL
    nchunks = D_FF // FFN_CHUNK
    row = lambda a: a.reshape(1, -1).astype(F32)

    def chunk_cols(w):
        return w.reshape(d, nchunks, FFN_CHUNK).transpose(1, 0, 2).astype(BF16)

    ffn = [[(chunk_cols(ffn_up[l, j, :, :D_FF]), chunk_cols(ffn_up[l, j, :, D_FF:]),
             ffn_down[l, j].reshape(nchunks, FFN_CHUNK, d).astype(BF16)) for j in range(2)]
           for l in range(DEPTH)]

    def block_diag(w):
        eye = jnp.eye(LRU_BLOCKS, dtype=w.dtype)
        return jnp.einsum('gij,gh->gihj', w, eye).reshape(LRU_WIDTH, LRU_WIDTH).astype(BF16)

    head_of_col = jnp.arange(d) // RWKV_HEAD
    sel = (head_of_col[:, None] == jnp.arange(SEL_WIDTH)[None, :]).astype(BF16)
    return dict(
        ln_g=ln_g, ln_b=ln_b, ffn=ffn,
        xa_q=xa_q.astype(BF16), xa_o=xa_o.astype(BF16),
        w_in=l0_w_in.astype(BF16), conv_w=l0_conv_w, conv_b=row(l0_conv_b),
        lru_wa=block_diag(l0_lru_wa), lru_ba=row(l0_lru_ba), lru_wx=block_diag(l0_lru_wx),
        lru_bx=row(l0_lru_bx), lru_lam=row(l0_lru_lambda),
        ret_gn_g=row(l0_ret_gn_g), ret_gn_b=row(l0_ret_gn_b),
        w_out_a=l0_w_out[:LRU_WIDTH].astype(BF16), w_out_b=l0_w_out[LRU_WIDTH:].astype(BF16),
        mu=l1_mu, w_rkv=l1_w_rkv.astype(BF16), w0=row(l1_w0), w1=l1_w1.astype(BF16), w2=l1_w2.astype(BF16),
        a0=row(l1_a0), a1=l1_a1.astype(BF16), a2=l1_a2.astype(BF16), g1=l1_g1.astype(BF16),
        g2=l1_g2.astype(BF16), k_k=row(l1_k_k), k_a=row(l1_k_a), r_k=row(l1_r_k),
        gn_g=row(l1_gn_g), gn_b=row(l1_gn_b), w_out_c=l1_w_out.astype(BF16), sel=sel, selt=sel.T,
    )


def _rotary_tables(pos):
    half = RET_HEAD_DIM // 2
    inv_freq = ROPE_BASE ** (-jnp.arange(half, dtype=F32) / half)
    ang = pos.astype(F32)[:, None] * inv_freq[None, :]
    cos = jnp.cos(ang)
    sin = jnp.sin(ang)
    return jnp.concatenate([cos, cos], axis=-1), jnp.concatenate([-sin, sin], axis=-1)


def _run_trunk(x, pos, mem_k, mem_v, states, p):
    bsz, t, d = x.shape
    n = bsz * t
    (conv_buf, h0, s_ret), (shift, s_wkv) = states
    lng = lambda l, j: p['ln_g'][l, j].reshape(1, d)
    lnb = lambda l, j: p['ln_b'][l, j].reshape(1, d)
    flat = lambda a: a.reshape(n, a.shape[-1])
    chunk = min(CHUNK, t)

    x = _ffn_postnorm(flat(x), *p['ffn'][0][0], lng(0, 0), lnb(0, 0))
    proj = _matmul(x, p['w_in']).reshape(bsz, t, -1)
    ya, new_buf, h_last = _lru_branch(proj, conv_buf, h0.reshape(bsz, 1, LRU_WIDTH), p['conv_w'], p['conv_b'],
                                      p['lru_wa'], p['lru_ba'], p['lru_wx'], p['lru_bx'], p['lru_lam'])
    cos2, sin2 = _rotary_tables(pos)
    yb, s_ret_new = _ret_branch(proj, cos2, sin2, s_ret, p['ret_gn_g'], p['ret_gn_b'], chunk)
    x = _ab_out_postnorm(x, flat(ya), flat(yb), p['w_out_a'], p['w_out_b'], lng(0, 1), lnb(0, 1))
    x = _xattn_postnorm(x.reshape(bsz, t, d), mem_k[0], mem_v[0], p['xa_q'][0], p['xa_o'][0],
                        lng(0, 2), lnb(0, 2))
    x = _ffn_postnorm(flat(x), *p['ffn'][0][1], lng(0, 3), lnb(0, 3))

    x = _ffn_postnorm(x, *p['ffn'][1][0], lng(1, 0), lnb(1, 0))
    x3 = x.reshape(bsz, t, d)
    r, lw, k2, v, kk, b, gate, bonus = _rwkv_proj(
        x3, shift, p['mu'], p['w_rkv'], p['w0'], p['w1'], p['w2'], p['a0'], p['a1'], p['a2'], p['g1'], p['g2'],
        p['k_k'], p['k_a'], p['r_k'], p['sel'], p['selt'])
    o, s_wkv_new = _wkv_scan(r, lw, k2, v, kk, b, s_wkv, chunk)
    new_shift = x3[:, t - 1:, :]
    x = _wkv_out_postnorm(x, flat(o), flat(bonus), flat(gate), p['gn_g'], p['gn_b'], p['sel'], p['selt'],
                          p['w_out_c'], lng(1, 1), lnb(1, 1))
    x = _xattn_postnorm(x.reshape(bsz, t, d), mem_k[1], mem_v[1], p['xa_q'][1], p['xa_o'][1],
                        lng(1, 2), lnb(1, 2))
    x = _ffn_postnorm(flat(x), *p['ffn'][1][1], lng(1, 3), lnb(1, 3))
    new_states = ((new_buf, h_last.reshape(bsz, LRU_WIDTH), s_ret_new), (new_shift, s_wkv_new))
    return x.reshape(bsz, t, d), new_states


def kernel(x_prompt, x_sample, mem_prompt, state_conv0, state_lru0, state_ret0, state_shift1, state_wkv1,
           cache_mem_k, cache_mem_v, ln_g, ln_b, ffn_up, ffn_down, xa_q, xa_k, xa_v, xa_o,
           l0_w_in, l0_conv_w, l0_conv_b, l0_lru_wa, l0_lru_ba, l0_lru_wx, l0_lru_bx, l0_lru_lambda,
           l0_ret_gn_g, l0_ret_gn_b, l0_w_out, l1_mu, l1_w_rkv, l1_w0, l1_w1, l1_w2, l1_a0, l1_a1, l1_a2,
           l1_g1, l1_g2, l1_k_k, l1_k_a, l1_r_k, l1_gn_g, l1_gn_b, l1_w_out):
    d = D_MODEL
    p = _prep_weights(ln_g, ln_b, ffn_up, ffn_down, xa_q, xa_o, l0_w_in, l0_conv_w, l0_conv_b, l0_lru_wa,
                      l0_lru_ba, l0_lru_wx, l0_lru_bx, l0_lru_lambda, l0_ret_gn_g, l0_ret_gn_b, l0_w_out,
                      l1_mu, l1_w_rkv, l1_w0, l1_w1, l1_w2, l1_a0, l1_a1, l1_a2, l1_g1, l1_g2, l1_k_k, l1_k_a,
                      l1_r_k, l1_gn_g, l1_gn_b, l1_w_out)

    bp, tp, _ = x_prompt.shape
    dt = x_prompt.dtype
    mem_flat = mem_prompt.reshape(bp * N_MEM, d)
    mem_k_p = _matmul_stack(mem_flat, xa_k.astype(BF16)).reshape(DEPTH, bp, N_MEM, d)
    mem_v_p = _matmul_stack(mem_flat, xa_v.astype(BF16)).reshape(DEPTH, bp, N_MEM, d)
    zero_states = ((jnp.zeros((bp, CONV_WIDTH - 1, LRU_WIDTH), dt), jnp.zeros((bp, LRU_WIDTH), dt),
                    jnp.zeros((bp, RET_HEADS, RET_HEAD_DIM, RET_HEAD_DIM), dt)),
                   (jnp.zeros((bp, 1, d), dt), jnp.zeros((bp, RWKV_HEADS, RWKV_HEAD, RWKV_HEAD), dt)))
    y_prompt, st_p = _run_trunk(x_prompt, jnp.arange(tp, dtype=jnp.int32), mem_k_p, mem_v_p, zero_states, p)

    bs, ts, _ = x_sample.shape
    pos_s = PAST_LEN + jnp.arange(ts, dtype=jnp.int32)
    sample_states = ((state_conv0, state_lru0, state_ret0), (state_shift1, state_wkv1))
    y_sample, st_s = _run_trunk(x_sample, pos_s, cache_mem_k.reshape(DEPTH, bs, N_MEM, d),
                                cache_mem_v.reshape(DEPTH, bs, N_MEM, d), sample_states, p)

    (p_conv0, p_lru0, p_ret0), (p_shift1, p_wkv1) = st_p
    (s_conv0, s_lru0, s_ret0), (s_shift1, s_wkv1) = st_s
    mem_shape = (DEPTH, bp, N_MEM, MEM_HEADS, MEM_HEAD_DIM)
    return (y_prompt, y_sample, mem_k_p.reshape(mem_shape), mem_v_p.reshape(mem_shape),
            p_conv0, p_lru0, p_ret0, p_shift1, p_wkv1, s_conv0, s_lru0, s_ret0, s_shift1, s_wkv1)
```

```python
import functools
import math

import jax
import jax.numpy as jnp
from jax import lax
from jax.experimental import pallas as pl
from jax.experimental.pallas import tpu as pltpu

D_MODEL = 1024
DEPTH = 2
PAST_LEN = 4096
CHUNK = 64
N_MEM = 256
MEM_HEADS = 4
MEM_HEAD_DIM = D_MODEL // MEM_HEADS
D_FF = 2816
LRU_WIDTH = 512
LRU_BLOCKS = 8
LRU_BLOCK = LRU_WIDTH // LRU_BLOCKS
CONV_WIDTH = 4
LRU_C = 8.0
RET_HEADS = 4
RET_WIDTH = 512
RET_HEAD_DIM = RET_WIDTH // RET_HEADS
ROPE_BASE = 10000.0
RWKV_HEAD = 64
RWKV_HEADS = D_MODEL // RWKV_HEAD
LN_EPS = 1e-5
RWKV_GN_EPS = 64e-5
ALPHA = (2 * DEPTH) ** 0.25

F32 = jnp.float32
BF16 = jnp.bfloat16

V7X_SUBLANES = 8
V7X_LANES = 128
VMEM_LIMIT_BYTES = 56 * 1024 * 1024
FFN_CHUNK = 256
ROW_TILE = 512
SEL_WIDTH = V7X_LANES
WKV_GROUP = 4
WKV_CHUNKS_PER_STEP = 2
RET_CHUNKS_PER_STEP = 8

_NT = (((1,), (1,)), ((), ()))
_TN = (((0,), (0,)), ((), ()))


def _dot(a, b):
    return jnp.dot(a.astype(BF16), b.astype(BF16), preferred_element_type=F32)


def _dot_nt(a, b):
    return lax.dot_general(a.astype(BF16), b.astype(BF16), _NT, preferred_element_type=F32)


def _dot_tn(a, b):
    return lax.dot_general(a.astype(BF16), b.astype(BF16), _TN, preferred_element_type=F32)


def _dot_split(z, w):
    hi = z.astype(BF16)
    lo = (z - hi.astype(F32)).astype(BF16)
    return (jnp.dot(hi, w, preferred_element_type=F32) + jnp.dot(lo, w, preferred_element_type=F32))


def _seg_sum(z, sel_ref, selt_ref):
    sums = jnp.dot(z.astype(BF16), sel_ref[...], preferred_element_type=F32)
    return _dot_split(sums, selt_ref[...])


def _layer_norm(y, g, b, eps):
    mu = jnp.mean(y, axis=-1, keepdims=True)
    yc = y - mu
    var = jnp.mean(yc * yc, axis=-1, keepdims=True)
    return yc * lax.rsqrt(var + eps) * g + b


def _sigmoid(x):
    return 1.0 / (1.0 + jnp.exp(-x))


def _softplus(x):
    return jnp.maximum(x, 0.0) + jnp.log1p(jnp.exp(-jnp.abs(x)))


def _gelu_tanh(x):
    return 0.5 * x * (1.0 + jnp.tanh(math.sqrt(2.0 / math.pi) * (x + 0.044715 * (x * x * x))))


def _params(sem):
    return pltpu.CompilerParams(dimension_semantics=sem, vmem_limit_bytes=VMEM_LIMIT_BYTES)


def _row_tile(n, pref=ROW_TILE):
    return pref if n % pref == 0 else n


def _ffn_kernel(x_ref, wg_ref, wu_ref, wd_ref, g_ref, b_ref, o_ref, *, nchunks):
    x = x_ref[...]
    xb = x.astype(BF16)
    acc = jnp.zeros(x.shape, F32)
    for c in range(nchunks):
        hg = jnp.dot(xb, wg_ref[c], preferred_element_type=F32)
        hu = jnp.dot(xb, wu_ref[c], preferred_element_type=F32)
        h = hg * _sigmoid(hg) * hu
        acc = acc + jnp.dot(h.astype(BF16), wd_ref[c], preferred_element_type=F32)
    o_ref[...] = _layer_norm(ALPHA * x + 0.5 * acc, g_ref[...], b_ref[...], LN_EPS)


def _ffn_postnorm(x, wg, wu, wd, g, b):
    n, d = x.shape
    nchunks, _, tf = wg.shape
    tm = _row_tile(n)
    const3 = lambda i: (0, 0, 0)
    return pl.pallas_call(
        functools.partial(_ffn_kernel, nchunks=nchunks),
        grid=(n // tm,),
        in_specs=[
            pl.BlockSpec((tm, d), lambda i: (i, 0)),
            pl.BlockSpec((nchunks, d, tf), const3),
            pl.BlockSpec((nchunks, d, tf), const3),
            pl.BlockSpec((nchunks, tf, d), const3),
            pl.BlockSpec((1, d), lambda i: (0, 0)),
            pl.BlockSpec((1, d), lambda i: (0, 0)),
        ],
        out_specs=pl.BlockSpec((tm, d), lambda i: (i, 0)),
        out_shape=jax.ShapeDtypeStruct((n, d), F32),
        compiler_params=_params(("parallel",)),
        name="ffn_postnorm",
    )(x, wg, wu, wd, g, b)


def _mm_kernel(x_ref, w_ref, o_ref):
    o_ref[...] = jnp.dot(x_ref[...].astype(BF16), w_ref[...], preferred_element_type=F32)


def _matmul(x, w):
    n, k = x.shape
    m = w.shape[1]
    tm = _row_tile(n)
    return pl.pallas_call(
        _mm_kernel,
        grid=(n // tm,),
        in_specs=[pl.BlockSpec((tm, k), lambda i: (i, 0)), pl.BlockSpec((k, m), lambda i: (0, 0))],
        out_specs=pl.BlockSpec((tm, m), lambda i: (i, 0)),
        out_shape=jax.ShapeDtypeStruct((n, m), F32),
        compiler_params=_params(("parallel",)),
        name="proj",
    )(x, w)


def _mm_stack_kernel(x_ref, w_ref, o_ref):
    o_ref[0] = jnp.dot(x_ref[...].astype(BF16), w_ref[0], preferred_element_type=F32)


def _matmul_stack(x, w):
    n, k = x.shape
    l, _, m = w.shape
    tm = _row_tile(n)
    return pl.pallas_call(
        _mm_stack_kernel,
        grid=(l, n // tm),
        in_specs=[pl.BlockSpec((tm, k), lambda j, i: (i, 0)), pl.BlockSpec((1, k, m), lambda j, i: (j, 0, 0))],
        out_specs=pl.BlockSpec((1, tm, m), lambda j, i: (j, i, 0)),
        out_shape=jax.ShapeDtypeStruct((l, n, m), F32),
        compiler_params=_params(("parallel", "parallel")),
        name="mem_proj",
    )(x, w)


def _ab_out_kernel(x_ref, ya_ref, yb_ref, wa_ref, wb_ref, g_ref, b_ref, o_ref):
    y = (jnp.dot(ya_ref[...].astype(BF16), wa_ref[...], preferred_element_type=F32)
         + jnp.dot(yb_ref[...].astype(BF16), wb_ref[...], preferred_element_type=F32))
    o_ref[...] = _layer_norm(ALPHA * x_ref[...] + y, g_ref[...], b_ref[...], LN_EPS)


def _ab_out_postnorm(x, ya, yb, wa, wb, g, b):
    n, d = x.shape
    ka = ya.shape[1]
    kb = yb.shape[1]
    tm = _row_tile(n)
    row = lambda i: (i, 0)
    const = lambda i: (0, 0)
    return pl.pallas_call(
        _ab_out_kernel,
        grid=(n // tm,),
        in_specs=[pl.BlockSpec((tm, d), row), pl.BlockSpec((tm, ka), row), pl.BlockSpec((tm, kb), row),
                  pl.BlockSpec((ka, d), const), pl.BlockSpec((kb, d), const),
                  pl.BlockSpec((1, d), const), pl.BlockSpec((1, d), const)],
        out_specs=pl.BlockSpec((tm, d), row),
        out_shape=jax.ShapeDtypeStruct((n, d), F32),
        compiler_params=_params(("parallel",)),
        name="ab_out_postnorm",
    )(x, ya, yb, wa, wb, g, b)


def _xattn_kernel(x_ref, k_ref, v_ref, wq_ref, wo_ref, g_ref, b_ref, o_ref):
    x = x_ref[0]
    q = jnp.dot(x.astype(BF16), wq_ref[...], preferred_element_type=F32)
    qb = q.astype(BF16)
    kb = k_ref[0].astype(BF16)
    vb = v_ref[0].astype(BF16)
    outs = []
    for h in range(MEM_HEADS):
        sl = slice(h * MEM_HEAD_DIM, (h + 1) * MEM_HEAD_DIM)
        s = lax.dot_general(qb[:, sl], kb[:, sl], _NT, preferred_element_type=F32) * (MEM_HEAD_DIM ** -0.5)
        m = jnp.max(s, axis=-1, keepdims=True)
        p = jnp.exp(s - m)
        p = p / jnp.sum(p, axis=-1, keepdims=True)
        outs.append(jnp.dot(p.astype(BF16), vb[:, sl], preferred_element_type=F32))
    o = jnp.concatenate(outs, axis=-1)
    y = jnp.dot(o.astype(BF16), wo_ref[...], preferred_element_type=F32)
    o_ref[0] = _layer_norm(ALPHA * x + y, g_ref[...], b_ref[...], LN_EPS)


def _xattn_postnorm(x, mem_k, mem_v, wq, wo, g, b):
    bsz, t, d = x.shape
    tm = _row_tile(t)
    const = lambda bi, ti: (0, 0)
    return pl.pallas_call(
        _xattn_kernel,
        grid=(bsz, t // tm),
        in_specs=[pl.BlockSpec((1, tm, d), lambda bi, ti: (bi, ti, 0)),
                  pl.BlockSpec((1, N_MEM, d), lambda bi, ti: (bi, 0, 0)),
                  pl.BlockSpec((1, N_MEM, d), lambda bi, ti: (bi, 0, 0)),
                  pl.BlockSpec((d, d), const), pl.BlockSpec((d, d), const),
                  pl.BlockSpec((1, d), const), pl.BlockSpec((1, d), const)],
        out_specs=pl.BlockSpec((1, tm, d), lambda bi, ti: (bi, ti, 0)),
        out_shape=jax.ShapeDtypeStruct((bsz, t, d), F32),
        compiler_params=_params(("parallel", "parallel")),
        name="xattn_postnorm",
    )(x, mem_k, mem_v, wq, wo, g, b)


def _lru_kernel(xa_ref, ga_ref, cbuf_ref, h0_ref, cw_ref, cb_ref, wa_ref, ba_ref, wx_ref, bx_ref, lam_ref,
                ya_ref, nbuf_ref, hlast_ref, prev_s, h_s, a_s, u_s, *, tc):
    t = pl.program_id(1)

    @pl.when(t == 0)
    def _():
        prev_s[...] = jnp.zeros(prev_s.shape, F32)
        prev_s[V7X_SUBLANES - (CONV_WIDTH - 1):, :] = cbuf_ref[0]
        h_s[...] = h0_ref[0]

    xa = xa_ref[0]
    ext = jnp.concatenate([prev_s[...], xa], axis=0)
    xc = cb_ref[...] + xa * cw_ref[CONV_WIDTH - 1:CONV_WIDTH, :]
    for s in range(1, CONV_WIDTH):
        shifted = pltpu.roll(ext, s, 0)[V7X_SUBLANES:, :]
        xc = xc + shifted * cw_ref[CONV_WIDTH - 1 - s:CONV_WIDTH - s, :]
    prev_s[...] = xa[tc - V7X_SUBLANES:, :]
    nbuf_ref[0] = xa[tc - (CONV_WIDTH - 1):, :]

    xcb = xc.astype(BF16)
    r = _sigmoid(jnp.dot(xcb, wa_ref[...], preferred_element_type=F32) + ba_ref[...])
    i = _sigmoid(jnp.dot(xcb, wx_ref[...], preferred_element_type=F32) + bx_ref[...])
    log_a = (-LRU_C * _softplus(-lam_ref[...])) * r
    a = jnp.exp(log_a)
    th = jnp.tanh(log_a)
    u = jnp.sqrt(-2.0 * th / (1.0 - th)) * (i * xc)

    row8 = lax.broadcasted_iota(jnp.int32, a.shape, 0) % V7X_SUBLANES
    s = 1
    while s < V7X_SUBLANES:
        inside = row8 >= s
        u = jnp.where(inside, u + a * pltpu.roll(u, s, 0), u)
        a = jnp.where(inside, a * pltpu.roll(a, s, 0), a)
        s *= 2
    a_s[...] = a
    u_s[...] = u

    def body(j, h):
        base = pl.multiple_of(j * V7X_SUBLANES, V7X_SUBLANES)
        hb = u_s[pl.ds(base, V7X_SUBLANES), :] + a_s[pl.ds(base, V7X_SUBLANES), :] * h
        u_s[pl.ds(base, V7X_SUBLANES), :] = hb
        return hb[V7X_SUBLANES - 1:, :]

    h = lax.fori_loop(0, tc // V7X_SUBLANES, body, h_s[...], unroll=4)
    h_s[...] = h
    hlast_ref[0] = h
    ya_ref[0] = u_s[...] * _gelu_tanh(ga_ref[0])


def _lru_branch(proj, conv_buf, h0, cw, cb, wa, ba, wx, bx, lam):
    bsz, t, _ = proj.shape
    w = LRU_WIDTH
    tc = _row_tile(t)
    const = lambda bi, ti: (0, 0)
    per_b = lambda bi, ti: (bi, 0, 0)
    return pl.pallas_call(
        functools.partial(_lru_kernel, tc=tc),
        grid=(bsz, t // tc),
        in_specs=[pl.BlockSpec((1, tc, w), lambda bi, ti: (bi, ti, 0)),
                  pl.BlockSpec((1, tc, w), lambda bi, ti: (bi, ti, 1)),
                  pl.BlockSpec((1, CONV_WIDTH - 1, w), per_b),
                  pl.BlockSpec((1, 1, w), per_b),
                  pl.BlockSpec((CONV_WIDTH, w), const), pl.BlockSpec((1, w), const),
                  pl.BlockSpec((w, w), const), pl.BlockSpec((1, w), const),
                  pl.BlockSpec((w, w), const), pl.BlockSpec((1, w), const),
                  pl.BlockSpec((1, w), const)],
        out_specs=[pl.BlockSpec((1, tc, w), lambda bi, ti: (bi, ti, 0)),
                   pl.BlockSpec((1, CONV_WIDTH - 1, w), per_b),
                   pl.BlockSpec((1, 1, w), per_b)],
        out_shape=[jax.ShapeDtypeStruct((bsz, t, w), F32),
                   jax.ShapeDtypeStruct((bsz, CONV_WIDTH - 1, w), F32),
                   jax.ShapeDtypeStruct((bsz, 1, w), F32)],
        scratch_shapes=[pltpu.VMEM((V7X_SUBLANES, w), F32), pltpu.VMEM((1, w), F32),
                        pltpu.VMEM((tc, w), F32), pltpu.VMEM((tc, w), F32)],
        compiler_params=_params(("parallel", "arbitrary")),
        name="lru_branch",
    )(proj, proj, conv_buf, h0, cw, cb, wa, ba, wx, bx, lam)


def _ret_kernel(q_ref, k_ref, v_ref, g_ref, cos_ref, sin_ref, s0_ref, gng_ref, gnb_ref,
                yb_ref, slast_ref, s_s, *, c, nch):
    t = pl.program_id(1)

    @pl.when(t == 0)
    def _():
        s_s[...] = s0_ref[0]

    half = RET_HEAD_DIM // 2
    ri = lax.broadcasted_iota(jnp.int32, (c, c), 0)
    ci = lax.broadcasted_iota(jnp.int32, (c, c), 1)
    diff = (ri - ci).astype(F32)
    it = lax.broadcasted_iota(jnp.int32, (c, 1), 0).astype(F32)
    log_g = [math.log1p(-(2.0 ** (-5.0 - h))) for h in range(RET_HEADS)]
    dmask = [jnp.where(diff >= 0, jnp.exp(lg * jnp.maximum(diff, 0.0)), 0.0) for lg in log_g]
    xi = [jnp.exp(lg * (it + 1.0)) for lg in log_g]
    zeta = [jnp.exp(lg * (c - 1.0 - it)) for lg in log_g]

    units = [(ch, h) for ch in range(nch) for h in range(RET_HEADS)]
    qr, inner, kv = {}, {}, {}
    for ch, h in units:
        rows = slice(ch * c, (ch + 1) * c)
        sl = slice(h * RET_HEAD_DIM, (h + 1) * RET_HEAD_DIM)
        cos2 = cos_ref[rows, :]
        sin2 = sin_ref[rows, :]
        qh = q_ref[0, rows, sl]
        kh = k_ref[0, rows, sl]
        vh = v_ref[0, rows, sl].astype(BF16)
        qr[ch, h] = (qh * cos2 + pltpu.roll(qh, half, 1) * sin2).astype(BF16)
        kr = (kh * cos2 + pltpu.roll(kh, half, 1) * sin2) * (RET_HEAD_DIM ** -0.5)
        scores = _dot_nt(qr[ch, h], kr) * dmask[h]
        inner[ch, h] = _dot(scores, vh)
        kv[ch, h] = _dot_tn(kr * zeta[h], vh)
    for h in range(RET_HEADS):
        sl = slice(h * RET_HEAD_DIM, (h + 1) * RET_HEAD_DIM)
        s = s_s[h]
        for ch in range(nch):
            rows = slice(ch * c, (ch + 1) * c)
            o = inner[ch, h] + _dot(qr[ch, h], s) * xi[h]
            s = math.exp(log_g[h] * c) * s + kv[ch, h]
            on = _layer_norm(o, gng_ref[:, sl], gnb_ref[:, sl], LN_EPS)
            gh = g_ref[0, rows, sl]
            yb_ref[0, rows, sl] = on * (gh * _sigmoid(gh))
        s_s[h] = s
    slast_ref[0] = s_s[...]


def _ret_branch(proj, cos2, sin2, s0, gn_g, gn_b, c):
    bsz, t, _ = proj.shape
    w = RET_WIDTH
    dk = RET_HEAD_DIM
    nch = RET_CHUNKS_PER_STEP if t % (RET_CHUNKS_PER_STEP * c) == 0 else 1
    const = lambda bi, ti: (0, 0)
    per_b = lambda bi, ti: (bi, 0, 0, 0)
    col = lambda j: (lambda bi, ti: (bi, ti, j))
    tc = nch * c
    return pl.pallas_call(
        functools.partial(_ret_kernel, c=c, nch=nch),
        grid=(bsz, t // tc),
        in_specs=[pl.BlockSpec((1, tc, w), col(2)), pl.BlockSpec((1, tc, w), col(3)),
                  pl.BlockSpec((1, tc, w), col(4)), pl.BlockSpec((1, tc, w), col(5)),
                  pl.BlockSpec((tc, dk), lambda bi, ti: (ti, 0)), pl.BlockSpec((tc, dk), lambda bi, ti: (ti, 0)),
                  pl.BlockSpec((1, RET_HEADS, dk, dk), per_b),
                  pl.BlockSpec((1, w), const), pl.BlockSpec((1, w), const)],
        out_specs=[pl.BlockSpec((1, tc, w), lambda bi, ti: (bi, ti, 0)),
                   pl.BlockSpec((1, RET_HEADS, dk, dk), per_b)],
        out_shape=[jax.ShapeDtypeStruct((bsz, t, w), F32),
                   jax.ShapeDtypeStruct((bsz, RET_HEADS, dk, dk), F32)],
        scratch_shapes=[pltpu.VMEM((RET_HEADS, dk, dk), F32)],
        compiler_params=_params(("parallel", "arbitrary")),
        name="ret_branch",
    )(proj, proj, proj, proj, cos2, sin2, s0, gn_g, gn_b)


def _rwkv_proj_kernel(x_ref, sh_ref, mu_ref, wrkv_ref, w0_ref, w1_ref, w2_ref, a0_ref, a1_ref, a2_ref,
                      g1_ref, g2_ref, kk_ref, ka_ref, rk_ref, sel_ref, selt_ref,
                      r_o, lw_o, k_o, v_o, kk_o, b_o, g_o, bonus_o, prev_s, *, tm):
    t = pl.program_id(1)

    @pl.when(t == 0)
    def _():
        prev_s[...] = jnp.zeros(prev_s.shape, F32)
        prev_s[V7X_SUBLANES - 1:, :] = sh_ref[0]

    x = x_ref[0]
    ext = jnp.concatenate([prev_s[...], x], axis=0)
    xp = pltpu.roll(ext, 1, 0)[V7X_SUBLANES:, :]
    prev_s[...] = x[tm - V7X_SUBLANES:, :]
    dx = xp - x

    def mix(p):
        return (x + dx * mu_ref[p:p + 1, :]).astype(BF16)

    r = jnp.dot(mix(0), wrkv_ref[0], preferred_element_type=F32)
    k = jnp.dot(mix(1), wrkv_ref[1], preferred_element_type=F32)
    v = jnp.dot(mix(2), wrkv_ref[2], preferred_element_type=F32)
    wl = _dot(jnp.tanh(jnp.dot(mix(3), w1_ref[...], preferred_element_type=F32)), w2_ref[...])
    w = -_softplus(-(w0_ref[...] + wl)) - 0.5
    al = _dot(jnp.dot(mix(4), a1_ref[...], preferred_element_type=F32), a2_ref[...])
    iclr = _sigmoid(a0_ref[...] + al)
    gate = _dot(_sigmoid(jnp.dot(mix(5), g1_ref[...], preferred_element_type=F32)), g2_ref[...])

    kk = k * kk_ref[...]
    norm = jnp.sqrt(_seg_sum(kk * kk, sel_ref, selt_ref))
    kk = kk / jnp.maximum(norm, 1e-12)
    k2 = k * (1.0 + (iclr - 1.0) * ka_ref[...])

    r_o[0] = r
    lw_o[0] = -jnp.exp(w)
    k_o[0] = k2
    v_o[0] = v
    kk_o[0] = kk
    b_o[0] = kk * iclr
    g_o[0] = gate
    bonus_o[0] = _seg_sum(r * k2 * rk_ref[...], sel_ref, selt_ref) * v


def _rwkv_proj(x, shift, mu, wrkv, w0, w1, w2, a0, a1, a2, g1, g2, k_k, k_a, r_k, sel, selt):
    bsz, t, d = x.shape
    tm = _row_tile(t, 256)
    const2 = lambda bi, ti: (0, 0)
    const3 = lambda bi, ti: (0, 0, 0)
    full2 = lambda a: pl.BlockSpec(a.shape, const2)
    tile = pl.BlockSpec((1, tm, d), lambda bi, ti: (bi, ti, 0))
    out = jax.ShapeDtypeStruct((bsz, t, d), F32)
    return pl.pallas_call(
        functools.partial(_rwkv_proj_kernel, tm=tm),
        grid=(bsz, t // tm),
        in_specs=[tile, pl.BlockSpec((1, 1, d), lambda bi, ti: (bi, 0, 0)), full2(mu),
                  pl.BlockSpec(wrkv.shape, const3), full2(w0), full2(w1), full2(w2), full2(a0), full2(a1),
                  full2(a2), full2(g1), full2(g2), full2(k_k), full2(k_a), full2(r_k), full2(sel), full2(selt)],
        out_specs=[tile] * 8,
        out_shape=[out] * 8,
        scratch_shapes=[pltpu.VMEM((V7X_SUBLANES, d), F32)],
        compiler_params=_params(("parallel", "arbitrary")),
        name="rwkv_proj",
    )(x, shift, mu, wrkv, w0, w1, w2, a0, a1, a2, g1, g2, k_k, k_a, r_k, sel, selt)


def _wkv_kernel(r_ref, lw_ref, k_ref, v_ref, kk_ref, b_ref, s0_ref, o_ref, slast_ref, s_s, *, c, nch):
    t = pl.program_id(1)
    n = RWKV_HEAD
    g = WKV_GROUP
    gl = g * n
    gc = g * c
    ngroups = RWKV_HEADS // g

    @pl.when(t == 0)
    def _():
        for h in range(RWKV_HEADS):
            s_s[:, h * n:(h + 1) * n] = s0_ref[0, h]

    iota = lambda shape, dim: lax.broadcasted_iota(jnp.int32, shape, dim)
    head_k = iota((1, gl), 1) // n
    head_j = iota((1, gc), 1) // c
    row_t = iota((c, gc), 0)
    col_j = iota((c, gc), 1) % c
    strict = row_t > col_j
    lower = row_t >= col_j
    eye_c = jnp.where(row_t == col_j, 1.0, 0.0).astype(F32)
    blk = (iota((gl, gl), 0) // n) == (iota((gl, gl), 1) // n)
    eye_l = iota((gl, gl), 0) == iota((gl, gl), 1)
    tri = jnp.where(iota((c, c), 0) >= iota((c, c), 1), 1.0, 0.0).astype(F32)

    def bd(z, lane_head):
        zb = z.astype(BF16)
        return jnp.concatenate([jnp.where(lane_head == h, zb, jnp.zeros_like(zb)) for h in range(g)], axis=0)

    def mm(a, b):
        return jnp.dot(a.astype(BF16), b, preferred_element_type=F32)

    def mm_nt(a, b):
        return lax.dot_general(a.astype(BF16), b, _NT, preferred_element_type=F32)

    units = [(ch, gi) for ch in range(nch) for gi in range(ngroups)]
    u = {}
    for ch in range(nch):
        rows = slice(ch * c, (ch + 1) * c)
        lw = lw_ref[0, rows, :]
        cum = jnp.dot(tri, lw, preferred_element_type=F32, precision=lax.Precision.HIGHEST)
        cum_last = cum[c - 1:c, :]
        e_neg = jnp.exp(-cum)
        e_end = jnp.exp(cum_last - cum)
        kk = kk_ref[0, rows, :]
        b = b_ref[0, rows, :]
        k = k_ref[0, rows, :]
        full = dict(a_t=kk * jnp.exp(cum - lw), b_t=b * e_neg, k_t=k * e_neg, r_t=r_ref[0, rows, :] * jnp.exp(cum),
                    b_e=b * e_end, k_e=k * e_end, d_end=jnp.exp(cum_last), v=v_ref[0, rows, :])
        for gi in range(ngroups):
            lanes = slice(gi * gl, (gi + 1) * gl)
            u[ch, gi] = {name: val[:, lanes] for name, val in full.items()}

    for key in units:
        d = u[key]
        x2 = jnp.concatenate([d['a_t'], d['r_t']], axis=0).astype(BF16)
        mb = mm_nt(x2, bd(d['b_t'], head_k))
        mk = mm_nt(x2, bd(d['k_t'], head_k))
        d['p_rb'] = jnp.where(lower, mb[c:], 0.0).astype(BF16)
        d['m_ak'] = jnp.where(strict, mk[:c], 0.0).astype(BF16)
        d['p_rk'] = jnp.where(lower, mk[c:], 0.0).astype(BF16)
        d['pw'] = -jnp.where(strict, mb[:c], 0.0)
        d['tinv'] = eye_c + d['pw']
        d['v_bd'] = bd(d['v'], head_k)
    for _ in range(int(math.log2(c)) - 1):
        for key in units:
            d = u[key]
            pwb = d['pw'].astype(BF16)
            d['pw'] = mm(pwb, bd(pwb, head_j))
        for key in units:
            d = u[key]
            d['tinv'] = d['tinv'] + mm(d['tinv'], bd(d['pw'], head_j))
    for key in units:
        d = u[key]
        d['w_h'] = mm(d['m_ak'], d['v_bd'])
    for key in units:
        d = u[key]
        tb = d['tinv'].astype(BF16)
        d['a_hat'] = mm(tb, bd(d['a_t'], head_k))
        d['w_hat'] = mm(tb, bd(d['w_h'], head_k))
    for key in units:
        d = u[key]
        d['r_hat'] = d['r_t'] - mm(d['p_rb'], bd(d['a_hat'], head_k))
        d['o0'] = mm(d['p_rk'], d['v_bd']) - mm(d['p_rb'], bd(d['w_hat'], head_k))
        b_e = d['b_e'].astype(BF16)
        gfull = lax.dot_general(d['a_hat'].astype(BF16), b_e, _TN, preferred_element_type=F32)
        d['gmat'] = (jnp.where(eye_l, d['d_end'], 0.0) - jnp.where(blk, gfull, 0.0)).astype(BF16)
        zf = (lax.dot_general(d['v'].astype(BF16), d['k_e'].astype(BF16), _TN, preferred_element_type=F32)
              - lax.dot_general(d['w_hat'].astype(BF16), b_e, _TN, preferred_element_type=F32))
        hmat = jnp.where(head_k == 0, zf[:n], 0.0)
        for h in range(1, g):
            hmat = hmat + jnp.where(head_k == h, zf[h * n:(h + 1) * n], 0.0)
        d['hmat'] = hmat

    for gi in range(ngroups):
        lanes = slice(gi * gl, (gi + 1) * gl)
        s = s_s[:, lanes]
        for ch in range(nch):
            d = u[ch, gi]
            o_ref[0, ch * c:(ch + 1) * c, lanes] = d['o0'] + mm_nt(d['r_hat'], bd(s, head_k))
            s = mm(s, d['gmat']) + d['hmat']
        s_s[:, lanes] = s

    @pl.when(t == pl.num_programs(1) - 1)
    def _():
        for h in range(RWKV_HEADS):
            slast_ref[0, h] = s_s[:, h * n:(h + 1) * n]


def _wkv_scan(r, lw, k, v, kk, b, s0, c):
    bsz, t, d = r.shape
    n = RWKV_HEAD
    nch = WKV_CHUNKS_PER_STEP if t % (WKV_CHUNKS_PER_STEP * c) == 0 else 1
    tile = pl.BlockSpec((1, nch * c, d), lambda bi, ti: (bi, ti, 0))
    st = pl.BlockSpec((1, RWKV_HEADS, n, n), lambda bi, ti: (bi, 0, 0, 0))
    return pl.pallas_call(
        functools.partial(_wkv_kernel, c=c, nch=nch),
        grid=(bsz, t // (nch * c)),
        in_specs=[tile] * 6 + [st],
        out_specs=[tile, st],
        out_shape=[jax.ShapeDtypeStruct((bsz, t, d), F32), jax.ShapeDtypeStruct(s0.shape, F32)],
        scratch_shapes=[pltpu.VMEM((n, d), F32)],
        compiler_params=_params(("parallel", "arbitrary")),
        name="wkv_scan",
    )(r, lw, k, v, kk, b, s0)


def _wkv_out_kernel(x_ref, o_ref, bonus_ref, gate_ref, gng_ref, gnb_ref, sel_ref, selt_ref, w_ref,
                    g_ref, b_ref, y_ref):
    o = o_ref[...]
    inv_n = 1.0 / RWKV_HEAD
    mu = _seg_sum(o, sel_ref, selt_ref) * inv_n
    oc = o - mu
    var = _seg_sum(oc * oc, sel_ref, selt_ref) * inv_n
    on = oc * lax.rsqrt(var + RWKV_GN_EPS) * gng_ref[...] + gnb_ref[...]
    z = (on + bonus_ref[...]) * gate_ref[...]
    y = jnp.dot(z.astype(BF16), w_ref[...], preferred_element_type=F32)
    y_ref[...] = _layer_norm(ALPHA * x_ref[...] + y, g_ref[...], b_ref[...], LN_EPS)


def _wkv_out_postnorm(x, o, bonus, gate, gn_g, gn_b, sel, selt, w, g, b):
    n, d = x.shape
    tm = _row_tile(n)
    row = pl.BlockSpec((tm, d), lambda i: (i, 0))
    const = lambda i: (0, 0)
    full = lambda a: pl.BlockSpec(a.shape, const)
    return pl.pallas_call(
        _wkv_out_kernel,
        grid=(n // tm,),
        in_specs=[row, row, row, row, full(gn_g), full(gn_b), full(sel), full(selt), full(w), full(g), full(b)],
        out_specs=row,
        out_shape=jax.ShapeDtypeStruct((n, d), F32),
        compiler_params=_params(("parallel",)),
        name="wkv_out_postnorm",
    )(x, o, bonus, gate, gn_g, gn_b, sel, selt, w, g, b)


def _prep_weights(ln_g, ln_b, ffn_up, ffn_down, xa_q, xa_o, l0_w_in, l0_conv_w, l0_conv_b, l0_lru_wa,
                  l0_lru_ba, l0_lru_wx, l0_lru_bx, l0_lru_lambda, l0_ret_gn_g, l0_ret_gn_b, l0_w_out,
                  l1_mu, l1_w_rkv, l1_w0, l1_w1, l1_w2, l1_a0, l1_a1, l1_a2, l1_g1, l1_g2, l1_k_k, l1_k_a,
                  l1_r_k, l1_gn_g, l1_gn_b, l1_w_out):
    d = D_MODEL
    nchunks = D_FF // FFN_CHUNK
    row = lambda a: a.reshape(1, -1).astype(F32)

    def chunk_cols(w):
        return w.reshape(d, nchunks, FFN_CHUNK).transpose(1, 0, 2).astype(BF16)

    ffn = [[(chunk_cols(ffn_up[l, j, :, :D_FF]), chunk_cols(ffn_up[l, j, :, D_FF:]),
             ffn_down[l, j].reshape(nchunks, FFN_CHUNK, d).astype(BF16)) for j in range(2)]
           for l in range(DEPTH)]

    def block_diag(w):
        eye = jnp.eye(LRU_BLOCKS, dtype=w.dtype)
        return jnp.einsum('gij,gh->gihj', w, eye).reshape(LRU_WIDTH, LRU_WIDTH).astype(BF16)

    head_of_col = jnp.arange(d) // RWKV_HEAD
    sel = (head_of_col[:, None] == jnp.arange(SEL_WIDTH)[None, :]).astype(BF16)
    return dict(
        ln_g=ln_g, ln_b=ln_b, ffn=ffn,
        xa_q=xa_q.astype(BF16), xa_o=xa_o.astype(BF16),
        w_in=l0_w_in.astype(BF16), conv_w=l0_conv_w, conv_b=row(l0_conv_b),
        lru_wa=block_diag(l0_lru_wa), lru_ba=row(l0_lru_ba), lru_wx=block_diag(l0_lru_wx),
        lru_bx=row(l0_lru_bx), lru_lam=row(l0_lru_lambda),
        ret_gn_g=row(l0_ret_gn_g), ret_gn_b=row(l0_ret_gn_b),
        w_out_a=l0_w_out[:LRU_WIDTH].astype(BF16), w_out_b=l0_w_out[LRU_WIDTH:].astype(BF16),
        mu=l1_mu, w_rkv=l1_w_rkv.astype(BF16), w0=row(l1_w0), w1=l1_w1.astype(BF16), w2=l1_w2.astype(BF16),
        a0=row(l1_a0), a1=l1_a1.astype(BF16), a2=l1_a2.astype(BF16), g1=l1_g1.astype(BF16),
        g2=l1_g2.astype(BF16), k_k=row(l1_k_k), k_a=row(l1_k_a), r_k=row(l1_r_k),
        gn_g=row(l1_gn_g), gn_b=row(l1_gn_b), w_out_c=l1_w_out.astype(BF16), sel=sel, selt=sel.T,
    )


def _rotary_tables(pos):
    half = RET_HEAD_DIM // 2
    inv_freq = ROPE_BASE ** (-jnp.arange(half, dtype=F32) / half)
    ang = pos.astype(F32)[:, None] * inv_freq[None, :]
    cos = jnp.cos(ang)
    sin = jnp.sin(ang)
    return jnp.concatenate([cos, cos], axis=-1), jnp.concatenate([-sin, sin], axis=-1)


def _run_trunk(x, pos, mem_k, mem_v, states, p):
    bsz, t, d = x.shape
    n = bsz * t
    (conv_buf, h0, s_ret), (shift, s_wkv) = states
    lng = lambda l, j: p['ln_g'][l, j].reshape(1, d)
    lnb = lambda l, j: p['ln_b'][l, j].reshape(1, d)
    flat = lambda a: a.reshape(n, a.shape[-1])
    chunk = min(CHUNK, t)

    x = _ffn_postnorm(flat(x), *p['ffn'][0][0], lng(0, 0), lnb(0, 0))
    proj = _matmul(x, p['w_in']).reshape(bsz, t, -1)
    ya, new_buf, h_last = _lru_branch(proj, conv_buf, h0.reshape(bsz, 1, LRU_WIDTH), p['conv_w'], p['conv_b'],
                                      p['lru_wa'], p['lru_ba'], p['lru_wx'], p['lru_bx'], p['lru_lam'])
    cos2, sin2 = _rotary_tables(pos)
    yb, s_ret_new = _ret_branch(proj, cos2, sin2, s_ret, p['ret_gn_g'], p['ret_gn_b'], chunk)
    x = _ab_out_postnorm(x, flat(ya), flat(yb), p['w_out_a'], p['w_out_b'], lng(0, 1), lnb(0, 1))
    x = _xattn_postnorm(x.reshape(bsz, t, d), mem_k[0], mem_v[0], p['xa_q'][0], p['xa_o'][0],
                        lng(0, 2), lnb(0, 2))
    x = _ffn_postnorm(flat(x), *p['ffn'][0][1], lng(0, 3), lnb(0, 3))

    x = _ffn_postnorm(x, *p['ffn'][1][0], lng(1, 0), lnb(1, 0))
    x3 = x.reshape(bsz, t, d)
    r, lw, k2, v, kk, b, gate, bonus = _rwkv_proj(
        x3, shift, p['mu'], p['w_rkv'], p['w0'], p['w1'], p['w2'], p['a0'], p['a1'], p['a2'], p['g1'], p['g2'],
        p['k_k'], p['k_a'], p['r_k'], p['sel'], p['selt'])
    o, s_wkv_new = _wkv_scan(r, lw, k2, v, kk, b, s_wkv, chunk)
    new_shift = x3[:, t - 1:, :]
    x = _wkv_out_postnorm(x, flat(o), flat(bonus), flat(gate), p['gn_g'], p['gn_b'], p['sel'], p['selt'],
                          p['w_out_c'], lng(1, 1), lnb(1, 1))
    x = _xattn_postnorm(x.reshape(bsz, t, d), mem_k[1], mem_v[1], p['xa_q'][1], p['xa_o'][1],
                        lng(1, 2), lnb(1, 2))
    x = _ffn_postnorm(flat(x), *p['ffn'][1][1], lng(1, 3), lnb(1, 3))
    new_states = ((new_buf, h_last.reshape(bsz, LRU_WIDTH), s_ret_new), (new_shift, s_wkv_new))
    return x.reshape(bsz, t, d), new_states


def kernel(x_prompt, x_sample, mem_prompt, state_conv0, state_lru0, state_ret0, state_shift1, state_wkv1,
           cache_mem_k, cache_mem_v, ln_g, ln_b, ffn_up, ffn_down, xa_q, xa_k, xa_v, xa_o,
           l0_w_in, l0_conv_w, l0_conv_b, l0_lru_wa, l0_lru_ba, l0_lru_wx, l0_lru_bx, l0_lru_lambda,
           l0_ret_gn_g, l0_ret_gn_b, l0_w_out, l1_mu, l1_w_rkv, l1_w0, l1_w1, l1_w2, l1_a0, l1_a1, l1_a2,
           l1_g1, l1_g2, l1_k_k, l1_k_a, l1_r_k, l1_gn_g, l1_gn_b, l1_w_out):
    d = D_MODEL
    p = _prep_weights(ln_g, ln_b, ffn_up, ffn_down, xa_q, xa_o, l0_w_in, l0_conv_w, l0_conv_b, l0_lru_wa,
                      l0_lru_ba, l0_lru_wx, l0_lru_bx, l0_lru_lambda, l0_ret_gn_g, l0_ret_gn_b, l0_w_out,
                      l1_mu, l1_w_rkv, l1_w0, l1_w1, l1_w2, l1_a0, l1_a1, l1_a2, l1_g1, l1_g2, l1_k_k, l1_k_a,
                      l1_r_k, l1_gn_g, l1_gn_b, l1_w_out)

    bp, tp, _ = x_prompt.shape
    dt = x_prompt.dtype
    mem_flat = mem_prompt.reshape(bp * N_MEM, d)
    mem_k_p = _matmul_stack(mem_flat, xa_k.astype(BF16)).reshape(DEPTH, bp, N_MEM, d)
    mem_v_p = _matmul_stack(mem_flat, xa_v.astype(BF16)).reshape(DEPTH, bp, N_MEM, d)
    zero_states = ((jnp.zeros((bp, CONV_WIDTH - 1, LRU_WIDTH), dt), jnp.zeros((bp, LRU_WIDTH), dt),
                    jnp.zeros((bp, RET_HEADS, RET_HEAD_DIM, RET_HEAD_DIM), dt)),
                   (jnp.zeros((bp, 1, d), dt), jnp.zeros((bp, RWKV_HEADS, RWKV_HEAD, RWKV_HEAD), dt)))
    y_prompt, st_p = _run_trunk(x_prompt, jnp.arange(tp, dtype=jnp.int32), mem_k_p, mem_v_p, zero_states, p)

    bs, ts, _ = x_sample.shape
    pos_s = PAST_LEN + jnp.arange(ts, dtype=jnp.int32)
    sample_states = ((state_conv0, state_lru0, state_ret0), (state_shift1, state_wkv1))
    y_sample, st_s = _run_trunk(x_sample, pos_s, cache_mem_k.reshape(DEPTH, bs, N_MEM, d),
                                cache_mem_v.reshape(DEPTH, bs, N_MEM, d), sample_states, p)

    (p_conv0, p_lru0, p_ret0), (p_shift1, p_wkv1) = st_p
    (s_conv0, s_lru0, s_ret0), (s_shift1, s_wkv1) = st_s
    mem_shape = (DEPTH, bp, N_MEM, MEM_HEADS, MEM_HEAD_DIM)
    return (y_prompt, y_sample, mem_k_p.reshape(mem_shape), mem_v_p.reshape(mem_shape),
            p_conv0, p_lru0, p_ret0, p_shift1, p_wkv1, s_conv0, s_lru0, s_ret0, s_shift1, s_wkv1)
```

```python
import functools
import math

import jax
import jax.numpy as jnp
from jax import lax
from jax.experimental import pallas as pl
from jax.experimental.pallas import tpu as pltpu

D_MODEL = 1024
DEPTH = 2
PAST_LEN = 4096
CHUNK = 64
N_MEM = 256
MEM_HEADS = 4
MEM_HEAD_DIM = D_MODEL // MEM_HEADS
D_FF = 2816
LRU_WIDTH = 512
LRU_BLOCKS = 8
LRU_BLOCK = LRU_WIDTH // LRU_BLOCKS
CONV_WIDTH = 4
LRU_C = 8.0
RET_HEADS = 4
RET_WIDTH = 512
RET_HEAD_DIM = RET_WIDTH // RET_HEADS
ROPE_BASE = 10000.0
RWKV_HEAD = 64
RWKV_HEADS = D_MODEL // RWKV_HEAD
LN_EPS = 1e-5
RWKV_GN_EPS = 64e-5
ALPHA = (2 * DEPTH) ** 0.25

F32 = jnp.float32
BF16 = jnp.bfloat16

V7X_SUBLANES = 8
V7X_LANES = 128
VMEM_LIMIT_BYTES = 56 * 1024 * 1024
FFN_CHUNK = 256
ROW_TILE = 512
FFN_ROW_TILE = 512
SEL_WIDTH = V7X_LANES
WKV_GROUP = 4
WKV_CHUNKS_PER_STEP = 4
RET_CHUNKS_PER_STEP = 8

_NT = (((1,), (1,)), ((), ()))
_TN = (((0,), (0,)), ((), ()))


def _dot(a, b):
    return jnp.dot(a.astype(BF16), b.astype(BF16), preferred_element_type=F32)


def _dot_nt(a, b):
    return lax.dot_general(a.astype(BF16), b.astype(BF16), _NT, preferred_element_type=F32)


def _dot_tn(a, b):
    return lax.dot_general(a.astype(BF16), b.astype(BF16), _TN, preferred_element_type=F32)


def _dot_split(z, w):
    hi = z.astype(BF16)
    lo = (z - hi.astype(F32)).astype(BF16)
    return (jnp.dot(hi, w, preferred_element_type=F32) + jnp.dot(lo, w, preferred_element_type=F32))


def _seg_sum(z, sel_ref, selt_ref):
    sums = jnp.dot(z.astype(BF16), sel_ref[...], preferred_element_type=F32)
    return _dot_split(sums, selt_ref[...])


def _layer_norm(y, g, b, eps):
    mu = jnp.mean(y, axis=-1, keepdims=True)
    yc = y - mu
    var = jnp.mean(yc * yc, axis=-1, keepdims=True)
    return yc * lax.rsqrt(var + eps) * g + b


def _sigmoid(x):
    return 1.0 / (1.0 + jnp.exp(-x))


def _softplus(x):
    return jnp.maximum(x, 0.0) + jnp.log1p(jnp.exp(-jnp.abs(x)))


def _gelu_tanh(x):
    return 0.5 * x * (1.0 + jnp.tanh(math.sqrt(2.0 / math.pi) * (x + 0.044715 * (x * x * x))))


def _params(sem):
    return pltpu.CompilerParams(dimension_semantics=sem, vmem_limit_bytes=VMEM_LIMIT_BYTES)


def _row_tile(n, pref=ROW_TILE):
    return pref if n % pref == 0 else n


def _ffn_kernel(x_ref, wup_ref, wdn_ref, g_ref, b_ref, o_ref):
    x = x_ref[...]
    xb = x.astype(BF16)
    acc = jnp.zeros(x.shape, F32)
    for lo in range(0, D_FF, FFN_CHUNK):
        hg = jnp.dot(xb, wup_ref[:, lo:lo + FFN_CHUNK], preferred_element_type=F32)
        hu = jnp.dot(xb, wup_ref[:, D_FF + lo:D_FF + lo + FFN_CHUNK], preferred_element_type=F32)
        h = hg * _sigmoid(hg) * hu
        acc = acc + jnp.dot(h.astype(BF16), wdn_ref[lo:lo + FFN_CHUNK, :], preferred_element_type=F32)
    o_ref[...] = _layer_norm(ALPHA * x + 0.5 * acc, g_ref[...], b_ref[...], LN_EPS)


def _ffn_postnorm(x, w_up, w_down, layer, which, g, b):
    n, d = x.shape
    tm = _row_tile(n, FFN_ROW_TILE)
    pick = lambda i: (layer, which, 0, 0)
    return pl.pallas_call(
        _ffn_kernel,
        grid=(n // tm,),
        in_specs=[
            pl.BlockSpec((tm, d), lambda i: (i, 0)),
            pl.BlockSpec((None, None, d, 2 * D_FF), pick),
            pl.BlockSpec((None, None, D_FF, d), pick),
            pl.BlockSpec((1, d), lambda i: (0, 0)),
            pl.BlockSpec((1, d), lambda i: (0, 0)),
        ],
        out_specs=pl.BlockSpec((tm, d), lambda i: (i, 0)),
        out_shape=jax.ShapeDtypeStruct((n, d), F32),
        compiler_params=_params(("parallel",)),
        name="ffn_postnorm",
    )(x, w_up, w_down, g, b)


def _mm_kernel(x_ref, w_ref, o_ref):
    o_ref[...] = jnp.dot(x_ref[...].astype(BF16), w_ref[...], preferred_element_type=F32)


def _matmul(x, w):
    n, k = x.shape
    m = w.shape[1]
    tm = _row_tile(n)
    return pl.pallas_call(
        _mm_kernel,
        grid=(n // tm,),
        in_specs=[pl.BlockSpec((tm, k), lambda i: (i, 0)), pl.BlockSpec((k, m), lambda i: (0, 0))],
        out_specs=pl.BlockSpec((tm, m), lambda i: (i, 0)),
        out_shape=jax.ShapeDtypeStruct((n, m), F32),
        compiler_params=_params(("parallel",)),
        name="proj",
    )(x, w)


def _mem_proj_kernel(x_ref, w_ref, o_ref):
    y = jnp.dot(x_ref[...].astype(BF16), w_ref[...], preferred_element_type=F32)
    for h in range(MEM_HEADS):
        o_ref[:, h, :] = y[:, h * MEM_HEAD_DIM:(h + 1) * MEM_HEAD_DIM]


def _mem_proj(mem, w):
    bsz, m, d = mem.shape
    depth = w.shape[0]
    return pl.pallas_call(
        _mem_proj_kernel,
        grid=(depth, bsz),
        in_specs=[pl.BlockSpec((None, m, d), lambda l, bi: (bi, 0, 0)),
                  pl.BlockSpec((None, d, d), lambda l, bi: (l, 0, 0))],
        out_specs=pl.BlockSpec((None, None, m, MEM_HEADS, MEM_HEAD_DIM), lambda l, bi: (l, bi, 0, 0, 0)),
        out_shape=jax.ShapeDtypeStruct((depth, bsz, m, MEM_HEADS, MEM_HEAD_DIM), F32),
        compiler_params=_params(("parallel", "parallel")),
        name="mem_proj",
    )(mem, w)


def _ab_out_kernel(x_ref, ya_ref, yb_ref, wa_ref, wb_ref, g_ref, b_ref, o_ref):
    y = (jnp.dot(ya_ref[...].astype(BF16), wa_ref[...], preferred_element_type=F32)
         + jnp.dot(yb_ref[...].astype(BF16), wb_ref[...], preferred_element_type=F32))
    o_ref[...] = _layer_norm(ALPHA * x_ref[...] + y, g_ref[...], b_ref[...], LN_EPS)


def _ab_out_postnorm(x, ya, yb, wa, wb, g, b):
    n, d = x.shape
    ka = ya.shape[1]
    kb = yb.shape[1]
    tm = _row_tile(n)
    row = lambda i: (i, 0)
    const = lambda i: (0, 0)
    return pl.pallas_call(
        _ab_out_kernel,
        grid=(n // tm,),
        in_specs=[pl.BlockSpec((tm, d), row), pl.BlockSpec((tm, ka), row), pl.BlockSpec((tm, kb), row),
                  pl.BlockSpec((ka, d), const), pl.BlockSpec((kb, d), const),
                  pl.BlockSpec((1, d), const), pl.BlockSpec((1, d), const)],
        out_specs=pl.BlockSpec((tm, d), row),
        out_shape=jax.ShapeDtypeStruct((n, d), F32),
        compiler_params=_params(("parallel",)),
        name="ab_out_postnorm",
    )(x, ya, yb, wa, wb, g, b)


def _xattn_kernel(x_ref, k_ref, v_ref, wq_ref, wo_ref, g_ref, b_ref, o_ref, kb_s, vb_s):
    @pl.when(pl.program_id(1) == 0)
    def _():
        for h in range(MEM_HEADS):
            kb_s[h] = k_ref[:, h, :].astype(BF16)
            vb_s[h] = v_ref[:, h, :].astype(BF16)

    x = x_ref[0]
    q = jnp.dot(x.astype(BF16), wq_ref[...], preferred_element_type=F32)
    qb = q.astype(BF16)
    outs = []
    for h in range(MEM_HEADS):
        sl = slice(h * MEM_HEAD_DIM, (h + 1) * MEM_HEAD_DIM)
        s = lax.dot_general(qb[:, sl], kb_s[h], _NT, preferred_element_type=F32) * (MEM_HEAD_DIM ** -0.5)
        m = jnp.max(s, axis=-1, keepdims=True)
        p = jnp.exp(s - m)
        p = p / jnp.sum(p, axis=-1, keepdims=True)
        outs.append(jnp.dot(p.astype(BF16), vb_s[h], preferred_element_type=F32))
    o = jnp.concatenate(outs, axis=-1)
    y = jnp.dot(o.astype(BF16), wo_ref[...], preferred_element_type=F32)
    o_ref[0] = _layer_norm(ALPHA * x + y, g_ref[...], b_ref[...], LN_EPS)


def _xattn_postnorm(x, mem_k, mem_v, wq, wo, layer, g, b):
    bsz, t, d = x.shape
    tm = _row_tile(t)
    const = lambda bi, ti: (0, 0)
    mem_spec = pl.BlockSpec((None, None, N_MEM, MEM_HEADS, MEM_HEAD_DIM), lambda bi, ti: (layer, bi, 0, 0, 0))
    w_spec = pl.BlockSpec((None, d, d), lambda bi, ti: (layer, 0, 0))
    return pl.pallas_call(
        _xattn_kernel,
        grid=(bsz, t // tm),
        in_specs=[pl.BlockSpec((1, tm, d), lambda bi, ti: (bi, ti, 0)),
                  mem_spec, mem_spec, w_spec, w_spec,
                  pl.BlockSpec((1, d), const), pl.BlockSpec((1, d), const)],
        out_specs=pl.BlockSpec((1, tm, d), lambda bi, ti: (bi, ti, 0)),
        out_shape=jax.ShapeDtypeStruct((bsz, t, d), F32),
        scratch_shapes=[pltpu.VMEM((MEM_HEADS, N_MEM, MEM_HEAD_DIM), BF16)] * 2,
        compiler_params=_params(("parallel", "arbitrary")),
        name="xattn_postnorm",
    )(x, mem_k, mem_v, wq, wo, g, b)


def _lru_kernel(xa_ref, ga_ref, cbuf_ref, h0_ref, cw_ref, cb_ref, wa_ref, ba_ref, wx_ref, bx_ref, lam_ref,
                ya_ref, nbuf_ref, hlast_ref, prev_s, h_s, a_s, u_s, *, tc):
    t = pl.program_id(1)

    @pl.when(t == 0)
    def _():
        prev_s[...] = jnp.zeros(prev_s.shape, F32)
        prev_s[V7X_SUBLANES - (CONV_WIDTH - 1):, :] = cbuf_ref[0]
        h_s[...] = h0_ref[0]

    xa = xa_ref[0]
    ext = jnp.concatenate([prev_s[...], xa], axis=0)
    xc = cb_ref[...] + xa * cw_ref[CONV_WIDTH - 1:CONV_WIDTH, :]
    for s in range(1, CONV_WIDTH):
        shifted = pltpu.roll(ext, s, 0)[V7X_SUBLANES:, :]
        xc = xc + shifted * cw_ref[CONV_WIDTH - 1 - s:CONV_WIDTH - s, :]
    prev_s[...] = xa[tc - V7X_SUBLANES:, :]
    nbuf_ref[0] = xa[tc - (CONV_WIDTH - 1):, :]

    xcb = xc.astype(BF16)
    r = _sigmoid(jnp.dot(xcb, wa_ref[...], preferred_element_type=F32) + ba_ref[...])
    i = _sigmoid(jnp.dot(xcb, wx_ref[...], preferred_element_type=F32) + bx_ref[...])
    log_a = (-LRU_C * _softplus(-lam_ref[...])) * r
    a = jnp.exp(log_a)
    th = jnp.tanh(log_a)
    u = jnp.sqrt(-2.0 * th / (1.0 - th)) * (i * xc)

    row8 = lax.broadcasted_iota(jnp.int32, a.shape, 0) % V7X_SUBLANES
    s = 1
    while s < V7X_SUBLANES:
        inside = row8 >= s
        u = jnp.where(inside, u + a * pltpu.roll(u, s, 0), u)
        a = jnp.where(inside, a * pltpu.roll(a, s, 0), a)
        s *= 2
    a_s[...] = a
    u_s[...] = u

    def body(j, h):
        base = pl.multiple_of(j * V7X_SUBLANES, V7X_SUBLANES)
        hb = u_s[pl.ds(base, V7X_SUBLANES), :] + a_s[pl.ds(base, V7X_SUBLANES), :] * h
        u_s[pl.ds(base, V7X_SUBLANES), :] = hb
        return hb[V7X_SUBLANES - 1:, :]

    h = lax.fori_loop(0, tc // V7X_SUBLANES, body, h_s[...], unroll=4)
    h_s[...] = h
    hlast_ref[0] = h
    ya_ref[0] = u_s[...] * _gelu_tanh(ga_ref[0])


def _lru_branch(proj, conv_buf, h0, cw, cb, wa, ba, wx, bx, lam):
    bsz, t, _ = proj.shape
    w = LRU_WIDTH
    tc = _row_tile(t)
    const = lambda bi, ti: (0, 0)
    per_b = lambda bi, ti: (bi, 0, 0)
    return pl.pallas_call(
        functools.partial(_lru_kernel, tc=tc),
        grid=(bsz, t // tc),
        in_specs=[pl.BlockSpec((1, tc, w), lambda bi, ti: (bi, ti, 0)),
                  pl.BlockSpec((1, tc, w), lambda bi, ti: (bi, ti, 1)),
                  pl.BlockSpec((1, CONV_WIDTH - 1, w), per_b),
                  pl.BlockSpec((1, 1, w), per_b),
                  pl.BlockSpec((CONV_WIDTH, w), const), pl.BlockSpec((1, w), const),
                  pl.BlockSpec((w, w), const), pl.BlockSpec((1, w), const),
                  pl.BlockSpec((w, w), const), pl.BlockSpec((1, w), const),
                  pl.BlockSpec((1, w), const)],
        out_specs=[pl.BlockSpec((1, tc, w), lambda bi, ti: (bi, ti, 0)),
                   pl.BlockSpec((1, CONV_WIDTH - 1, w), per_b),
                   pl.BlockSpec((1, 1, w), per_b)],
        out_shape=[jax.ShapeDtypeStruct((bsz, t, w), F32),
                   jax.ShapeDtypeStruct((bsz, CONV_WIDTH - 1, w), F32),
                   jax.ShapeDtypeStruct((bsz, 1, w), F32)],
        scratch_shapes=[pltpu.VMEM((V7X_SUBLANES, w), F32), pltpu.VMEM((1, w), F32),
                        pltpu.VMEM((tc, w), F32), pltpu.VMEM((tc, w), F32)],
        compiler_params=_params(("parallel", "arbitrary")),
        name="lru_branch",
    )(proj, proj, conv_buf, h0, cw, cb, wa, ba, wx, bx, lam)


def _ret_kernel(q_ref, k_ref, v_ref, g_ref, cos_ref, sin_ref, s0_ref, gng_ref, gnb_ref,
                yb_ref, slast_ref, s_s, *, c, nch):
    t = pl.program_id(1)

    @pl.when(t == 0)
    def _():
        s_s[...] = s0_ref[0]

    half = RET_HEAD_DIM // 2
    ri = lax.broadcasted_iota(jnp.int32, (c, c), 0)
    ci = lax.broadcasted_iota(jnp.int32, (c, c), 1)
    diff = (ri - ci).astype(F32)
    it = lax.broadcasted_iota(jnp.int32, (c, 1), 0).astype(F32)
    log_g = [math.log1p(-(2.0 ** (-5.0 - h))) for h in range(RET_HEADS)]
    dmask = [jnp.where(diff >= 0, jnp.exp(lg * jnp.maximum(diff, 0.0)), 0.0) for lg in log_g]
    xi = [jnp.exp(lg * (it + 1.0)) for lg in log_g]
    zeta = [jnp.exp(lg * (c - 1.0 - it)) for lg in log_g]

    units = [(ch, h) for ch in range(nch) for h in range(RET_HEADS)]
    qr, inner, kv = {}, {}, {}
    for ch, h in units:
        rows = slice(ch * c, (ch + 1) * c)
        sl = slice(h * RET_HEAD_DIM, (h + 1) * RET_HEAD_DIM)
        cos2 = cos_ref[rows, :]
        sin2 = sin_ref[rows, :]
        qh = q_ref[0, rows, sl]
        kh = k_ref[0, rows, sl]
        vh = v_ref[0, rows, sl].astype(BF16)
        qr[ch, h] = (qh * cos2 + pltpu.roll(qh, half, 1) * sin2).astype(BF16)
        kr = (kh * cos2 + pltpu.roll(kh, half, 1) * sin2) * (RET_HEAD_DIM ** -0.5)
        scores = _dot_nt(qr[ch, h], kr) * dmask[h]
        inner[ch, h] = _dot(scores, vh)
        kv[ch, h] = _dot_tn(kr * zeta[h], vh)
    for h in range(RET_HEADS):
        sl = slice(h * RET_HEAD_DIM, (h + 1) * RET_HEAD_DIM)
        s = s_s[h]
        for ch in range(nch):
            rows = slice(ch * c, (ch + 1) * c)
            o = inner[ch, h] + _dot(qr[ch, h], s) * xi[h]
            s = math.exp(log_g[h] * c) * s + kv[ch, h]
            on = _layer_norm(o, gng_ref[:, sl], gnb_ref[:, sl], LN_EPS)
            gh = g_ref[0, rows, sl]
            yb_ref[0, rows, sl] = on * (gh * _sigmoid(gh))
        s_s[h] = s
    slast_ref[0] = s_s[...]


def _ret_branch(proj, cos2, sin2, s0, gn_g, gn_b, c):
    bsz, t, _ = proj.shape
    w = RET_WIDTH
    dk = RET_HEAD_DIM
    nch = RET_CHUNKS_PER_STEP if t % (RET_CHUNKS_PER_STEP * c) == 0 else 1
    const = lambda bi, ti: (0, 0)
    per_b = lambda bi, ti: (bi, 0, 0, 0)
    col = lambda j: (lambda bi, ti: (bi, ti, j))
    tc = nch * c
    return pl.pallas_call(
        functools.partial(_ret_kernel, c=c, nch=nch),
        grid=(bsz, t // tc),
        in_specs=[pl.BlockSpec((1, tc, w), col(2)), pl.BlockSpec((1, tc, w), col(3)),
                  pl.BlockSpec((1, tc, w), col(4)), pl.BlockSpec((1, tc, w), col(5)),
                  pl.BlockSpec((tc, dk), lambda bi, ti: (ti, 0)), pl.BlockSpec((tc, dk), lambda bi, ti: (ti, 0)),
                  pl.BlockSpec((1, RET_HEADS, dk, dk), per_b),
                  pl.BlockSpec((1, w), const), pl.BlockSpec((1, w), const)],
        out_specs=[pl.BlockSpec((1, tc, w), lambda bi, ti: (bi, ti, 0)),
                   pl.BlockSpec((1, RET_HEADS, dk, dk), per_b)],
        out_shape=[jax.ShapeDtypeStruct((bsz, t, w), F32),
                   jax.ShapeDtypeStruct((bsz, RET_HEADS, dk, dk), F32)],
        scratch_shapes=[pltpu.VMEM((RET_HEADS, dk, dk), F32)],
        compiler_params=_params(("parallel", "arbitrary")),
        name="ret_branch",
    )(proj, proj, proj, proj, cos2, sin2, s0, gn_g, gn_b)


def _rwkv_proj_kernel(x_ref, sh_ref, mu_ref, wrkv_ref, w0_ref, w1_ref, w2_ref, a0_ref, a1_ref, a2_ref,
                      g1_ref, g2_ref, kk_ref, ka_ref, rk_ref, sel_ref, selt_ref,
                      r_o, lw_o, k_o, v_o, kk_o, b_o, g_o, bonus_o, prev_s, *, tm):
    t = pl.program_id(1)

    @pl.when(t == 0)
    def _():
        prev_s[...] = jnp.zeros(prev_s.shape, F32)
        prev_s[V7X_SUBLANES - 1:, :] = sh_ref[0]

    x = x_ref[0]
    ext = jnp.concatenate([prev_s[...], x], axis=0)
    xp = pltpu.roll(ext, 1, 0)[V7X_SUBLANES:, :]
    prev_s[...] = x[tm - V7X_SUBLANES:, :]
    dx = xp - x

    def mix(p):
        return (x + dx * mu_ref[p:p + 1, :]).astype(BF16)

    r = jnp.dot(mix(0), wrkv_ref[0], preferred_element_type=F32)
    k = jnp.dot(mix(1), wrkv_ref[1], preferred_element_type=F32)
    v = jnp.dot(mix(2), wrkv_ref[2], preferred_element_type=F32)
    wl = _dot(jnp.tanh(jnp.dot(mix(3), w1_ref[...], preferred_element_type=F32)), w2_ref[...])
    w = -_softplus(-(w0_ref[...] + wl)) - 0.5
    al = _dot(jnp.dot(mix(4), a1_ref[...], preferred_element_type=F32), a2_ref[...])
    iclr = _sigmoid(a0_ref[...] + al)
    gate = _dot(_sigmoid(jnp.dot(mix(5), g1_ref[...], preferred_element_type=F32)), g2_ref[...])

    kk = k * kk_ref[...]
    norm = jnp.sqrt(_seg_sum(kk * kk, sel_ref, selt_ref))
    kk = kk / jnp.maximum(norm, 1e-12)
    k2 = k * (1.0 + (iclr - 1.0) * ka_ref[...])

    r_o[0] = r
    lw_o[0] = -jnp.exp(w)
    k_o[0] = k2
    v_o[0] = v.astype(BF16)
    kk_o[0] = kk
    b_o[0] = kk * iclr
    g_o[0] = gate
    bonus_o[0] = _seg_sum(r * k2 * rk_ref[...], sel_ref, selt_ref) * v


def _rwkv_proj(x, shift, mu, wrkv, w0, w1, w2, a0, a1, a2, g1, g2, k_k, k_a, r_k, sel, selt):
    bsz, t, d = x.shape
    tm = _row_tile(t, 256)
    const2 = lambda bi, ti: (0, 0)
    const3 = lambda bi, ti: (0, 0, 0)
    full2 = lambda a: pl.BlockSpec(a.shape, const2)
    tile = pl.BlockSpec((1, tm, d), lambda bi, ti: (bi, ti, 0))
    out = jax.ShapeDtypeStruct((bsz, t, d), F32)
    return pl.pallas_call(
        functools.partial(_rwkv_proj_kernel, tm=tm),
        grid=(bsz, t // tm),
        in_specs=[tile, pl.BlockSpec((1, 1, d), lambda bi, ti: (bi, 0, 0)), full2(mu),
                  pl.BlockSpec(wrkv.shape, const3), full2(w0), full2(w1), full2(w2), full2(a0), full2(a1),
                  full2(a2), full2(g1), full2(g2), full2(k_k), full2(k_a), full2(r_k), full2(sel), full2(selt)],
        out_specs=[tile] * 8,
        out_shape=[out, out, out, jax.ShapeDtypeStruct((bsz, t, d), BF16), out, out, out, out],
        scratch_shapes=[pltpu.VMEM((V7X_SUBLANES, d), F32)],
        compiler_params=_params(("parallel", "arbitrary")),
        name="rwkv_proj",
    )(x, shift, mu, wrkv, w0, w1, w2, a0, a1, a2, g1, g2, k_k, k_a, r_k, sel, selt)


def _wkv_kernel(r_ref, lw_ref, k_ref, v_ref, kk_ref, b_ref, s0_ref, o_ref, slast_ref, s_s, *, c, nch):
    t = pl.program_id(1)
    n = RWKV_HEAD
    g = WKV_GROUP
    gl = g * n
    gc = g * c
    ngroups = RWKV_HEADS // g

    @pl.when(t == 0)
    def _():
        for h in range(RWKV_HEADS):
            s_s[:, h * n:(h + 1) * n] = s0_ref[0, h]

    iota = lambda shape, dim: lax.broadcasted_iota(jnp.int32, shape, dim)
    head_k = iota((1, gl), 1) // n
    head_j = iota((1, gc), 1) // c
    row_t = iota((c, gc), 0)
    col_j = iota((c, gc), 1) % c
    strict = row_t > col_j
    lower = row_t >= col_j
    eye_c = jnp.where(row_t == col_j, 1.0, 0.0).astype(F32)
    blk = (iota((gl, gl), 0) // n) == (iota((gl, gl), 1) // n)
    eye_l = iota((gl, gl), 0) == iota((gl, gl), 1)
    tri = jnp.where(iota((c, c), 0) >= iota((c, c), 1), 1.0, 0.0).astype(F32)

    def bd(z, lane_head):
        zb = z.astype(BF16)
        return jnp.concatenate([jnp.where(lane_head == h, zb, jnp.zeros_like(zb)) for h in range(g)], axis=0)

    def mm(a, b):
        return jnp.dot(a.astype(BF16), b, preferred_element_type=F32)

    def mm_nt(a, b):
        return lax.dot_general(a.astype(BF16), b, _NT, preferred_element_type=F32)

    units = [(ch, gi) for ch in range(nch) for gi in range(ngroups)]
    u = {}
    for ch in range(nch):
        rows = slice(ch * c, (ch + 1) * c)
        lw = lw_ref[0, rows, :]
        cum = jnp.dot(tri, lw, preferred_element_type=F32, precision=lax.Precision.HIGHEST)
        cum_last = cum[c - 1:c, :]
        e_neg = jnp.exp(-cum)
        e_end = jnp.exp(cum_last - cum)
        kk = kk_ref[0, rows, :]
        b = b_ref[0, rows, :]
        k = k_ref[0, rows, :]
        full = dict(a_t=kk * jnp.exp(cum - lw), b_t=b * e_neg, k_t=k * e_neg, r_t=r_ref[0, rows, :] * jnp.exp(cum),
                    b_e=b * e_end, k_e=k * e_end, d_end=jnp.exp(cum_last), v=v_ref[0, rows, :])
        for gi in range(ngroups):
            lanes = slice(gi * gl, (gi + 1) * gl)
            u[ch, gi] = {name: val[:, lanes] for name, val in full.items()}

    for key in units:
        d = u[key]
        x2 = jnp.concatenate([d['a_t'], d['r_t']], axis=0).astype(BF16)
        mb = mm_nt(x2, bd(d['b_t'], head_k))
        mk = mm_nt(x2, bd(d['k_t'], head_k))
        d['p_rb'] = jnp.where(lower, mb[c:], 0.0).astype(BF16)
        mk_lo = jnp.concatenate([jnp.where(strict, mk[:c], 0.0), jnp.where(lower, mk[c:], 0.0)], axis=0)
        wo = mm(mk_lo, bd(d['v'], head_k))
        d['w_h'] = wo[:c]
        d['o0'] = wo[c:]
        d['pw'] = -jnp.where(strict, mb[:c], 0.0)
        d['tinv'] = eye_c + d['pw']
    nsq = int(math.log2(c)) - 1
    for level in range(nsq):
        for key in units:
            d = u[key]
            pwb = d['pw'].astype(BF16)
            w = bd(pwb, head_j)
            if level == 0:
                d['pw'] = mm(pwb, w)
            else:
                res = mm(jnp.concatenate([pwb, d['tinv'].astype(BF16)], axis=0), w)
                d['pw'] = res[:c]
                d['tinv'] = d['tinv'] + res[c:]
    for key in units:
        d = u[key]
        d['tinv'] = d['tinv'] + mm(d['tinv'], bd(d['pw'], head_j))
    for key in units:
        d = u[key]
        tb = d['tinv'].astype(BF16)
        d['a_hat'] = mm(tb, bd(d['a_t'], head_k))
        d['w_hat'] = mm(tb, bd(d['w_h'], head_k))
    for key in units:
        d = u[key]
        d['r_hat'] = d['r_t'] - mm(d['p_rb'], bd(d['a_hat'], head_k))
        d['o0'] = d['o0'] - mm(d['p_rb'], bd(d['w_hat'], head_k))
        b_e = d['b_e'].astype(BF16)
        gfull = lax.dot_general(d['a_hat'].astype(BF16), b_e, _TN, preferred_element_type=F32)
        d['gmat'] = (jnp.where(eye_l, d['d_end'], 0.0) - jnp.where(blk, gfull, 0.0)).astype(BF16)
        vw = jnp.concatenate([d['v'].astype(BF16), d['w_hat'].astype(BF16)], axis=0)
        kb = jnp.concatenate([d['k_e'].astype(BF16), -b_e], axis=0)
        zf = lax.dot_general(vw, kb, _TN, preferred_element_type=F32)
        hmat = jnp.where(head_k == 0, zf[:n], 0.0)
        for h in range(1, g):
            hmat = hmat + jnp.where(head_k == h, zf[h * n:(h + 1) * n], 0.0)
        d['hmat'] = hmat

    for gi in range(ngroups):
        lanes = slice(gi * gl, (gi + 1) * gl)
        s = s_s[:, lanes]
        for ch in range(nch):
            d = u[ch, gi]
            o_ref[0, ch * c:(ch + 1) * c, lanes] = d['o0'] + mm_nt(d['r_hat'], bd(s, head_k))
            s = mm(s, d['gmat']) + d['hmat']
        s_s[:, lanes] = s

    @pl.when(t == pl.num_programs(1) - 1)
    def _():
        for h in range(RWKV_HEADS):
            slast_ref[0, h] = s_s[:, h * n:(h + 1) * n]


def _wkv_scan(r, lw, k, v, kk, b, s0, c):
    bsz, t, d = r.shape
    n = RWKV_HEAD
    nch = WKV_CHUNKS_PER_STEP if t % (WKV_CHUNKS_PER_STEP * c) == 0 else 1
    tile = pl.BlockSpec((1, nch * c, d), lambda bi, ti: (bi, ti, 0))
    st = pl.BlockSpec((1, RWKV_HEADS, n, n), lambda bi, ti: (bi, 0, 0, 0))
    return pl.pallas_call(
        functools.partial(_wkv_kernel, c=c, nch=nch),
        grid=(bsz, t // (nch * c)),
        in_specs=[tile] * 6 + [st],
        out_specs=[tile, st],
        out_shape=[jax.ShapeDtypeStruct((bsz, t, d), F32), jax.ShapeDtypeStruct(s0.shape, F32)],
        scratch_shapes=[pltpu.VMEM((n, d), F32)],
        compiler_params=_params(("parallel", "arbitrary")),
        name="wkv_scan",
    )(r, lw, k, v, kk, b, s0)


def _wkv_out_kernel(x_ref, o_ref, bonus_ref, gate_ref, gng_ref, gnb_ref, sel_ref, selt_ref, w_ref,
                    g_ref, b_ref, y_ref):
    o = o_ref[...]
    inv_n = 1.0 / RWKV_HEAD
    mu = _seg_sum(o, sel_ref, selt_ref) * inv_n
    oc = o - mu
    var = _seg_sum(oc * oc, sel_ref, selt_ref) * inv_n
    on = oc * lax.rsqrt(var + RWKV_GN_EPS) * gng_ref[...] + gnb_ref[...]
    z = (on + bonus_ref[...]) * gate_ref[...]
    y = jnp.dot(z.astype(BF16), w_ref[...], preferred_element_type=F32)
    y_ref[...] = _layer_norm(ALPHA * x_ref[...] + y, g_ref[...], b_ref[...], LN_EPS)


def _wkv_out_postnorm(x, o, bonus, gate, gn_g, gn_b, sel, selt, w, g, b):
    n, d = x.shape
    tm = _row_tile(n)
    row = pl.BlockSpec((tm, d), lambda i: (i, 0))
    const = lambda i: (0, 0)
    full = lambda a: pl.BlockSpec(a.shape, const)
    return pl.pallas_call(
        _wkv_out_kernel,
        grid=(n // tm,),
        in_specs=[row, row, row, row, full(gn_g), full(gn_b), full(sel), full(selt), full(w), full(g), full(b)],
        out_specs=row,
        out_shape=jax.ShapeDtypeStruct((n, d), F32),
        compiler_params=_params(("parallel",)),
        name="wkv_out_postnorm",
    )(x, o, bonus, gate, gn_g, gn_b, sel, selt, w, g, b)


def _prep_weights(ln_g, ln_b, ffn_up, ffn_down, xa_q, xa_o, l0_w_in, l0_conv_w, l0_conv_b, l0_lru_wa,
                  l0_lru_ba, l0_lru_wx, l0_lru_bx, l0_lru_lambda, l0_ret_gn_g, l0_ret_gn_b, l0_w_out,
                  l1_mu, l1_w_rkv, l1_w0, l1_w1, l1_w2, l1_a0, l1_a1, l1_a2, l1_g1, l1_g2, l1_k_k, l1_k_a,
                  l1_r_k, l1_gn_g, l1_gn_b, l1_w_out):
    d = D_MODEL
    row = lambda a: a.reshape(1, -1).astype(F32)

    def block_diag(w):
        eye = jnp.eye(LRU_BLOCKS, dtype=w.dtype)
        return jnp.einsum('gij,gh->gihj', w, eye).reshape(LRU_WIDTH, LRU_WIDTH).astype(BF16)

    head_of_col = jnp.arange(d) // RWKV_HEAD
    sel = (head_of_col[:, None] == jnp.arange(SEL_WIDTH)[None, :]).astype(BF16)
    return dict(
        ln_g=ln_g, ln_b=ln_b, ffn_up=ffn_up.astype(BF16), ffn_down=ffn_down.astype(BF16),
        xa_q=xa_q.astype(BF16), xa_o=xa_o.astype(BF16),
        w_in=l0_w_in.astype(BF16), conv_w=l0_conv_w, conv_b=row(l0_conv_b),
        lru_wa=block_diag(l0_lru_wa), lru_ba=row(l0_lru_ba), lru_wx=block_diag(l0_lru_wx),
        lru_bx=row(l0_lru_bx), lru_lam=row(l0_lru_lambda),
        ret_gn_g=row(l0_ret_gn_g), ret_gn_b=row(l0_ret_gn_b),
        w_out_a=l0_w_out[:LRU_WIDTH].astype(BF16), w_out_b=l0_w_out[LRU_WIDTH:].astype(BF16),
        mu=l1_mu, w_rkv=l1_w_rkv.astype(BF16), w0=row(l1_w0), w1=l1_w1.astype(BF16), w2=l1_w2.astype(BF16),
        a0=row(l1_a0), a1=l1_a1.astype(BF16), a2=l1_a2.astype(BF16), g1=l1_g1.astype(BF16),
        g2=l1_g2.astype(BF16), k_k=row(l1_k_k), k_a=row(l1_k_a), r_k=row(l1_r_k),
        gn_g=row(l1_gn_g), gn_b=row(l1_gn_b), w_out_c=l1_w_out.astype(BF16), sel=sel, selt=sel.T,
    )


def _rotary_tables(pos):
    half = RET_HEAD_DIM // 2
    inv_freq = ROPE_BASE ** (-jnp.arange(half, dtype=F32) / half)
    ang = pos.astype(F32)[:, None] * inv_freq[None, :]
    cos = jnp.cos(ang)
    sin = jnp.sin(ang)
    return jnp.concatenate([cos, cos], axis=-1), jnp.concatenate([-sin, sin], axis=-1)


def _run_trunk(x, pos, mem_k, mem_v, states, p):
    bsz, t, d = x.shape
    n = bsz * t
    (conv_buf, h0, s_ret), (shift, s_wkv) = states
    lng = lambda l, j: p['ln_g'][l, j].reshape(1, d)
    lnb = lambda l, j: p['ln_b'][l, j].reshape(1, d)
    flat = lambda a: a.reshape(n, a.shape[-1])
    chunk = min(CHUNK, t)
    ffn = lambda xx, l, j, nj: _ffn_postnorm(xx, p['ffn_up'], p['ffn_down'], l, j, lng(l, nj), lnb(l, nj))

    x = ffn(flat(x), 0, 0, 0)
    proj = _matmul(x, p['w_in']).reshape(bsz, t, -1)
    ya, new_buf, h_last = _lru_branch(proj, conv_buf, h0.reshape(bsz, 1, LRU_WIDTH), p['conv_w'], p['conv_b'],
                                      p['lru_wa'], p['lru_ba'], p['lru_wx'], p['lru_bx'], p['lru_lam'])
    cos2, sin2 = _rotary_tables(pos)
    yb, s_ret_new = _ret_branch(proj, cos2, sin2, s_ret, p['ret_gn_g'], p['ret_gn_b'], chunk)
    x = _ab_out_postnorm(x, flat(ya), flat(yb), p['w_out_a'], p['w_out_b'], lng(0, 1), lnb(0, 1))
    x = _xattn_postnorm(x.reshape(bsz, t, d), mem_k, mem_v, p['xa_q'], p['xa_o'], 0, lng(0, 2), lnb(0, 2))
    x = ffn(flat(x), 0, 1, 3)

    x = ffn(x, 1, 0, 0)
    x3 = x.reshape(bsz, t, d)
    r, lw, k2, v, kk, b, gate, bonus = _rwkv_proj(
        x3, shift, p['mu'], p['w_rkv'], p['w0'], p['w1'], p['w2'], p['a0'], p['a1'], p['a2'], p['g1'], p['g2'],
        p['k_k'], p['k_a'], p['r_k'], p['sel'], p['selt'])
    o, s_wkv_new = _wkv_scan(r, lw, k2, v, kk, b, s_wkv, chunk)
    new_shift = x3[:, t - 1:, :]
    x = _wkv_out_postnorm(x, flat(o), flat(bonus), flat(gate), p['gn_g'], p['gn_b'], p['sel'], p['selt'],
                          p['w_out_c'], lng(1, 1), lnb(1, 1))
    x = _xattn_postnorm(x.reshape(bsz, t, d), mem_k, mem_v, p['xa_q'], p['xa_o'], 1, lng(1, 2), lnb(1, 2))
    x = ffn(flat(x), 1, 1, 3)
    new_states = ((new_buf, h_last.reshape(bsz, LRU_WIDTH), s_ret_new), (new_shift, s_wkv_new))
    return x.reshape(bsz, t, d), new_states


def kernel(x_prompt, x_sample, mem_prompt, state_conv0, state_lru0, state_ret0, state_shift1, state_wkv1,
           cache_mem_k, cache_mem_v, ln_g, ln_b, ffn_up, ffn_down, xa_q, xa_k, xa_v, xa_o,
           l0_w_in, l0_conv_w, l0_conv_b, l0_lru_wa, l0_lru_ba, l0_lru_wx, l0_lru_bx, l0_lru_lambda,
           l0_ret_gn_g, l0_ret_gn_b, l0_w_out, l1_mu, l1_w_rkv, l1_w0, l1_w1, l1_w2, l1_a0, l1_a1, l1_a2,
           l1_g1, l1_g2, l1_k_k, l1_k_a, l1_r_k, l1_gn_g, l1_gn_b, l1_w_out):
    d = D_MODEL
    p = _prep_weights(ln_g, ln_b, ffn_up, ffn_down, xa_q, xa_o, l0_w_in, l0_conv_w, l0_conv_b, l0_lru_wa,
                      l0_lru_ba, l0_lru_wx, l0_lru_bx, l0_lru_lambda, l0_ret_gn_g, l0_ret_gn_b, l0_w_out,
                      l1_mu, l1_w_rkv, l1_w0, l1_w1, l1_w2, l1_a0, l1_a1, l1_a2, l1_g1, l1_g2, l1_k_k, l1_k_a,
                      l1_r_k, l1_gn_g, l1_gn_b, l1_w_out)

    bp, tp, _ = x_prompt.shape
    dt = x_prompt.dtype
    mem_k_p = _mem_proj(mem_prompt, xa_k.astype(BF16))
    mem_v_p = _mem_proj(mem_prompt, xa_v.astype(BF16))
    zero_states = ((jnp.zeros((bp, CONV_WIDTH - 1, LRU_WIDTH), dt), jnp.zeros((bp, LRU_WIDTH), dt),
                    jnp.zeros((bp, RET_HEADS, RET_HEAD_DIM, RET_HEAD_DIM), dt)),
                   (jnp.zeros((bp, 1, d), dt), jnp.zeros((bp, RWKV_HEADS, RWKV_HEAD, RWKV_HEAD), dt)))
    y_prompt, st_p = _run_trunk(x_prompt, jnp.arange(tp, dtype=jnp.int32), mem_k_p, mem_v_p, zero_states, p)

    bs, ts, _ = x_sample.shape
    pos_s = PAST_LEN + jnp.arange(ts, dtype=jnp.int32)
    sample_states = ((state_conv0, state_lru0, state_ret0), (state_shift1, state_wkv1))
    y_sample, st_s = _run_trunk(x_sample, pos_s, cache_mem_k, cache_mem_v, sample_states, p)

    (p_conv0, p_lru0, p_ret0), (p_shift1, p_wkv1) = st_p
    (s_conv0, s_lru0, s_ret0), (s_shift1, s_wkv1) = st_s
    return (y_prompt, y_sample, mem_k_p, mem_v_p,
            p_conv0, p_lru0, p_ret0, p_shift1, p_wkv1, s_conv0, s_lru0, s_ret0, s_shift1, s_wkv1)
```

```python
import functools
import math

import jax
import jax.numpy as jnp
from jax import lax
from jax.experimental import pallas as pl
from jax.experimental.pallas import tpu as pltpu

D_MODEL = 1024
DEPTH = 2
PAST_LEN = 4096
CHUNK = 64
N_MEM = 256
MEM_HEADS = 4
MEM_HEAD_DIM = D_MODEL // MEM_HEADS
D_FF = 2816
LRU_WIDTH = 512
LRU_BLOCKS = 8
LRU_BLOCK = LRU_WIDTH // LRU_BLOCKS
CONV_WIDTH = 4
LRU_C = 8.0
RET_HEADS = 4
RET_WIDTH = 512
RET_HEAD_DIM = RET_WIDTH // RET_HEADS
ROPE_BASE = 10000.0
RWKV_HEAD = 64
RWKV_HEADS = D_MODEL // RWKV_HEAD
LN_EPS = 1e-5
RWKV_GN_EPS = 64e-5
ALPHA = (2 * DEPTH) ** 0.25

F32 = jnp.float32
BF16 = jnp.bfloat16

V7X_SUBLANES = 8
V7X_LANES = 128
VMEM_LIMIT_BYTES = 56 * 1024 * 1024
FFN_CHUNK = 256
ROW_TILE = 1024
FFN_ROW_TILE = 1024
RWKV_PROJ_ROW_TILE = 512
SEL_WIDTH = V7X_LANES
WKV_GROUP = 4
WKV_CHUNKS_PER_STEP = 4
RET_CHUNKS_PER_STEP = 8

_NT = (((1,), (1,)), ((), ()))
_TN = (((0,), (0,)), ((), ()))


def _dot(a, b):
    return jnp.dot(a.astype(BF16), b.astype(BF16), preferred_element_type=F32)


def _dot_nt(a, b):
    return lax.dot_general(a.astype(BF16), b.astype(BF16), _NT, preferred_element_type=F32)


def _dot_tn(a, b):
    return lax.dot_general(a.astype(BF16), b.astype(BF16), _TN, preferred_element_type=F32)


def _dot_split(z, w):
    hi = z.astype(BF16)
    lo = (z - hi.astype(F32)).astype(BF16)
    return (jnp.dot(hi, w, preferred_element_type=F32) + jnp.dot(lo, w, preferred_element_type=F32))


def _seg_sum(z, sel_ref, selt_ref):
    sums = jnp.dot(z.astype(BF16), sel_ref[...], preferred_element_type=F32)
    return _dot_split(sums, selt_ref[...])


def _layer_norm(y, g, b, eps):
    mu = jnp.mean(y, axis=-1, keepdims=True)
    yc = y - mu
    var = jnp.mean(yc * yc, axis=-1, keepdims=True)
    return yc * lax.rsqrt(var + eps) * g + b


def _sigmoid(x):
    return 1.0 / (1.0 + jnp.exp(-x))


def _softplus(x):
    return jnp.maximum(x, 0.0) + jnp.log1p(jnp.exp(-jnp.abs(x)))


def _gelu_tanh(x):
    return 0.5 * x * (1.0 + jnp.tanh(math.sqrt(2.0 / math.pi) * (x + 0.044715 * (x * x * x))))


def _params(sem):
    return pltpu.CompilerParams(dimension_semantics=sem, vmem_limit_bytes=VMEM_LIMIT_BYTES)


def _row_tile(n, pref=ROW_TILE):
    return pref if n % pref == 0 else n


def _ffn_kernel(x_ref, wup_ref, wdn_ref, g_ref, b_ref, o_ref):
    x = x_ref[...]
    xb = x.astype(BF16)
    acc = jnp.zeros(x.shape, F32)
    for lo in range(0, D_FF, FFN_CHUNK):
        hg = jnp.dot(xb, wup_ref[:, lo:lo + FFN_CHUNK], preferred_element_type=F32)
        hu = jnp.dot(xb, wup_ref[:, D_FF + lo:D_FF + lo + FFN_CHUNK], preferred_element_type=F32)
        h = hg * _sigmoid(hg) * hu
        acc = acc + jnp.dot(h.astype(BF16), wdn_ref[lo:lo + FFN_CHUNK, :], preferred_element_type=F32)
    o_ref[...] = _layer_norm(ALPHA * x + 0.5 * acc, g_ref[...], b_ref[...], LN_EPS)


def _ffn_postnorm(x, w_up, w_down, layer, which, g, b):
    n, d = x.shape
    tm = _row_tile(n, FFN_ROW_TILE)
    pick = lambda i: (layer, which, 0, 0)
    return pl.pallas_call(
        _ffn_kernel,
        grid=(n // tm,),
        in_specs=[
            pl.BlockSpec((tm, d), lambda i: (i, 0)),
            pl.BlockSpec((None, None, d, 2 * D_FF), pick, pipeline_mode=pl.Buffered(1)),
            pl.BlockSpec((None, None, D_FF, d), pick, pipeline_mode=pl.Buffered(1)),
            pl.BlockSpec((1, d), lambda i: (0, 0)),
            pl.BlockSpec((1, d), lambda i: (0, 0)),
        ],
        out_specs=pl.BlockSpec((tm, d), lambda i: (i, 0)),
        out_shape=jax.ShapeDtypeStruct((n, d), F32),
        compiler_params=_params(("parallel",)),
        name="ffn_postnorm",
    )(x, w_up, w_down, g, b)


def _mm_kernel(x_ref, w_ref, o_ref):
    o_ref[...] = jnp.dot(x_ref[...].astype(BF16), w_ref[...], preferred_element_type=F32)


def _matmul(x, w):
    n, k = x.shape
    m = w.shape[1]
    tm = _row_tile(n)
    return pl.pallas_call(
        _mm_kernel,
        grid=(n // tm,),
        in_specs=[pl.BlockSpec((tm, k), lambda i: (i, 0)), pl.BlockSpec((k, m), lambda i: (0, 0))],
        out_specs=pl.BlockSpec((tm, m), lambda i: (i, 0)),
        out_shape=jax.ShapeDtypeStruct((n, m), F32),
        compiler_params=_params(("parallel",)),
        name="proj",
    )(x, w)


def _mem_proj_kernel(x_ref, w_ref, o_ref):
    y = jnp.dot(x_ref[...].astype(BF16), w_ref[...], preferred_element_type=F32)
    for h in range(MEM_HEADS):
        o_ref[:, h, :] = y[:, h * MEM_HEAD_DIM:(h + 1) * MEM_HEAD_DIM]


def _mem_proj(mem, w):
    bsz, m, d = mem.shape
    depth = w.shape[0]
    return pl.pallas_call(
        _mem_proj_kernel,
        grid=(depth, bsz),
        in_specs=[pl.BlockSpec((None, m, d), lambda l, bi: (bi, 0, 0)),
                  pl.BlockSpec((None, d, d), lambda l, bi: (l, 0, 0))],
        out_specs=pl.BlockSpec((None, None, m, MEM_HEADS, MEM_HEAD_DIM), lambda l, bi: (l, bi, 0, 0, 0)),
        out_shape=jax.ShapeDtypeStruct((depth, bsz, m, MEM_HEADS, MEM_HEAD_DIM), F32),
        compiler_params=_params(("parallel", "parallel")),
        name="mem_proj",
    )(mem, w)


def _ab_out_kernel(x_ref, ya_ref, yb_ref, wa_ref, wb_ref, g_ref, b_ref, o_ref):
    y = (jnp.dot(ya_ref[...].astype(BF16), wa_ref[...], preferred_element_type=F32)
         + jnp.dot(yb_ref[...].astype(BF16), wb_ref[...], preferred_element_type=F32))
    o_ref[...] = _layer_norm(ALPHA * x_ref[...] + y, g_ref[...], b_ref[...], LN_EPS)


def _ab_out_postnorm(x, ya, yb, wa, wb, g, b):
    n, d = x.shape
    ka = ya.shape[1]
    kb = yb.shape[1]
    tm = _row_tile(n)
    row = lambda i: (i, 0)
    const = lambda i: (0, 0)
    return pl.pallas_call(
        _ab_out_kernel,
        grid=(n // tm,),
        in_specs=[pl.BlockSpec((tm, d), row), pl.BlockSpec((tm, ka), row), pl.BlockSpec((tm, kb), row),
                  pl.BlockSpec((ka, d), const), pl.BlockSpec((kb, d), const),
                  pl.BlockSpec((1, d), const), pl.BlockSpec((1, d), const)],
        out_specs=pl.BlockSpec((tm, d), row),
        out_shape=jax.ShapeDtypeStruct((n, d), F32),
        compiler_params=_params(("parallel",)),
        name="ab_out_postnorm",
    )(x, ya, yb, wa, wb, g, b)


def _xattn_kernel(x_ref, k_ref, v_ref, wq_ref, wo_ref, g_ref, b_ref, o_ref, kb_s, vb_s):
    @pl.when(pl.program_id(1) == 0)
    def _():
        for h in range(MEM_HEADS):
            kb_s[h] = k_ref[:, h, :].astype(BF16)
            vb_s[h] = v_ref[:, h, :].astype(BF16)

    x = x_ref[0]
    q = jnp.dot(x.astype(BF16), wq_ref[...], preferred_element_type=F32)
    qb = q.astype(BF16)
    outs = []
    for h in range(MEM_HEADS):
        sl = slice(h * MEM_HEAD_DIM, (h + 1) * MEM_HEAD_DIM)
        s = lax.dot_general(qb[:, sl], kb_s[h], _NT, preferred_element_type=F32)
        m = jnp.max(s, axis=-1, keepdims=True)
        p = jnp.exp(s - m)
        p = p * (1.0 / jnp.sum(p, axis=-1, keepdims=True))
        outs.append(jnp.dot(p.astype(BF16), vb_s[h], preferred_element_type=F32))
    o = jnp.concatenate(outs, axis=-1)
    y = jnp.dot(o.astype(BF16), wo_ref[...], preferred_element_type=F32)
    o_ref[0] = _layer_norm(ALPHA * x + y, g_ref[...], b_ref[...], LN_EPS)


def _xattn_postnorm(x, mem_k, mem_v, wq, wo, layer, g, b):
    bsz, t, d = x.shape
    tm = _row_tile(t)
    const = lambda bi, ti: (0, 0)
    mem_spec = pl.BlockSpec((None, None, N_MEM, MEM_HEADS, MEM_HEAD_DIM), lambda bi, ti: (layer, bi, 0, 0, 0))
    w_spec = pl.BlockSpec((None, d, d), lambda bi, ti: (layer, 0, 0))
    return pl.pallas_call(
        _xattn_kernel,
        grid=(bsz, t // tm),
        in_specs=[pl.BlockSpec((1, tm, d), lambda bi, ti: (bi, ti, 0)),
                  mem_spec, mem_spec, w_spec, w_spec,
                  pl.BlockSpec((1, d), const), pl.BlockSpec((1, d), const)],
        out_specs=pl.BlockSpec((1, tm, d), lambda bi, ti: (bi, ti, 0)),
        out_shape=jax.ShapeDtypeStruct((bsz, t, d), F32),
        scratch_shapes=[pltpu.VMEM((MEM_HEADS, N_MEM, MEM_HEAD_DIM), BF16)] * 2,
        compiler_params=_params(("parallel", "arbitrary")),
        name="xattn_postnorm",
    )(x, mem_k, mem_v, wq, wo, g, b)


def _lru_kernel(xa_ref, ga_ref, cbuf_ref, h0_ref, cw_ref, cb_ref, wa_ref, ba_ref, wx_ref, bx_ref, lam_ref,
                ya_ref, nbuf_ref, hlast_ref, prev_s, h_s, a_s, u_s, *, tc):
    t = pl.program_id(1)

    @pl.when(t == 0)
    def _():
        prev_s[...] = jnp.zeros(prev_s.shape, F32)
        prev_s[V7X_SUBLANES - (CONV_WIDTH - 1):, :] = cbuf_ref[0]
        h_s[...] = h0_ref[0]

    xa = xa_ref[0]
    ext = jnp.concatenate([prev_s[...], xa], axis=0)
    xc = cb_ref[...] + xa * cw_ref[CONV_WIDTH - 1:CONV_WIDTH, :]
    for s in range(1, CONV_WIDTH):
        shifted = pltpu.roll(ext, s, 0)[V7X_SUBLANES:, :]
        xc = xc + shifted * cw_ref[CONV_WIDTH - 1 - s:CONV_WIDTH - s, :]
    prev_s[...] = xa[tc - V7X_SUBLANES:, :]
    nbuf_ref[0] = xa[tc - (CONV_WIDTH - 1):, :]

    xcb = xc.astype(BF16)
    r = _sigmoid(jnp.dot(xcb, wa_ref[...], preferred_element_type=F32) + ba_ref[...])
    i = _sigmoid(jnp.dot(xcb, wx_ref[...], preferred_element_type=F32) + bx_ref[...])
    log_a = (-LRU_C * _softplus(-lam_ref[...])) * r
    a = jnp.exp(log_a)
    th = jnp.tanh(log_a)
    u = jnp.sqrt(-2.0 * th / (1.0 - th)) * (i * xc)

    row8 = lax.broadcasted_iota(jnp.int32, a.shape, 0) % V7X_SUBLANES
    s = 1
    while s < V7X_SUBLANES:
        inside = row8 >= s
        u = jnp.where(inside, u + a * pltpu.roll(u, s, 0), u)
        a = jnp.where(inside, a * pltpu.roll(a, s, 0), a)
        s *= 2
    a_s[...] = a
    u_s[...] = u

    def body(j, h):
        base = pl.multiple_of(j * V7X_SUBLANES, V7X_SUBLANES)
        hb = u_s[pl.ds(base, V7X_SUBLANES), :] + a_s[pl.ds(base, V7X_SUBLANES), :] * h
        u_s[pl.ds(base, V7X_SUBLANES), :] = hb
        return hb[V7X_SUBLANES - 1:, :]

    h = lax.fori_loop(0, tc // V7X_SUBLANES, body, h_s[...], unroll=4)
    h_s[...] = h
    hlast_ref[0] = h
    ya_ref[0] = u_s[...] * _gelu_tanh(ga_ref[0])


def _lru_branch(proj, conv_buf, h0, cw, cb, wa, ba, wx, bx, lam):
    bsz, t, _ = proj.shape
    w = LRU_WIDTH
    tc = _row_tile(t)
    const = lambda bi, ti: (0, 0)
    per_b = lambda bi, ti: (bi, 0, 0)
    return pl.pallas_call(
        functools.partial(_lru_kernel, tc=tc),
        grid=(bsz, t // tc),
        in_specs=[pl.BlockSpec((1, tc, w), lambda bi, ti: (bi, ti, 0)),
                  pl.BlockSpec((1, tc, w), lambda bi, ti: (bi, ti, 1)),
                  pl.BlockSpec((1, CONV_WIDTH - 1, w), per_b),
                  pl.BlockSpec((1, 1, w), per_b),
                  pl.BlockSpec((CONV_WIDTH, w), const), pl.BlockSpec((1, w), const),
                  pl.BlockSpec((w, w), const), pl.BlockSpec((1, w), const),
                  pl.BlockSpec((w, w), const), pl.BlockSpec((1, w), const),
                  pl.BlockSpec((1, w), const)],
        out_specs=[pl.BlockSpec((1, tc, w), lambda bi, ti: (bi, ti, 0)),
                   pl.BlockSpec((1, CONV_WIDTH - 1, w), per_b),
                   pl.BlockSpec((1, 1, w), per_b)],
        out_shape=[jax.ShapeDtypeStruct((bsz, t, w), F32),
                   jax.ShapeDtypeStruct((bsz, CONV_WIDTH - 1, w), F32),
                   jax.ShapeDtypeStruct((bsz, 1, w), F32)],
        scratch_shapes=[pltpu.VMEM((V7X_SUBLANES, w), F32), pltpu.VMEM((1, w), F32),
                        pltpu.VMEM((tc, w), F32), pltpu.VMEM((tc, w), F32)],
        compiler_params=_params(("parallel", "arbitrary")),
        name="lru_branch",
    )(proj, proj, conv_buf, h0, cw, cb, wa, ba, wx, bx, lam)


def _ret_kernel(q_ref, k_ref, v_ref, g_ref, cos_ref, sin_ref, s0_ref, gng_ref, gnb_ref,
                yb_ref, slast_ref, s_s, *, c, nch):
    t = pl.program_id(1)

    @pl.when(t == 0)
    def _():
        s_s[...] = s0_ref[0]

    half = RET_HEAD_DIM // 2
    ri = lax.broadcasted_iota(jnp.int32, (c, c), 0)
    ci = lax.broadcasted_iota(jnp.int32, (c, c), 1)
    diff = (ri - ci).astype(F32)
    it = lax.broadcasted_iota(jnp.int32, (c, 1), 0).astype(F32)
    log_g = [math.log1p(-(2.0 ** (-5.0 - h))) for h in range(RET_HEADS)]
    dmask = [jnp.where(diff >= 0, jnp.exp(lg * jnp.maximum(diff, 0.0)), 0.0) for lg in log_g]
    xi = [jnp.exp(lg * (it + 1.0)) for lg in log_g]
    zeta = [jnp.exp(lg * (c - 1.0 - it)) for lg in log_g]

    units = [(ch, h) for ch in range(nch) for h in range(RET_HEADS)]
    qr, inner, kv = {}, {}, {}
    for ch, h in units:
        rows = slice(ch * c, (ch + 1) * c)
        sl = slice(h * RET_HEAD_DIM, (h + 1) * RET_HEAD_DIM)
        cos2 = cos_ref[rows, :]
        sin2 = sin_ref[rows, :]
        qh = q_ref[0, rows, sl]
        kh = k_ref[0, rows, sl]
        vh = v_ref[0, rows, sl].astype(BF16)
        qr[ch, h] = (qh * cos2 + pltpu.roll(qh, half, 1) * sin2).astype(BF16)
        kr = (kh * cos2 + pltpu.roll(kh, half, 1) * sin2) * (RET_HEAD_DIM ** -0.5)
        scores = _dot_nt(qr[ch, h], kr) * dmask[h]
        inner[ch, h] = _dot(scores, vh)
        kv[ch, h] = _dot_tn(kr * zeta[h], vh)
    for h in range(RET_HEADS):
        sl = slice(h * RET_HEAD_DIM, (h + 1) * RET_HEAD_DIM)
        s = s_s[h]
        for ch in range(nch):
            rows = slice(ch * c, (ch + 1) * c)
            o = inner[ch, h] + _dot(qr[ch, h], s) * xi[h]
            s = math.exp(log_g[h] * c) * s + kv[ch, h]
            on = _layer_norm(o, gng_ref[:, sl], gnb_ref[:, sl], LN_EPS)
            gh = g_ref[0, rows, sl]
            yb_ref[0, rows, sl] = on * (gh * _sigmoid(gh))
        s_s[h] = s
    slast_ref[0] = s_s[...]


def _ret_branch(proj, cos2, sin2, s0, gn_g, gn_b, c):
    bsz, t, _ = proj.shape
    w = RET_WIDTH
    dk = RET_HEAD_DIM
    nch = RET_CHUNKS_PER_STEP if t % (RET_CHUNKS_PER_STEP * c) == 0 else 1
    const = lambda bi, ti: (0, 0)
    per_b = lambda bi, ti: (bi, 0, 0, 0)
    col = lambda j: (lambda bi, ti: (bi, ti, j))
    tc = nch * c
    return pl.pallas_call(
        functools.partial(_ret_kernel, c=c, nch=nch),
        grid=(bsz, t // tc),
        in_specs=[pl.BlockSpec((1, tc, w), col(2)), pl.BlockSpec((1, tc, w), col(3)),
                  pl.BlockSpec((1, tc, w), col(4)), pl.BlockSpec((1, tc, w), col(5)),
                  pl.BlockSpec((tc, dk), lambda bi, ti: (ti, 0)), pl.BlockSpec((tc, dk), lambda bi, ti: (ti, 0)),
                  pl.BlockSpec((1, RET_HEADS, dk, dk), per_b),
                  pl.BlockSpec((1, w), const), pl.BlockSpec((1, w), const)],
        out_specs=[pl.BlockSpec((1, tc, w), lambda bi, ti: (bi, ti, 0)),
                   pl.BlockSpec((1, RET_HEADS, dk, dk), per_b)],
        out_shape=[jax.ShapeDtypeStruct((bsz, t, w), F32),
                   jax.ShapeDtypeStruct((bsz, RET_HEADS, dk, dk), F32)],
        scratch_shapes=[pltpu.VMEM((RET_HEADS, dk, dk), F32)],
        compiler_params=_params(("parallel", "arbitrary")),
        name="ret_branch",
    )(proj, proj, proj, proj, cos2, sin2, s0, gn_g, gn_b)


def _rwkv_proj_kernel(x_ref, sh_ref, mu_ref, wrkv_ref, w0_ref, w1_ref, w2_ref, a0_ref, a1_ref, a2_ref,
                      g1_ref, g2_ref, kk_ref, ka_ref, rk_ref, sel_ref, selt_ref,
                      r_o, lw_o, k_o, v_o, kk_o, b_o, g_o, bonus_o, prev_s, *, tm):
    t = pl.program_id(1)

    @pl.when(t == 0)
    def _():
        prev_s[...] = jnp.zeros(prev_s.shape, F32)
        prev_s[V7X_SUBLANES - 1:, :] = sh_ref[0]

    x = x_ref[0]
    ext = jnp.concatenate([prev_s[...], x], axis=0)
    xp = pltpu.roll(ext, 1, 0)[V7X_SUBLANES:, :]
    prev_s[...] = x[tm - V7X_SUBLANES:, :]
    dx = xp - x

    def mix(p):
        return (x + dx * mu_ref[p:p + 1, :]).astype(BF16)

    r = jnp.dot(mix(0), wrkv_ref[0], preferred_element_type=F32)
    k = jnp.dot(mix(1), wrkv_ref[1], preferred_element_type=F32)
    v = jnp.dot(mix(2), wrkv_ref[2], preferred_element_type=F32)
    wl = _dot(jnp.tanh(jnp.dot(mix(3), w1_ref[...], preferred_element_type=F32)), w2_ref[...])
    w = -_softplus(-(w0_ref[...] + wl)) - 0.5
    al = _dot(jnp.dot(mix(4), a1_ref[...], preferred_element_type=F32), a2_ref[...])
    iclr = _sigmoid(a0_ref[...] + al)
    gate = _dot(_sigmoid(jnp.dot(mix(5), g1_ref[...], preferred_element_type=F32)), g2_ref[...])

    kk = k * kk_ref[...]
    sq = jnp.dot((kk * kk).astype(BF16), sel_ref[...], preferred_element_type=F32)
    inv_norm = 1.0 / jnp.maximum(jnp.sqrt(sq), 1e-12)
    kk = kk * _dot_split(inv_norm, selt_ref[...])
    k2 = k * (1.0 + (iclr - 1.0) * ka_ref[...])

    r_o[0] = r
    lw_o[0] = -jnp.exp(w)
    k_o[0] = k2
    v_o[0] = v.astype(BF16)
    kk_o[0] = kk
    b_o[0] = kk * iclr
    g_o[0] = gate
    bonus_o[0] = _seg_sum(r * k2 * rk_ref[...], sel_ref, selt_ref) * v


def _rwkv_proj(x, shift, mu, wrkv, w0, w1, w2, a0, a1, a2, g1, g2, k_k, k_a, r_k, sel, selt):
    bsz, t, d = x.shape
    tm = _row_tile(t, RWKV_PROJ_ROW_TILE)
    const2 = lambda bi, ti: (0, 0)
    const3 = lambda bi, ti: (0, 0, 0)
    full2 = lambda a: pl.BlockSpec(a.shape, const2)
    tile = pl.BlockSpec((1, tm, d), lambda bi, ti: (bi, ti, 0))
    out = jax.ShapeDtypeStruct((bsz, t, d), F32)
    return pl.pallas_call(
        functools.partial(_rwkv_proj_kernel, tm=tm),
        grid=(bsz, t // tm),
        in_specs=[tile, pl.BlockSpec((1, 1, d), lambda bi, ti: (bi, 0, 0)), full2(mu),
                  pl.BlockSpec(wrkv.shape, const3), full2(w0), full2(w1), full2(w2), full2(a0), full2(a1),
                  full2(a2), full2(g1), full2(g2), full2(k_k), full2(k_a), full2(r_k), full2(sel), full2(selt)],
        out_specs=[tile] * 8,
        out_shape=[out, out, out, jax.ShapeDtypeStruct((bsz, t, d), BF16), out, out, out, out],
        scratch_shapes=[pltpu.VMEM((V7X_SUBLANES, d), F32)],
        compiler_params=_params(("parallel", "arbitrary")),
        name="rwkv_proj",
    )(x, shift, mu, wrkv, w0, w1, w2, a0, a1, a2, g1, g2, k_k, k_a, r_k, sel, selt)


def _wkv_kernel(r_ref, lw_ref, k_ref, v_ref, kk_ref, b_ref, s0_ref, o_ref, slast_ref, s_s, *, c, nch):
    t = pl.program_id(1)
    n = RWKV_HEAD
    g = WKV_GROUP
    gl = g * n
    gc = g * c
    ngroups = RWKV_HEADS // g

    @pl.when(t == 0)
    def _():
        for h in range(RWKV_HEADS):
            s_s[:, h * n:(h + 1) * n] = s0_ref[0, h]

    iota = lambda shape, dim: lax.broadcasted_iota(jnp.int32, shape, dim)
    head_k = iota((1, gl), 1) // n
    head_j = iota((1, gc), 1) // c
    row_t = iota((c, gc), 0)
    col_j = iota((c, gc), 1) % c
    strict = row_t > col_j
    lower = row_t >= col_j
    eye_c = jnp.where(row_t == col_j, 1.0, 0.0).astype(F32)
    blk = (iota((gl, gl), 0) // n) == (iota((gl, gl), 1) // n)
    eye_l = iota((gl, gl), 0) == iota((gl, gl), 1)
    tri = jnp.where(iota((c, c), 0) >= iota((c, c), 1), 1.0, 0.0).astype(F32)

    def bd(z, lane_head):
        zb = z.astype(BF16)
        return jnp.concatenate([jnp.where(lane_head == h, zb, jnp.zeros_like(zb)) for h in range(g)], axis=0)

    def mm(a, b):
        return jnp.dot(a.astype(BF16), b, preferred_element_type=F32)

    def mm_nt(a, b):
        return lax.dot_general(a.astype(BF16), b, _NT, preferred_element_type=F32)

    units = [(ch, gi) for ch in range(nch) for gi in range(ngroups)]
    u = {}
    for ch in range(nch):
        rows = slice(ch * c, (ch + 1) * c)
        lw = lw_ref[0, rows, :]
        cum = jnp.dot(tri, lw, preferred_element_type=F32, precision=lax.Precision.HIGHEST)
        cum_last = cum[c - 1:c, :]
        e_neg = jnp.exp(-cum)
        e_end = jnp.exp(cum_last - cum)
        kk = kk_ref[0, rows, :]
        b = b_ref[0, rows, :]
        k = k_ref[0, rows, :]
        full = dict(a_t=kk * jnp.exp(cum - lw), b_t=b * e_neg, k_t=k * e_neg, r_t=r_ref[0, rows, :] * jnp.exp(cum),
                    b_e=b * e_end, k_e=k * e_end, d_end=jnp.exp(cum_last), v=v_ref[0, rows, :])
        for gi in range(ngroups):
            lanes = slice(gi * gl, (gi + 1) * gl)
            u[ch, gi] = {name: val[:, lanes] for name, val in full.items()}

    for key in units:
        d = u[key]
        x2 = jnp.concatenate([d['a_t'], d['r_t']], axis=0).astype(BF16)
        mb = mm_nt(x2, bd(d['b_t'], head_k))
        mk = mm_nt(x2, bd(d['k_t'], head_k))
        d['p_rb'] = jnp.where(lower, mb[c:], 0.0).astype(BF16)
        mk_lo = jnp.concatenate([jnp.where(strict, mk[:c], 0.0), jnp.where(lower, mk[c:], 0.0)], axis=0)
        wo = mm(mk_lo, bd(d['v'], head_k))
        d['w_h'] = wo[:c]
        d['o0'] = wo[c:]
        d['pw'] = -jnp.where(strict, mb[:c], 0.0)
        d['tinv'] = eye_c + d['pw']
    nsq = int(math.log2(c)) - 1
    for level in range(nsq):
        for key in units:
            d = u[key]
            pwb = d['pw'].astype(BF16)
            w = bd(pwb, head_j)
            if level == 0:
                d['pw'] = mm(pwb, w)
            else:
                res = mm(jnp.concatenate([pwb, d['tinv'].astype(BF16)], axis=0), w)
                d['pw'] = res[:c]
                d['tinv'] = d['tinv'] + res[c:]
    for key in units:
        d = u[key]
        d['tinv'] = d['tinv'] + mm(d['tinv'], bd(d['pw'], head_j))
    for key in units:
        d = u[key]
        tb = d['tinv'].astype(BF16)
        d['a_hat'] = mm(tb, bd(d['a_t'], head_k))
        d['w_hat'] = mm(tb, bd(d['w_h'], head_k))
    for key in units:
        d = u[key]
        d['r_hat'] = d['r_t'] - mm(d['p_rb'], bd(d['a_hat'], head_k))
        d['o0'] = d['o0'] - mm(d['p_rb'], bd(d['w_hat'], head_k))
        b_e = d['b_e'].astype(BF16)
        gfull = lax.dot_general(d['a_hat'].astype(BF16), b_e, _TN, preferred_element_type=F32)
        d['gmat'] = (jnp.where(eye_l, d['d_end'], 0.0) - jnp.where(blk, gfull, 0.0)).astype(BF16)
        vw = jnp.concatenate([d['v'].astype(BF16), d['w_hat'].astype(BF16)], axis=0)
        kb = jnp.concatenate([d['k_e'].astype(BF16), -b_e], axis=0)
        zf = lax.dot_general(vw, kb, _TN, preferred_element_type=F32)
        hmat = jnp.where(head_k == 0, zf[:n], 0.0)
        for h in range(1, g):
            hmat = hmat + jnp.where(head_k == h, zf[h * n:(h + 1) * n], 0.0)
        d['hmat'] = hmat

    for gi in range(ngroups):
        lanes = slice(gi * gl, (gi + 1) * gl)
        s = s_s[:, lanes]
        for ch in range(nch):
            d = u[ch, gi]
            o_ref[0, ch * c:(ch + 1) * c, lanes] = d['o0'] + mm_nt(d['r_hat'], bd(s, head_k))
            s = mm(s, d['gmat']) + d['hmat']
        s_s[:, lanes] = s

    @pl.when(t == pl.num_programs(1) - 1)
    def _():
        for h in range(RWKV_HEADS):
            slast_ref[0, h] = s_s[:, h * n:(h + 1) * n]


def _wkv_scan(r, lw, k, v, kk, b, s0, c):
    bsz, t, d = r.shape
    n = RWKV_HEAD
    nch = WKV_CHUNKS_PER_STEP if t % (WKV_CHUNKS_PER_STEP * c) == 0 else 1
    tile = pl.BlockSpec((1, nch * c, d), lambda bi, ti: (bi, ti, 0))
    st = pl.BlockSpec((1, RWKV_HEADS, n, n), lambda bi, ti: (bi, 0, 0, 0))
    return pl.pallas_call(
        functools.partial(_wkv_kernel, c=c, nch=nch),
        grid=(bsz, t // (nch * c)),
        in_specs=[tile] * 6 + [st],
        out_specs=[tile, st],
        out_shape=[jax.ShapeDtypeStruct((bsz, t, d), F32), jax.ShapeDtypeStruct(s0.shape, F32)],
        scratch_shapes=[pltpu.VMEM((n, d), F32)],
        compiler_params=_params(("parallel", "arbitrary")),
        name="wkv_scan",
    )(r, lw, k, v, kk, b, s0)


def _wkv_out_kernel(x_ref, o_ref, bonus_ref, gate_ref, gng_ref, gnb_ref, sel_ref, selt_ref, w_ref,
                    g_ref, b_ref, y_ref):
    o = o_ref[...]
    inv_n = 1.0 / RWKV_HEAD
    mu = _seg_sum(o, sel_ref, selt_ref) * inv_n
    oc = o - mu
    var = _seg_sum(oc * oc, sel_ref, selt_ref) * inv_n
    on = oc * lax.rsqrt(var + RWKV_GN_EPS) * gng_ref[...] + gnb_ref[...]
    z = (on + bonus_ref[...]) * gate_ref[...]
    y = jnp.dot(z.astype(BF16), w_ref[...], preferred_element_type=F32)
    y_ref[...] = _layer_norm(ALPHA * x_ref[...] + y, g_ref[...], b_ref[...], LN_EPS)


def _wkv_out_postnorm(x, o, bonus, gate, gn_g, gn_b, sel, selt, w, g, b):
    n, d = x.shape
    tm = _row_tile(n)
    row = pl.BlockSpec((tm, d), lambda i: (i, 0))
    const = lambda i: (0, 0)
    full = lambda a: pl.BlockSpec(a.shape, const)
    return pl.pallas_call(
        _wkv_out_kernel,
        grid=(n // tm,),
        in_specs=[row, row, row, row, full(gn_g), full(gn_b), full(sel), full(selt), full(w), full(g), full(b)],
        out_specs=row,
        out_shape=jax.ShapeDtypeStruct((n, d), F32),
        compiler_params=_params(("parallel",)),
        name="wkv_out_postnorm",
    )(x, o, bonus, gate, gn_g, gn_b, sel, selt, w, g, b)


def _prep_weights(ln_g, ln_b, ffn_up, ffn_down, xa_q, xa_o, l0_w_in, l0_conv_w, l0_conv_b, l0_lru_wa,
                  l0_lru_ba, l0_lru_wx, l0_lru_bx, l0_lru_lambda, l0_ret_gn_g, l0_ret_gn_b, l0_w_out,
                  l1_mu, l1_w_rkv, l1_w0, l1_w1, l1_w2, l1_a0, l1_a1, l1_a2, l1_g1, l1_g2, l1_k_k, l1_k_a,
                  l1_r_k, l1_gn_g, l1_gn_b, l1_w_out):
    d = D_MODEL
    row = lambda a: a.reshape(1, -1).astype(F32)

    def block_diag(w):
        eye = jnp.eye(LRU_BLOCKS, dtype=w.dtype)
        return jnp.einsum('gij,gh->gihj', w, eye).reshape(LRU_WIDTH, LRU_WIDTH).astype(BF16)

    head_of_col = jnp.arange(d) // RWKV_HEAD
    sel = (head_of_col[:, None] == jnp.arange(SEL_WIDTH)[None, :]).astype(BF16)
    return dict(
        ln_g=ln_g, ln_b=ln_b, ffn_up=ffn_up.astype(BF16), ffn_down=ffn_down.astype(BF16),
        xa_q=(xa_q * (MEM_HEAD_DIM ** -0.5)).astype(BF16), xa_o=xa_o.astype(BF16),
        w_in=l0_w_in.astype(BF16), conv_w=l0_conv_w, conv_b=row(l0_conv_b),
        lru_wa=block_diag(l0_lru_wa), lru_ba=row(l0_lru_ba), lru_wx=block_diag(l0_lru_wx),
        lru_bx=row(l0_lru_bx), lru_lam=row(l0_lru_lambda),
        ret_gn_g=row(l0_ret_gn_g), ret_gn_b=row(l0_ret_gn_b),
        w_out_a=l0_w_out[:LRU_WIDTH].astype(BF16), w_out_b=l0_w_out[LRU_WIDTH:].astype(BF16),
        mu=l1_mu, w_rkv=l1_w_rkv.astype(BF16), w0=row(l1_w0), w1=l1_w1.astype(BF16), w2=l1_w2.astype(BF16),
        a0=row(l1_a0), a1=l1_a1.astype(BF16), a2=l1_a2.astype(BF16), g1=l1_g1.astype(BF16),
        g2=l1_g2.astype(BF16), k_k=row(l1_k_k), k_a=row(l1_k_a), r_k=row(l1_r_k),
        gn_g=row(l1_gn_g), gn_b=row(l1_gn_b), w_out_c=l1_w_out.astype(BF16), sel=sel, selt=sel.T,
    )


def _rotary_tables(pos):
    half = RET_HEAD_DIM // 2
    inv_freq = ROPE_BASE ** (-jnp.arange(half, dtype=F32) / half)
    ang = pos.astype(F32)[:, None] * inv_freq[None, :]
    cos = jnp.cos(ang)
    sin = jnp.sin(ang)
    return jnp.concatenate([cos, cos], axis=-1), jnp.concatenate([-sin, sin], axis=-1)


def _run_trunk(x, pos, mem_k, mem_v, states, p):
    bsz, t, d = x.shape
    n = bsz * t
    (conv_buf, h0, s_ret), (shift, s_wkv) = states
    lng = lambda l, j: p['ln_g'][l, j].reshape(1, d)
    lnb = lambda l, j: p['ln_b'][l, j].reshape(1, d)
    flat = lambda a: a.reshape(n, a.shape[-1])
    chunk = min(CHUNK, t)
    ffn = lambda xx, l, j, nj: _ffn_postnorm(xx, p['ffn_up'], p['ffn_down'], l, j, lng(l, nj), lnb(l, nj))

    x = ffn(flat(x), 0, 0, 0)
    proj = _matmul(x, p['w_in']).reshape(bsz, t, -1)
    ya, new_buf, h_last = _lru_branch(proj, conv_buf, h0.reshape(bsz, 1, LRU_WIDTH), p['conv_w'], p['conv_b'],
                                      p['lru_wa'], p['lru_ba'], p['lru_wx'], p['lru_bx'], p['lru_lam'])
    cos2, sin2 = _rotary_tables(pos)
    yb, s_ret_new = _ret_branch(proj, cos2, sin2, s_ret, p['ret_gn_g'], p['ret_gn_b'], chunk)
    x = _ab_out_postnorm(x, flat(ya), flat(yb), p['w_out_a'], p['w_out_b'], lng(0, 1), lnb(0, 1))
    x = _xattn_postnorm(x.reshape(bsz, t, d), mem_k, mem_v, p['xa_q'], p['xa_o'], 0, lng(0, 2), lnb(0, 2))
    x = ffn(flat(x), 0, 1, 3)

    x = ffn(x, 1, 0, 0)
    x3 = x.reshape(bsz, t, d)
    r, lw, k2, v, kk, b, gate, bonus = _rwkv_proj(
        x3, shift, p['mu'], p['w_rkv'], p['w0'], p['w1'], p['w2'], p['a0'], p['a1'], p['a2'], p['g1'], p['g2'],
        p['k_k'], p['k_a'], p['r_k'], p['sel'], p['selt'])
    o, s_wkv_new = _wkv_scan(r, lw, k2, v, kk, b, s_wkv, chunk)
    new_shift = x3[:, t - 1:, :]
    x = _wkv_out_postnorm(x, flat(o), flat(bonus), flat(gate), p['gn_g'], p['gn_b'], p['sel'], p['selt'],
                          p['w_out_c'], lng(1, 1), lnb(1, 1))
    x = _xattn_postnorm(x.reshape(bsz, t, d), mem_k, mem_v, p['xa_q'], p['xa_o'], 1, lng(1, 2), lnb(1, 2))
    x = ffn(flat(x), 1, 1, 3)
    new_states = ((new_buf, h_last.reshape(bsz, LRU_WIDTH), s_ret_new), (new_shift, s_wkv_new))
    return x.reshape(bsz, t, d), new_states


def kernel(x_prompt, x_sample, mem_prompt, state_conv0, state_lru0, state_ret0, state_shift1, state_wkv1,
           cache_mem_k, cache_mem_v, ln_g, ln_b, ffn_up, ffn_down, xa_q, xa_k, xa_v, xa_o,
           l0_w_in, l0_conv_w, l0_conv_b, l0_lru_wa, l0_lru_ba, l0_lru_wx, l0_lru_bx, l0_lru_lambda,
           l0_ret_gn_g, l0_ret_gn_b, l0_w_out, l1_mu, l1_w_rkv, l1_w0, l1_w1, l1_w2, l1_a0, l1_a1, l1_a2,
           l1_g1, l1_g2, l1_k_k, l1_k_a, l1_r_k, l1_gn_g, l1_gn_b, l1_w_out):
    d = D_MODEL
    p = _prep_weights(ln_g, ln_b, ffn_up, ffn_down, xa_q, xa_o, l0_w_in, l0_conv_w, l0_conv_b, l0_lru_wa,
                      l0_lru_ba, l0_lru_wx, l0_lru_bx, l0_lru_lambda, l0_ret_gn_g, l0_ret_gn_b, l0_w_out,
                      l1_mu, l1_w_rkv, l1_w0, l1_w1, l1_w2, l1_a0, l1_a1, l1_a2, l1_g1, l1_g2, l1_k_k, l1_k_a,
                      l1_r_k, l1_gn_g, l1_gn_b, l1_w_out)

    bp, tp, _ = x_prompt.shape
    dt = x_prompt.dtype
    mem_k_p = _mem_proj(mem_prompt, xa_k.astype(BF16))
    mem_v_p = _mem_proj(mem_prompt, xa_v.astype(BF16))
    zero_states = ((jnp.zeros((bp, CONV_WIDTH - 1, LRU_WIDTH), dt), jnp.zeros((bp, LRU_WIDTH), dt),
                    jnp.zeros((bp, RET_HEADS, RET_HEAD_DIM, RET_HEAD_DIM), dt)),
                   (jnp.zeros((bp, 1, d), dt), jnp.zeros((bp, RWKV_HEADS, RWKV_HEAD, RWKV_HEAD), dt)))
    y_prompt, st_p = _run_trunk(x_prompt, jnp.arange(tp, dtype=jnp.int32), mem_k_p, mem_v_p, zero_states, p)

    bs, ts, _ = x_sample.shape
    pos_s = PAST_LEN + jnp.arange(ts, dtype=jnp.int32)
    sample_states = ((state_conv0, state_lru0, state_ret0), (state_shift1, state_wkv1))
    y_sample, st_s = _run_trunk(x_sample, pos_s, cache_mem_k, cache_mem_v, sample_states, p)

    (p_conv0, p_lru0, p_ret0), (p_shift1, p_wkv1) = st_p
    (s_conv0, s_lru0, s_ret0), (s_shift1, s_wkv1) = st_s
    return (y_prompt, y_sample, mem_k_p, mem_v_p,
            p_conv0, p_lru0, p_ret0, p_shift1, p_wkv1, s_conv0, s_lru0, s_ret0, s_shift1, s_wkv1)
```

```python
import functools
import math

import jax
import jax.numpy as jnp
from jax import lax
from jax.experimental import pallas as pl
from jax.experimental.pallas import tpu as pltpu

D_MODEL = 1024
DEPTH = 2
PAST_LEN = 4096
CHUNK = 64
N_MEM = 256
MEM_HEADS = 4
MEM_HEAD_DIM = D_MODEL // MEM_HEADS
D_FF = 2816
LRU_WIDTH = 512
LRU_BLOCKS = 8
LRU_BLOCK = LRU_WIDTH // LRU_BLOCKS
CONV_WIDTH = 4
LRU_C = 8.0
RET_HEADS = 4
RET_WIDTH = 512
RET_HEAD_DIM = RET_WIDTH // RET_HEADS
ROPE_BASE = 10000.0
RWKV_HEAD = 64
RWKV_HEADS = D_MODEL // RWKV_HEAD
LN_EPS = 1e-5
RWKV_GN_EPS = 64e-5
ALPHA = (2 * DEPTH) ** 0.25

F32 = jnp.float32
BF16 = jnp.bfloat16

V7X_SUBLANES = 8
V7X_LANES = 128
VMEM_LIMIT_BYTES = 56 * 1024 * 1024
FFN_CHUNK = 256
ROW_TILE = 1024
FFN_ROW_TILE = 1024
RWKV_PROJ_ROW_TILE = 512
SEL_WIDTH = V7X_LANES
WKV_GROUP = 4
WKV_CHUNKS_PER_STEP = 4
RET_CHUNKS_PER_STEP = 8

_NT = (((1,), (1,)), ((), ()))
_TN = (((0,), (0,)), ((), ()))


def _dot(a, b):
    return jnp.dot(a.astype(BF16), b.astype(BF16), preferred_element_type=F32)


def _dot_nt(a, b):
    return lax.dot_general(a.astype(BF16), b.astype(BF16), _NT, preferred_element_type=F32)


def _dot_tn(a, b):
    return lax.dot_general(a.astype(BF16), b.astype(BF16), _TN, preferred_element_type=F32)


def _head_bcast(vals, selt_ref):
    lane = lax.broadcasted_iota(jnp.int32, vals.shape, 1)
    vals = jnp.where(lane < RWKV_HEADS, vals, 0.0)
    hi = vals.astype(BF16).astype(F32)
    packed = hi + pltpu.roll(vals - hi, RWKV_HEADS, 1)
    return jnp.dot(packed.astype(BF16), selt_ref[...], preferred_element_type=F32)


def _seg_sum(z, sel_ref, selt_ref):
    sums = jnp.dot(z.astype(BF16), sel_ref[...], preferred_element_type=F32)
    return _head_bcast(sums, selt_ref)


def _layer_norm(y, g, b, eps):
    mu = jnp.mean(y, axis=-1, keepdims=True)
    yc = y - mu
    var = jnp.mean(yc * yc, axis=-1, keepdims=True)
    return yc * lax.rsqrt(var + eps) * g + b


def _sigmoid(x):
    return 1.0 / (1.0 + jnp.exp(-x))


def _softplus(x):
    return jnp.maximum(x, 0.0) + jnp.log1p(jnp.exp(-jnp.abs(x)))


def _gelu_tanh(x):
    return 0.5 * x * (1.0 + jnp.tanh(math.sqrt(2.0 / math.pi) * (x + 0.044715 * (x * x * x))))


def _params(sem):
    return pltpu.CompilerParams(dimension_semantics=sem, vmem_limit_bytes=VMEM_LIMIT_BYTES)


def _row_tile(n, pref=ROW_TILE):
    return pref if n % pref == 0 else n


def _ffn_kernel(x_ref, wup_ref, wdn_ref, g_ref, b_ref, o_ref):
    x = x_ref[...]
    xb = x.astype(BF16)
    acc = jnp.zeros(x.shape, F32)
    for lo in range(0, D_FF, FFN_CHUNK):
        hg = jnp.dot(xb, wup_ref[:, lo:lo + FFN_CHUNK], preferred_element_type=F32)
        hu = jnp.dot(xb, wup_ref[:, D_FF + lo:D_FF + lo + FFN_CHUNK], preferred_element_type=F32)
        h = hg * _sigmoid(hg) * hu
        acc = acc + jnp.dot(h.astype(BF16), wdn_ref[lo:lo + FFN_CHUNK, :], preferred_element_type=F32)
    o_ref[...] = _layer_norm(ALPHA * x + 0.5 * acc, g_ref[...], b_ref[...], LN_EPS)


def _ffn_postnorm(x, w_up, w_down, layer, which, g, b):
    n, d = x.shape
    tm = _row_tile(n, FFN_ROW_TILE)
    pick = lambda i: (layer, which, 0, 0)
    return pl.pallas_call(
        _ffn_kernel,
        grid=(n // tm,),
        in_specs=[
            pl.BlockSpec((tm, d), lambda i: (i, 0)),
            pl.BlockSpec((None, None, d, 2 * D_FF), pick, pipeline_mode=pl.Buffered(1)),
            pl.BlockSpec((None, None, D_FF, d), pick, pipeline_mode=pl.Buffered(1)),
            pl.BlockSpec((1, d), lambda i: (0, 0)),
            pl.BlockSpec((1, d), lambda i: (0, 0)),
        ],
        out_specs=pl.BlockSpec((tm, d), lambda i: (i, 0)),
        out_shape=jax.ShapeDtypeStruct((n, d), F32),
        compiler_params=_params(("parallel",)),
        name="ffn_postnorm",
    )(x, w_up, w_down, g, b)


def _mm_kernel(x_ref, w_ref, o_ref):
    o_ref[...] = jnp.dot(x_ref[...].astype(BF16), w_ref[...], preferred_element_type=F32).astype(o_ref.dtype)


def _matmul(x, w, out_dtype):
    n, k = x.shape
    m = w.shape[1]
    tm = _row_tile(n)
    return pl.pallas_call(
        _mm_kernel,
        grid=(n // tm,),
        in_specs=[pl.BlockSpec((tm, k), lambda i: (i, 0)), pl.BlockSpec((k, m), lambda i: (0, 0))],
        out_specs=pl.BlockSpec((tm, m), lambda i: (i, 0)),
        out_shape=jax.ShapeDtypeStruct((n, m), out_dtype),
        compiler_params=_params(("parallel",)),
        name="proj",
    )(x, w)


def _mem_proj_kernel(x_ref, w_ref, o_ref):
    y = jnp.dot(x_ref[...].astype(BF16), w_ref[...], preferred_element_type=F32)
    for h in range(MEM_HEADS):
        o_ref[:, h, :] = y[:, h * MEM_HEAD_DIM:(h + 1) * MEM_HEAD_DIM]


def _mem_proj(mem, w):
    bsz, m, d = mem.shape
    depth = w.shape[0]
    return pl.pallas_call(
        _mem_proj_kernel,
        grid=(depth, bsz),
        in_specs=[pl.BlockSpec((None, m, d), lambda l, bi: (bi, 0, 0)),
                  pl.BlockSpec((None, d, d), lambda l, bi: (l, 0, 0))],
        out_specs=pl.BlockSpec((None, None, m, MEM_HEADS, MEM_HEAD_DIM), lambda l, bi: (l, bi, 0, 0, 0)),
        out_shape=jax.ShapeDtypeStruct((depth, bsz, m, MEM_HEADS, MEM_HEAD_DIM), F32),
        compiler_params=_params(("parallel", "parallel")),
        name="mem_proj",
    )(mem, w)


def _ab_out_kernel(x_ref, ya_ref, yb_ref, wa_ref, wb_ref, g_ref, b_ref, o_ref):
    y = (jnp.dot(ya_ref[...].astype(BF16), wa_ref[...], preferred_element_type=F32)
         + jnp.dot(yb_ref[...].astype(BF16), wb_ref[...], preferred_element_type=F32))
    o_ref[...] = _layer_norm(ALPHA * x_ref[...] + y, g_ref[...], b_ref[...], LN_EPS)


def _ab_out_postnorm(x, ya, yb, wa, wb, g, b):
    n, d = x.shape
    ka = ya.shape[1]
    kb = yb.shape[1]
    tm = _row_tile(n)
    row = lambda i: (i, 0)
    const = lambda i: (0, 0)
    return pl.pallas_call(
        _ab_out_kernel,
        grid=(n // tm,),
        in_specs=[pl.BlockSpec((tm, d), row), pl.BlockSpec((tm, ka), row), pl.BlockSpec((tm, kb), row),
                  pl.BlockSpec((ka, d), const), pl.BlockSpec((kb, d), const),
                  pl.BlockSpec((1, d), const), pl.BlockSpec((1, d), const)],
        out_specs=pl.BlockSpec((tm, d), row),
        out_shape=jax.ShapeDtypeStruct((n, d), F32),
        compiler_params=_params(("parallel",)),
        name="ab_out_postnorm",
    )(x, ya, yb, wa, wb, g, b)


def _xattn_kernel(x_ref, k_ref, v_ref, wq_ref, wo_ref, g_ref, b_ref, o_ref, kb_s, vb_s):
    @pl.when(pl.program_id(1) == 0)
    def _():
        for h in range(MEM_HEADS):
            kb_s[h] = k_ref[:, h, :].astype(BF16)
            vb_s[h] = v_ref[:, h, :].astype(BF16)

    x = x_ref[0]
    q = jnp.dot(x.astype(BF16), wq_ref[...], preferred_element_type=F32)
    qb = q.astype(BF16)
    outs = []
    for h in range(MEM_HEADS):
        sl = slice(h * MEM_HEAD_DIM, (h + 1) * MEM_HEAD_DIM)
        s = lax.dot_general(qb[:, sl], kb_s[h], _NT, preferred_element_type=F32)
        m = jnp.max(s, axis=-1, keepdims=True)
        p = jnp.exp(s - m)
        p = p * (1.0 / jnp.sum(p, axis=-1, keepdims=True))
        outs.append(jnp.dot(p.astype(BF16), vb_s[h], preferred_element_type=F32))
    o = jnp.concatenate(outs, axis=-1)
    y = jnp.dot(o.astype(BF16), wo_ref[...], preferred_element_type=F32)
    o_ref[0] = _layer_norm(ALPHA * x + y, g_ref[...], b_ref[...], LN_EPS)


def _xattn_postnorm(x, mem_k, mem_v, wq, wo, layer, g, b):
    bsz, t, d = x.shape
    tm = _row_tile(t)
    const = lambda bi, ti: (0, 0)
    mem_spec = pl.BlockSpec((None, None, N_MEM, MEM_HEADS, MEM_HEAD_DIM), lambda bi, ti: (layer, bi, 0, 0, 0))
    w_spec = pl.BlockSpec((None, d, d), lambda bi, ti: (layer, 0, 0))
    return pl.pallas_call(
        _xattn_kernel,
        grid=(bsz, t // tm),
        in_specs=[pl.BlockSpec((1, tm, d), lambda bi, ti: (bi, ti, 0)),
                  mem_spec, mem_spec, w_spec, w_spec,
                  pl.BlockSpec((1, d), const), pl.BlockSpec((1, d), const)],
        out_specs=pl.BlockSpec((1, tm, d), lambda bi, ti: (bi, ti, 0)),
        out_shape=jax.ShapeDtypeStruct((bsz, t, d), F32),
        scratch_shapes=[pltpu.VMEM((MEM_HEADS, N_MEM, MEM_HEAD_DIM), BF16)] * 2,
        compiler_params=_params(("parallel", "arbitrary")),
        name="xattn_postnorm",
    )(x, mem_k, mem_v, wq, wo, g, b)


def _lru_kernel(xa_ref, ga_ref, cbuf_ref, h0_ref, cw_ref, cb_ref, wa_ref, ba_ref, wx_ref, bx_ref, lam_ref,
                ya_ref, nbuf_ref, hlast_ref, prev_s, h_s, a_s, u_s, *, tc):
    t = pl.program_id(1)

    @pl.when(t == 0)
    def _():
        prev_s[...] = jnp.zeros(prev_s.shape, F32)
        prev_s[V7X_SUBLANES - (CONV_WIDTH - 1):, :] = cbuf_ref[0]
        h_s[...] = h0_ref[0]

    xa = xa_ref[0].astype(F32)
    ext = jnp.concatenate([prev_s[...], xa], axis=0)
    xc = cb_ref[...] + xa * cw_ref[CONV_WIDTH - 1:CONV_WIDTH, :]
    for s in range(1, CONV_WIDTH):
        shifted = pltpu.roll(ext, s, 0)[V7X_SUBLANES:, :]
        xc = xc + shifted * cw_ref[CONV_WIDTH - 1 - s:CONV_WIDTH - s, :]
    prev_s[...] = xa[tc - V7X_SUBLANES:, :]
    nbuf_ref[0] = xa[tc - (CONV_WIDTH - 1):, :]

    xcb = xc.astype(BF16)
    r = _sigmoid(jnp.dot(xcb, wa_ref[...], preferred_element_type=F32) + ba_ref[...])
    i = _sigmoid(jnp.dot(xcb, wx_ref[...], preferred_element_type=F32) + bx_ref[...])
    log_a = (-LRU_C * _softplus(-lam_ref[...])) * r
    a = jnp.exp(log_a)
    th = jnp.tanh(log_a)
    u = jnp.sqrt(-2.0 * th / (1.0 - th)) * (i * xc)

    row8 = lax.broadcasted_iota(jnp.int32, a.shape, 0) % V7X_SUBLANES
    s = 1
    while s < V7X_SUBLANES:
        inside = row8 >= s
        u = jnp.where(inside, u + a * pltpu.roll(u, s, 0), u)
        a = jnp.where(inside, a * pltpu.roll(a, s, 0), a)
        s *= 2
    a_s[...] = a
    u_s[...] = u

    def body(j, h):
        base = pl.multiple_of(j * V7X_SUBLANES, V7X_SUBLANES)
        hb = u_s[pl.ds(base, V7X_SUBLANES), :] + a_s[pl.ds(base, V7X_SUBLANES), :] * h
        u_s[pl.ds(base, V7X_SUBLANES), :] = hb
        return hb[V7X_SUBLANES - 1:, :]

    h = lax.fori_loop(0, tc // V7X_SUBLANES, body, h_s[...], unroll=4)
    h_s[...] = h
    hlast_ref[0] = h
    ya_ref[0] = u_s[...] * _gelu_tanh(ga_ref[0].astype(F32))


def _lru_branch(proj, conv_buf, h0, cw, cb, wa, ba, wx, bx, lam):
    bsz, t, _ = proj.shape
    w = LRU_WIDTH
    tc = _row_tile(t)
    const = lambda bi, ti: (0, 0)
    per_b = lambda bi, ti: (bi, 0, 0)
    return pl.pallas_call(
        functools.partial(_lru_kernel, tc=tc),
        grid=(bsz, t // tc),
        in_specs=[pl.BlockSpec((1, tc, w), lambda bi, ti: (bi, ti, 0)),
                  pl.BlockSpec((1, tc, w), lambda bi, ti: (bi, ti, 1)),
                  pl.BlockSpec((1, CONV_WIDTH - 1, w), per_b),
                  pl.BlockSpec((1, 1, w), per_b),
                  pl.BlockSpec((CONV_WIDTH, w), const), pl.BlockSpec((1, w), const),
                  pl.BlockSpec((w, w), const), pl.BlockSpec((1, w), const),
                  pl.BlockSpec((w, w), const), pl.BlockSpec((1, w), const),
                  pl.BlockSpec((1, w), const)],
        out_specs=[pl.BlockSpec((1, tc, w), lambda bi, ti: (bi, ti, 0)),
                   pl.BlockSpec((1, CONV_WIDTH - 1, w), per_b),
                   pl.BlockSpec((1, 1, w), per_b)],
        out_shape=[jax.ShapeDtypeStruct((bsz, t, w), F32),
                   jax.ShapeDtypeStruct((bsz, CONV_WIDTH - 1, w), F32),
                   jax.ShapeDtypeStruct((bsz, 1, w), F32)],
        scratch_shapes=[pltpu.VMEM((V7X_SUBLANES, w), F32), pltpu.VMEM((1, w), F32),
                        pltpu.VMEM((tc, w), F32), pltpu.VMEM((tc, w), F32)],
        compiler_params=_params(("parallel", "arbitrary")),
        name="lru_branch",
    )(proj, proj, conv_buf, h0, cw, cb, wa, ba, wx, bx, lam)


def _ret_kernel(q_ref, k_ref, v_ref, g_ref, cos_ref, sin_ref, s0_ref, gng_ref, gnb_ref,
                yb_ref, slast_ref, s_s, *, c, nch):
    t = pl.program_id(1)

    @pl.when(t == 0)
    def _():
        s_s[...] = s0_ref[0]

    half = RET_HEAD_DIM // 2
    ri = lax.broadcasted_iota(jnp.int32, (c, c), 0)
    ci = lax.broadcasted_iota(jnp.int32, (c, c), 1)
    diff = (ri - ci).astype(F32)
    it = lax.broadcasted_iota(jnp.int32, (c, 1), 0).astype(F32)
    log_g = [math.log1p(-(2.0 ** (-5.0 - h))) for h in range(RET_HEADS)]
    dmask = [jnp.where(diff >= 0, jnp.exp(lg * jnp.maximum(diff, 0.0)), 0.0) for lg in log_g]
    xi = [jnp.exp(lg * (it + 1.0)) for lg in log_g]
    zeta = [jnp.exp(lg * (c - 1.0 - it)) for lg in log_g]

    units = [(ch, h) for ch in range(nch) for h in range(RET_HEADS)]
    qr, inner, kv = {}, {}, {}
    for ch, h in units:
        rows = slice(ch * c, (ch + 1) * c)
        sl = slice(h * RET_HEAD_DIM, (h + 1) * RET_HEAD_DIM)
        cos2 = cos_ref[rows, :]
        sin2 = sin_ref[rows, :]
        qh = q_ref[0, rows, sl].astype(F32)
        kh = k_ref[0, rows, sl].astype(F32)
        vh = v_ref[0, rows, sl].astype(BF16)
        qr[ch, h] = (qh * cos2 + pltpu.roll(qh, half, 1) * sin2).astype(BF16)
        kr = (kh * cos2 + pltpu.roll(kh, half, 1) * sin2) * (RET_HEAD_DIM ** -0.5)
        scores = _dot_nt(qr[ch, h], kr) * dmask[h]
        inner[ch, h] = _dot(scores, vh)
        kv[ch, h] = _dot_tn(kr * zeta[h], vh)
    for h in range(RET_HEADS):
        sl = slice(h * RET_HEAD_DIM, (h + 1) * RET_HEAD_DIM)
        s = s_s[h]
        outs = []
        for ch in range(nch):
            outs.append(inner[ch, h] + _dot(qr[ch, h], s) * xi[h])
            s = math.exp(log_g[h] * c) * s + kv[ch, h]
        s_s[h] = s
        on = _layer_norm(jnp.concatenate(outs, axis=0), gng_ref[:, sl], gnb_ref[:, sl], LN_EPS)
        gh = g_ref[0, :, sl].astype(F32)
        yb_ref[0, :, sl] = on * (gh * _sigmoid(gh))
    slast_ref[0] = s_s[...]


def _ret_branch(proj, cos2, sin2, s0, gn_g, gn_b, c):
    bsz, t, _ = proj.shape
    w = RET_WIDTH
    dk = RET_HEAD_DIM
    nch = RET_CHUNKS_PER_STEP if t % (RET_CHUNKS_PER_STEP * c) == 0 else 1
    const = lambda bi, ti: (0, 0)
    per_b = lambda bi, ti: (bi, 0, 0, 0)
    col = lambda j: (lambda bi, ti: (bi, ti, j))
    tc = nch * c
    return pl.pallas_call(
        functools.partial(_ret_kernel, c=c, nch=nch),
        grid=(bsz, t // tc),
        in_specs=[pl.BlockSpec((1, tc, w), col(2)), pl.BlockSpec((1, tc, w), col(3)),
                  pl.BlockSpec((1, tc, w), col(4)), pl.BlockSpec((1, tc, w), col(5)),
                  pl.BlockSpec((tc, dk), lambda bi, ti: (ti, 0)), pl.BlockSpec((tc, dk), lambda bi, ti: (ti, 0)),
                  pl.BlockSpec((1, RET_HEADS, dk, dk), per_b),
                  pl.BlockSpec((1, w), const), pl.BlockSpec((1, w), const)],
        out_specs=[pl.BlockSpec((1, tc, w), lambda bi, ti: (bi, ti, 0)),
                   pl.BlockSpec((1, RET_HEADS, dk, dk), per_b)],
        out_shape=[jax.ShapeDtypeStruct((bsz, t, w), F32),
                   jax.ShapeDtypeStruct((bsz, RET_HEADS, dk, dk), F32)],
        scratch_shapes=[pltpu.VMEM((RET_HEADS, dk, dk), F32)],
        compiler_params=_params(("parallel", "arbitrary")),
        name="ret_branch",
    )(proj, proj, proj, proj, cos2, sin2, s0, gn_g, gn_b)


def _rwkv_proj_kernel(x_ref, sh_ref, mu_ref, wrkv_ref, w0_ref, w1_ref, w2_ref, a0_ref, a1_ref, a2_ref,
                      g1_ref, g2_ref, kk_ref, ka_ref, rk_ref, sel_ref, selt_ref,
                      r_o, lw_o, k_o, v_o, kk_o, b_o, g_o, bonus_o, prev_s, *, tm):
    t = pl.program_id(1)

    @pl.when(t == 0)
    def _():
        prev_s[...] = jnp.zeros(prev_s.shape, F32)
        prev_s[V7X_SUBLANES - 1:, :] = sh_ref[0]

    x = x_ref[0]
    ext = jnp.concatenate([prev_s[...], x], axis=0)
    xp = pltpu.roll(ext, 1, 0)[V7X_SUBLANES:, :]
    prev_s[...] = x[tm - V7X_SUBLANES:, :]
    dx = xp - x

    def mix(p):
        return (x + dx * mu_ref[p:p + 1, :]).astype(BF16)

    r = jnp.dot(mix(0), wrkv_ref[0], preferred_element_type=F32)
    k = jnp.dot(mix(1), wrkv_ref[1], preferred_element_type=F32)
    v = jnp.dot(mix(2), wrkv_ref[2], preferred_element_type=F32)
    wl = _dot(jnp.tanh(jnp.dot(mix(3), w1_ref[...], preferred_element_type=F32)), w2_ref[...])
    w = -_softplus(-(w0_ref[...] + wl)) - 0.5
    al = _dot(jnp.dot(mix(4), a1_ref[...], preferred_element_type=F32), a2_ref[...])
    iclr = _sigmoid(a0_ref[...] + al)
    gate = _dot(_sigmoid(jnp.dot(mix(5), g1_ref[...], preferred_element_type=F32)), g2_ref[...])

    kk = k * kk_ref[...]
    sq = jnp.dot((kk * kk).astype(BF16), sel_ref[...], preferred_element_type=F32)
    inv_norm = 1.0 / jnp.maximum(jnp.sqrt(sq), 1e-12)
    kk = kk * _head_bcast(inv_norm, selt_ref)
    k2 = k * (1.0 + (iclr - 1.0) * ka_ref[...])

    r_o[0] = r
    lw_o[0] = -jnp.exp(w)
    k_o[0] = k2
    v_o[0] = v.astype(BF16)
    kk_o[0] = kk
    b_o[0] = kk * iclr
    g_o[0] = gate
    bonus_o[0] = _seg_sum(r * k2 * rk_ref[...], sel_ref, selt_ref) * v


def _rwkv_proj(x, shift, mu, wrkv, w0, w1, w2, a0, a1, a2, g1, g2, k_k, k_a, r_k, sel, selt):
    bsz, t, d = x.shape
    tm = _row_tile(t, RWKV_PROJ_ROW_TILE)
    const2 = lambda bi, ti: (0, 0)
    const3 = lambda bi, ti: (0, 0, 0)
    full2 = lambda a: pl.BlockSpec(a.shape, const2)
    tile = pl.BlockSpec((1, tm, d), lambda bi, ti: (bi, ti, 0))
    out = jax.ShapeDtypeStruct((bsz, t, d), F32)
    return pl.pallas_call(
        functools.partial(_rwkv_proj_kernel, tm=tm),
        grid=(bsz, t // tm),
        in_specs=[tile, pl.BlockSpec((1, 1, d), lambda bi, ti: (bi, 0, 0)), full2(mu),
                  pl.BlockSpec(wrkv.shape, const3), full2(w0), full2(w1), full2(w2), full2(a0), full2(a1),
                  full2(a2), full2(g1), full2(g2), full2(k_k), full2(k_a), full2(r_k), full2(sel), full2(selt)],
        out_specs=[tile] * 8,
        out_shape=[out, out, out, jax.ShapeDtypeStruct((bsz, t, d), BF16), out, out, out, out],
        scratch_shapes=[pltpu.VMEM((V7X_SUBLANES, d), F32)],
        compiler_params=_params(("parallel", "arbitrary")),
        name="rwkv_proj",
    )(x, shift, mu, wrkv, w0, w1, w2, a0, a1, a2, g1, g2, k_k, k_a, r_k, sel, selt)


def _wkv_kernel(r_ref, lw_ref, k_ref, v_ref, kk_ref, b_ref, s0_ref, o_ref, slast_ref, s_s, *, c, nch):
    t = pl.program_id(1)
    n = RWKV_HEAD
    g = WKV_GROUP
    gl = g * n
    gc = g * c
    ngroups = RWKV_HEADS // g

    @pl.when(t == 0)
    def _():
        for h in range(RWKV_HEADS):
            s_s[:, h * n:(h + 1) * n] = s0_ref[0, h]

    iota = lambda shape, dim: lax.broadcasted_iota(jnp.int32, shape, dim)
    head_k = iota((1, gl), 1) // n
    head_j = iota((1, gc), 1) // c
    row_t = iota((c, gc), 0)
    col_j = iota((c, gc), 1) % c
    strict = row_t > col_j
    lower = row_t >= col_j
    eye_c = jnp.where(row_t == col_j, 1.0, 0.0).astype(F32)
    blk = (iota((gl, gl), 0) // n) == (iota((gl, gl), 1) // n)
    eye_l = iota((gl, gl), 0) == iota((gl, gl), 1)
    tri = jnp.where(iota((c, c), 0) >= iota((c, c), 1), 1.0, 0.0).astype(F32)

    def bd(z, lane_head):
        zb = z.astype(BF16)
        return jnp.concatenate([jnp.where(lane_head == h, zb, jnp.zeros_like(zb)) for h in range(g)], axis=0)

    def mm(a, b):
        return jnp.dot(a.astype(BF16), b, preferred_element_type=F32)

    def mm_nt(a, b):
        return lax.dot_general(a.astype(BF16), b, _NT, preferred_element_type=F32)

    units = [(ch, gi) for ch in range(nch) for gi in range(ngroups)]
    u = {}
    for ch in range(nch):
        rows = slice(ch * c, (ch + 1) * c)
        lw = lw_ref[0, rows, :]
        cum = jnp.dot(tri, lw, preferred_element_type=F32, precision=lax.Precision.HIGHEST)
        cum_last = cum[c - 1:c, :]
        e_neg = jnp.exp(-cum)
        e_end = jnp.exp(cum_last - cum)
        kk = kk_ref[0, rows, :]
        b = b_ref[0, rows, :]
        k = k_ref[0, rows, :]
        full = dict(a_t=kk * jnp.exp(cum - lw), b_t=b * e_neg, k_t=k * e_neg, r_t=r_ref[0, rows, :] * jnp.exp(cum),
                    b_e=b * e_end, k_e=k * e_end, d_end=jnp.exp(cum_last), v=v_ref[0, rows, :])
        for gi in range(ngroups):
            lanes = slice(gi * gl, (gi + 1) * gl)
            u[ch, gi] = {name: val[:, lanes] for name, val in full.items()}

    for key in units:
        d = u[key]
        x2 = jnp.concatenate([d['a_t'], d['r_t']], axis=0).astype(BF16)
        mb = mm_nt(x2, bd(d['b_t'], head_k))
        mk = mm_nt(x2, bd(d['k_t'], head_k))
        d['p_rb'] = jnp.where(lower, mb[c:], 0.0).astype(BF16)
        mk_lo = jnp.concatenate([jnp.where(strict, mk[:c], 0.0), jnp.where(lower, mk[c:], 0.0)], axis=0)
        wo = mm(mk_lo, bd(d['v'], head_k))
        d['w_h'] = wo[:c]
        d['o0'] = wo[c:]
        d['pw'] = -jnp.where(strict, mb[:c], 0.0)
        d['tinv'] = eye_c + d['pw']
    nsq = int(math.log2(c)) - 1
    for level in range(nsq):
        for key in units:
            d = u[key]
            pwb = d['pw'].astype(BF16)
            w = bd(pwb, head_j)
            if level == 0:
                d['pw'] = mm(pwb, w)
            else:
                res = mm(jnp.concatenate([pwb, d['tinv'].astype(BF16)], axis=0), w)
                d['pw'] = res[:c]
                d['tinv'] = d['tinv'] + res[c:]
    for key in units:
        d = u[key]
        d['tinv'] = d['tinv'] + mm(d['tinv'], bd(d['pw'], head_j))
    for key in units:
        d = u[key]
        tb = d['tinv'].astype(BF16)
        d['a_hat'] = mm(tb, bd(d['a_t'], head_k))
        d['w_hat'] = mm(tb, bd(d['w_h'], head_k))
    for key in units:
        d = u[key]
        d['r_hat'] = d['r_t'] - mm(d['p_rb'], bd(d['a_hat'], head_k))
        d['o0'] = d['o0'] - mm(d['p_rb'], bd(d['w_hat'], head_k))
        b_e = d['b_e'].astype(BF16)
        gfull = lax.dot_general(d['a_hat'].astype(BF16), b_e, _TN, preferred_element_type=F32)
        d['gmat'] = (jnp.where(eye_l, d['d_end'], 0.0) - jnp.where(blk, gfull, 0.0)).astype(BF16)
        vw = jnp.concatenate([d['v'].astype(BF16), d['w_hat'].astype(BF16)], axis=0)
        kb = jnp.concatenate([d['k_e'].astype(BF16), -b_e], axis=0)
        zf = lax.dot_general(vw, kb, _TN, preferred_element_type=F32)
        hmat = jnp.where(head_k == 0, zf[:n], 0.0)
        for h in range(1, g):
            hmat = hmat + jnp.where(head_k == h, zf[h * n:(h + 1) * n], 0.0)
        d['hmat'] = hmat

    for gi in range(ngroups):
        lanes = slice(gi * gl, (gi + 1) * gl)
        s = s_s[:, lanes]
        for ch in range(nch):
            d = u[ch, gi]
            o_ref[0, ch * c:(ch + 1) * c, lanes] = d['o0'] + mm_nt(d['r_hat'], bd(s, head_k))
            s = mm(s, d['gmat']) + d['hmat']
        s_s[:, lanes] = s

    @pl.when(t == pl.num_programs(1) - 1)
    def _():
        for h in range(RWKV_HEADS):
            slast_ref[0, h] = s_s[:, h * n:(h + 1) * n]


def _wkv_scan(r, lw, k, v, kk, b, s0, c):
    bsz, t, d = r.shape
    n = RWKV_HEAD
    nch = WKV_CHUNKS_PER_STEP if t % (WKV_CHUNKS_PER_STEP * c) == 0 else 1
    tile = pl.BlockSpec((1, nch * c, d), lambda bi, ti: (bi, ti, 0))
    st = pl.BlockSpec((1, RWKV_HEADS, n, n), lambda bi, ti: (bi, 0, 0, 0))
    return pl.pallas_call(
        functools.partial(_wkv_kernel, c=c, nch=nch),
        grid=(bsz, t // (nch * c)),
        in_specs=[tile] * 6 + [st],
        out_specs=[tile, st],
        out_shape=[jax.ShapeDtypeStruct((bsz, t, d), F32), jax.ShapeDtypeStruct(s0.shape, F32)],
        scratch_shapes=[pltpu.VMEM((n, d), F32)],
        compiler_params=_params(("parallel", "arbitrary")),
        name="wkv_scan",
    )(r, lw, k, v, kk, b, s0)


def _wkv_out_kernel(x_ref, o_ref, bonus_ref, gate_ref, gng_ref, gnb_ref, sel_ref, selt_ref, w_ref,
                    g_ref, b_ref, y_ref):
    o = o_ref[...]
    inv_n = 1.0 / RWKV_HEAD
    mu = _seg_sum(o, sel_ref, selt_ref) * inv_n
    oc = o - mu
    var = _seg_sum(oc * oc, sel_ref, selt_ref) * inv_n
    on = oc * lax.rsqrt(var + RWKV_GN_EPS) * gng_ref[...] + gnb_ref[...]
    z = (on + bonus_ref[...]) * gate_ref[...]
    y = jnp.dot(z.astype(BF16), w_ref[...], preferred_element_type=F32)
    y_ref[...] = _layer_norm(ALPHA * x_ref[...] + y, g_ref[...], b_ref[...], LN_EPS)


def _wkv_out_postnorm(x, o, bonus, gate, gn_g, gn_b, sel, selt, w, g, b):
    n, d = x.shape
    tm = _row_tile(n)
    row = pl.BlockSpec((tm, d), lambda i: (i, 0))
    const = lambda i: (0, 0)
    full = lambda a: pl.BlockSpec(a.shape, const)
    return pl.pallas_call(
        _wkv_out_kernel,
        grid=(n // tm,),
        in_specs=[row, row, row, row, full(gn_g), full(gn_b), full(sel), full(selt), full(w), full(g), full(b)],
        out_specs=row,
        out_shape=jax.ShapeDtypeStruct((n, d), F32),
        compiler_params=_params(("parallel",)),
        name="wkv_out_postnorm",
    )(x, o, bonus, gate, gn_g, gn_b, sel, selt, w, g, b)


def _prep_weights(ln_g, ln_b, ffn_up, ffn_down, xa_q, xa_o, l0_w_in, l0_conv_w, l0_conv_b, l0_lru_wa,
                  l0_lru_ba, l0_lru_wx, l0_lru_bx, l0_lru_lambda, l0_ret_gn_g, l0_ret_gn_b, l0_w_out,
                  l1_mu, l1_w_rkv, l1_w0, l1_w1, l1_w2, l1_a0, l1_a1, l1_a2, l1_g1, l1_g2, l1_k_k, l1_k_a,
                  l1_r_k, l1_gn_g, l1_gn_b, l1_w_out):
    d = D_MODEL
    row = lambda a: a.reshape(1, -1).astype(F32)

    def block_diag(w):
        eye = jnp.eye(LRU_BLOCKS, dtype=w.dtype)
        return jnp.einsum('gij,gh->gihj', w, eye).reshape(LRU_WIDTH, LRU_WIDTH).astype(BF16)

    head_of_col = jnp.arange(d) // RWKV_HEAD
    sel = (head_of_col[:, None] == jnp.arange(SEL_WIDTH)[None, :]).astype(BF16)
    sel_row = jnp.arange(SEL_WIDTH)
    selt = ((sel_row[:, None] % RWKV_HEADS == head_of_col[None, :])
            & (sel_row[:, None] < 2 * RWKV_HEADS)).astype(BF16)
    return dict(
        ln_g=ln_g, ln_b=ln_b, ffn_up=ffn_up.astype(BF16), ffn_down=ffn_down.astype(BF16),
        xa_q=(xa_q * (MEM_HEAD_DIM ** -0.5)).astype(BF16), xa_o=xa_o.astype(BF16),
        w_in=l0_w_in.astype(BF16), conv_w=l0_conv_w, conv_b=row(l0_conv_b),
        lru_wa=block_diag(l0_lru_wa), lru_ba=row(l0_lru_ba), lru_wx=block_diag(l0_lru_wx),
        lru_bx=row(l0_lru_bx), lru_lam=row(l0_lru_lambda),
        ret_gn_g=row(l0_ret_gn_g), ret_gn_b=row(l0_ret_gn_b),
        w_out_a=l0_w_out[:LRU_WIDTH].astype(BF16), w_out_b=l0_w_out[LRU_WIDTH:].astype(BF16),
        mu=l1_mu, w_rkv=l1_w_rkv.astype(BF16), w0=row(l1_w0), w1=l1_w1.astype(BF16), w2=l1_w2.astype(BF16),
        a0=row(l1_a0), a1=l1_a1.astype(BF16), a2=l1_a2.astype(BF16), g1=l1_g1.astype(BF16),
        g2=l1_g2.astype(BF16), k_k=row(l1_k_k), k_a=row(l1_k_a), r_k=row(l1_r_k),
        gn_g=row(l1_gn_g), gn_b=row(l1_gn_b), w_out_c=l1_w_out.astype(BF16), sel=sel, selt=selt,
    )


def _rotary_tables(pos):
    half = RET_HEAD_DIM // 2
    inv_freq = ROPE_BASE ** (-jnp.arange(half, dtype=F32) / half)
    ang = pos.astype(F32)[:, None] * inv_freq[None, :]
    cos = jnp.cos(ang)
    sin = jnp.sin(ang)
    return jnp.concatenate([cos, cos], axis=-1), jnp.concatenate([-sin, sin], axis=-1)


def _run_trunk(x, pos, mem_k, mem_v, states, p):
    bsz, t, d = x.shape
    n = bsz * t
    (conv_buf, h0, s_ret), (shift, s_wkv) = states
    lng = lambda l, j: p['ln_g'][l, j].reshape(1, d)
    lnb = lambda l, j: p['ln_b'][l, j].reshape(1, d)
    flat = lambda a: a.reshape(n, a.shape[-1])
    chunk = min(CHUNK, t)
    ffn = lambda xx, l, j, nj: _ffn_postnorm(xx, p['ffn_up'], p['ffn_down'], l, j, lng(l, nj), lnb(l, nj))

    x = ffn(flat(x), 0, 0, 0)
    proj = _matmul(x, p['w_in'], BF16).reshape(bsz, t, -1)
    ya, new_buf, h_last = _lru_branch(proj, conv_buf, h0.reshape(bsz, 1, LRU_WIDTH), p['conv_w'], p['conv_b'],
                                      p['lru_wa'], p['lru_ba'], p['lru_wx'], p['lru_bx'], p['lru_lam'])
    cos2, sin2 = _rotary_tables(pos)
    yb, s_ret_new = _ret_branch(proj, cos2, sin2, s_ret, p['ret_gn_g'], p['ret_gn_b'], chunk)
    x = _ab_out_postnorm(x, flat(ya), flat(yb), p['w_out_a'], p['w_out_b'], lng(0, 1), lnb(0, 1))
    x = _xattn_postnorm(x.reshape(bsz, t, d), mem_k, mem_v, p['xa_q'], p['xa_o'], 0, lng(0, 2), lnb(0, 2))
    x = ffn(flat(x), 0, 1, 3)

    x = ffn(x, 1, 0, 0)
    x3 = x.reshape(bsz, t, d)
    r, lw, k2, v, kk, b, gate, bonus = _rwkv_proj(
        x3, shift, p['mu'], p['w_rkv'], p['w0'], p['w1'], p['w2'], p['a0'], p['a1'], p['a2'], p['g1'], p['g2'],
        p['k_k'], p['k_a'], p['r_k'], p['sel'], p['selt'])
    o, s_wkv_new = _wkv_scan(r, lw, k2, v, kk, b, s_wkv, chunk)
    new_shift = x3[:, t - 1:, :]
    x = _wkv_out_postnorm(x, flat(o), flat(bonus), flat(gate), p['gn_g'], p['gn_b'], p['sel'], p['selt'],
                          p['w_out_c'], lng(1, 1), lnb(1, 1))
    x = _xattn_postnorm(x.reshape(bsz, t, d), mem_k, mem_v, p['xa_q'], p['xa_o'], 1, lng(1, 2), lnb(1, 2))
    x = ffn(flat(x), 1, 1, 3)
    new_states = ((new_buf, h_last.reshape(bsz, LRU_WIDTH), s_ret_new), (new_shift, s_wkv_new))
    return x.reshape(bsz, t, d), new_states


def kernel(x_prompt, x_sample, mem_prompt, state_conv0, state_lru0, state_ret0, state_shift1, state_wkv1,
           cache_mem_k, cache_mem_v, ln_g, ln_b, ffn_up, ffn_down, xa_q, xa_k, xa_v, xa_o,
           l0_w_in, l0_conv_w, l0_conv_b, l0_lru_wa, l0_lru_ba, l0_lru_wx, l0_lru_bx, l0_lru_lambda,
           l0_ret_gn_g, l0_ret_gn_b, l0_w_out, l1_mu, l1_w_rkv, l1_w0, l1_w1, l1_w2, l1_a0, l1_a1, l1_a2,
           l1_g1, l1_g2, l1_k_k, l1_k_a, l1_r_k, l1_gn_g, l1_gn_b, l1_w_out):
    d = D_MODEL
    p = _prep_weights(ln_g, ln_b, ffn_up, ffn_down, xa_q, xa_o, l0_w_in, l0_conv_w, l0_conv_b, l0_lru_wa,
                      l0_lru_ba, l0_lru_wx, l0_lru_bx, l0_lru_lambda, l0_ret_gn_g, l0_ret_gn_b, l0_w_out,
                      l1_mu, l1_w_rkv, l1_w0, l1_w1, l1_w2, l1_a0, l1_a1, l1_a2, l1_g1, l1_g2, l1_k_k, l1_k_a,
                      l1_r_k, l1_gn_g, l1_gn_b, l1_w_out)

    bp, tp, _ = x_prompt.shape
    dt = x_prompt.dtype
    mem_k_p = _mem_proj(mem_prompt, xa_k.astype(BF16))
    mem_v_p = _mem_proj(mem_prompt, xa_v.astype(BF16))
    zero_states = ((jnp.zeros((bp, CONV_WIDTH - 1, LRU_WIDTH), dt), jnp.zeros((bp, LRU_WIDTH), dt),
                    jnp.zeros((bp, RET_HEADS, RET_HEAD_DIM, RET_HEAD_DIM), dt)),
                   (jnp.zeros((bp, 1, d), dt), jnp.zeros((bp, RWKV_HEADS, RWKV_HEAD, RWKV_HEAD), dt)))
    y_prompt, st_p = _run_trunk(x_prompt, jnp.arange(tp, dtype=jnp.int32), mem_k_p, mem_v_p, zero_states, p)

    bs, ts, _ = x_sample.shape
    pos_s = PAST_LEN + jnp.arange(ts, dtype=jnp.int32)
    sample_states = ((state_conv0, state_lru0, state_ret0), (state_shift1, state_wkv1))
    y_sample, st_s = _run_trunk(x_sample, pos_s, cache_mem_k, cache_mem_v, sample_states, p)

    (p_conv0, p_lru0, p_ret0), (p_shift1, p_wkv1) = st_p
    (s_conv0, s_lru0, s_ret0), (s_shift1, s_wkv1) = st_s
    return (y_prompt, y_sample, mem_k_p, mem_v_p,
            p_conv0, p_lru0, p_ret0, p_shift1, p_wkv1, s_conv0, s_lru0, s_ret0, s_shift1, s_wkv1)
```

```python
import functools
import math

import jax
import jax.numpy as jnp
from jax import lax
from jax.experimental import pallas as pl
from jax.experimental.pallas import tpu as pltpu

D_MODEL = 1024
DEPTH = 2
PAST_LEN = 4096
CHUNK = 64
N_MEM = 256
MEM_HEADS = 4
MEM_HEAD_DIM = D_MODEL // MEM_HEADS
D_FF = 2816
LRU_WIDTH = 512
LRU_BLOCKS = 8
LRU_BLOCK = LRU_WIDTH // LRU_BLOCKS
CONV_WIDTH = 4
LRU_C = 8.0
RET_HEADS = 4
RET_WIDTH = 512
RET_HEAD_DIM = RET_WIDTH // RET_HEADS
ROPE_BASE = 10000.0
RWKV_HEAD = 64
RWKV_HEADS = D_MODEL // RWKV_HEAD
LN_EPS = 1e-5
RWKV_GN_EPS = 64e-5
ALPHA = (2 * DEPTH) ** 0.25

F32 = jnp.float32
BF16 = jnp.bfloat16

V7X_SUBLANES = 8
V7X_LANES = 128
VMEM_LIMIT_BYTES = 56 * 1024 * 1024
FFN_CHUNK = 256
ROW_TILE = 1024
FFN_ROW_TILE = 1024
RWKV_PROJ_ROW_TILE = 512
SEL_WIDTH = V7X_LANES
WKV_GROUP = 4
WKV_CHUNKS_PER_STEP = 4
RET_CHUNKS_PER_STEP = 8

_NT = (((1,), (1,)), ((), ()))
_TN = (((0,), (0,)), ((), ()))


def _dot(a, b):
    return jnp.dot(a.astype(BF16), b.astype(BF16), preferred_element_type=F32)


def _dot_nt(a, b):
    return lax.dot_general(a.astype(BF16), b.astype(BF16), _NT, preferred_element_type=F32)


def _dot_tn(a, b):
    return lax.dot_general(a.astype(BF16), b.astype(BF16), _TN, preferred_element_type=F32)


def _head_bcast(vals, selt_ref):
    lane = lax.broadcasted_iota(jnp.int32, vals.shape, 1)
    vals = jnp.where(lane < RWKV_HEADS, vals, 0.0)
    hi = vals.astype(BF16).astype(F32)
    packed = hi + pltpu.roll(vals - hi, RWKV_HEADS, 1)
    return jnp.dot(packed.astype(BF16), selt_ref[...], preferred_element_type=F32)


def _seg_sum(z, sel_ref, selt_ref):
    sums = jnp.dot(z.astype(BF16), sel_ref[...], preferred_element_type=F32)
    return _head_bcast(sums, selt_ref)


def _cumsum_rows(x):
    rows = x.shape[0]
    row8 = lax.broadcasted_iota(jnp.int32, x.shape, 0) % V7X_SUBLANES
    s = 1
    while s < V7X_SUBLANES:
        x = jnp.where(row8 >= s, x + pltpu.roll(x, s, 0), x)
        s *= 2
    out = [x[:V7X_SUBLANES]]
    for j in range(1, rows // V7X_SUBLANES):
        out.append(x[j * V7X_SUBLANES:(j + 1) * V7X_SUBLANES] + out[-1][V7X_SUBLANES - 1:])
    return jnp.concatenate(out, axis=0)


def _layer_norm(y, g, b, eps):
    mu = jnp.mean(y, axis=-1, keepdims=True)
    yc = y - mu
    var = jnp.mean(yc * yc, axis=-1, keepdims=True)
    return yc * lax.rsqrt(var + eps) * g + b


def _sigmoid(x):
    return 1.0 / (1.0 + jnp.exp(-x))


def _softplus(x):
    return jnp.maximum(x, 0.0) + jnp.log1p(jnp.exp(-jnp.abs(x)))


def _gelu_tanh(x):
    return 0.5 * x * (1.0 + jnp.tanh(math.sqrt(2.0 / math.pi) * (x + 0.044715 * (x * x * x))))


def _params(sem):
    return pltpu.CompilerParams(dimension_semantics=sem, vmem_limit_bytes=VMEM_LIMIT_BYTES)


def _row_tile(n, pref=ROW_TILE):
    return pref if n % pref == 0 else n


def _ffn_kernel(x_ref, wup_ref, wdn_ref, g_ref, b_ref, o_ref):
    x = x_ref[...]
    xb = x.astype(BF16)
    acc = jnp.zeros(x.shape, F32)
    for lo in range(0, D_FF, FFN_CHUNK):
        hg = jnp.dot(xb, wup_ref[:, lo:lo + FFN_CHUNK], preferred_element_type=F32)
        hu = jnp.dot(xb, wup_ref[:, D_FF + lo:D_FF + lo + FFN_CHUNK], preferred_element_type=F32)
        h = hg * _sigmoid(hg) * hu
        acc = acc + jnp.dot(h.astype(BF16), wdn_ref[lo:lo + FFN_CHUNK, :], preferred_element_type=F32)
    o_ref[...] = _layer_norm(ALPHA * x + 0.5 * acc, g_ref[...], b_ref[...], LN_EPS)


def _ffn_postnorm(x, w_up, w_down, layer, which, g, b):
    n, d = x.shape
    tm = _row_tile(n, FFN_ROW_TILE)
    pick = lambda i: (layer, which, 0, 0)
    return pl.pallas_call(
        _ffn_kernel,
        grid=(n // tm,),
        in_specs=[
            pl.BlockSpec((tm, d), lambda i: (i, 0)),
            pl.BlockSpec((None, None, d, 2 * D_FF), pick, pipeline_mode=pl.Buffered(1)),
            pl.BlockSpec((None, None, D_FF, d), pick, pipeline_mode=pl.Buffered(1)),
            pl.BlockSpec((1, d), lambda i: (0, 0)),
            pl.BlockSpec((1, d), lambda i: (0, 0)),
        ],
        out_specs=pl.BlockSpec((tm, d), lambda i: (i, 0)),
        out_shape=jax.ShapeDtypeStruct((n, d), F32),
        compiler_params=_params(("parallel",)),
        name="ffn_postnorm",
    )(x, w_up, w_down, g, b)


def _mm_kernel(x_ref, w_ref, o_ref):
    o_ref[...] = jnp.dot(x_ref[...].astype(BF16), w_ref[...], preferred_element_type=F32).astype(o_ref.dtype)


def _matmul(x, w, out_dtype):
    n, k = x.shape
    m = w.shape[1]
    tm = _row_tile(n)
    return pl.pallas_call(
        _mm_kernel,
        grid=(n // tm,),
        in_specs=[pl.BlockSpec((tm, k), lambda i: (i, 0)), pl.BlockSpec((k, m), lambda i: (0, 0))],
        out_specs=pl.BlockSpec((tm, m), lambda i: (i, 0)),
        out_shape=jax.ShapeDtypeStruct((n, m), out_dtype),
        compiler_params=_params(("parallel",)),
        name="proj",
    )(x, w)


def _mem_proj_kernel(x_ref, w_ref, o_ref):
    y = jnp.dot(x_ref[...].astype(BF16), w_ref[...], preferred_element_type=F32)
    for h in range(MEM_HEADS):
        o_ref[:, h, :] = y[:, h * MEM_HEAD_DIM:(h + 1) * MEM_HEAD_DIM]


def _mem_proj(mem, w):
    bsz, m, d = mem.shape
    depth = w.shape[0]
    return pl.pallas_call(
        _mem_proj_kernel,
        grid=(depth, bsz),
        in_specs=[pl.BlockSpec((None, m, d), lambda l, bi: (bi, 0, 0)),
                  pl.BlockSpec((None, d, d), lambda l, bi: (l, 0, 0))],
        out_specs=pl.BlockSpec((None, None, m, MEM_HEADS, MEM_HEAD_DIM), lambda l, bi: (l, bi, 0, 0, 0)),
        out_shape=jax.ShapeDtypeStruct((depth, bsz, m, MEM_HEADS, MEM_HEAD_DIM), F32),
        compiler_params=_params(("parallel", "parallel")),
        name="mem_proj",
    )(mem, w)


def _ab_out_kernel(x_ref, ya_ref, yb_ref, wa_ref, wb_ref, g_ref, b_ref, o_ref):
    y = (jnp.dot(ya_ref[...].astype(BF16), wa_ref[...], preferred_element_type=F32)
         + jnp.dot(yb_ref[...].astype(BF16), wb_ref[...], preferred_element_type=F32))
    o_ref[...] = _layer_norm(ALPHA * x_ref[...] + y, g_ref[...], b_ref[...], LN_EPS)


def _ab_out_postnorm(x, ya, yb, wa, wb, g, b):
    n, d = x.shape
    ka = ya.shape[1]
    kb = yb.shape[1]
    tm = _row_tile(n)
    row = lambda i: (i, 0)
    const = lambda i: (0, 0)
    return pl.pallas_call(
        _ab_out_kernel,
        grid=(n // tm,),
        in_specs=[pl.BlockSpec((tm, d), row), pl.BlockSpec((tm, ka), row), pl.BlockSpec((tm, kb), row),
                  pl.BlockSpec((ka, d), const), pl.BlockSpec((kb, d), const),
                  pl.BlockSpec((1, d), const), pl.BlockSpec((1, d), const)],
        out_specs=pl.BlockSpec((tm, d), row),
        out_shape=jax.ShapeDtypeStruct((n, d), F32),
        compiler_params=_params(("parallel",)),
        name="ab_out_postnorm",
    )(x, ya, yb, wa, wb, g, b)


def _xattn_kernel(x_ref, k_ref, v_ref, wq_ref, wo_ref, g_ref, b_ref, o_ref, kb_s, vb_s):
    @pl.when(pl.program_id(1) == 0)
    def _():
        for h in range(MEM_HEADS):
            kb_s[h] = k_ref[:, h, :].astype(BF16)
            vb_s[h] = v_ref[:, h, :].astype(BF16)

    x = x_ref[0]
    q = jnp.dot(x.astype(BF16), wq_ref[...], preferred_element_type=F32)
    qb = q.astype(BF16)
    outs = []
    for h in range(MEM_HEADS):
        sl = slice(h * MEM_HEAD_DIM, (h + 1) * MEM_HEAD_DIM)
        s = lax.dot_general(qb[:, sl], kb_s[h], _NT, preferred_element_type=F32)
        m = jnp.max(s, axis=-1, keepdims=True)
        p = jnp.exp(s - m)
        p = p * (1.0 / jnp.sum(p, axis=-1, keepdims=True))
        outs.append(jnp.dot(p.astype(BF16), vb_s[h], preferred_element_type=F32))
    o = jnp.concatenate(outs, axis=-1)
    y = jnp.dot(o.astype(BF16), wo_ref[...], preferred_element_type=F32)
    o_ref[0] = _layer_norm(ALPHA * x + y, g_ref[...], b_ref[...], LN_EPS)


def _xattn_postnorm(x, mem_k, mem_v, wq, wo, layer, g, b):
    bsz, t, d = x.shape
    tm = _row_tile(t)
    const = lambda bi, ti: (0, 0)
    mem_spec = pl.BlockSpec((None, None, N_MEM, MEM_HEADS, MEM_HEAD_DIM), lambda bi, ti: (layer, bi, 0, 0, 0))
    w_spec = pl.BlockSpec((None, d, d), lambda bi, ti: (layer, 0, 0))
    return pl.pallas_call(
        _xattn_kernel,
        grid=(bsz, t // tm),
        in_specs=[pl.BlockSpec((1, tm, d), lambda bi, ti: (bi, ti, 0)),
                  mem_spec, mem_spec, w_spec, w_spec,
                  pl.BlockSpec((1, d), const), pl.BlockSpec((1, d), const)],
        out_specs=pl.BlockSpec((1, tm, d), lambda bi, ti: (bi, ti, 0)),
        out_shape=jax.ShapeDtypeStruct((bsz, t, d), F32),
        scratch_shapes=[pltpu.VMEM((MEM_HEADS, N_MEM, MEM_HEAD_DIM), BF16)] * 2,
        compiler_params=_params(("parallel", "arbitrary")),
        name="xattn_postnorm",
    )(x, mem_k, mem_v, wq, wo, g, b)


def _lru_kernel(xa_ref, ga_ref, cbuf_ref, h0_ref, cw_ref, cb_ref, wa_ref, ba_ref, wx_ref, bx_ref, lam_ref,
                ya_ref, nbuf_ref, hlast_ref, prev_s, h_s, a_s, u_s, *, tc):
    t = pl.program_id(1)

    @pl.when(t == 0)
    def _():
        prev_s[...] = jnp.zeros(prev_s.shape, F32)
        prev_s[V7X_SUBLANES - (CONV_WIDTH - 1):, :] = cbuf_ref[0]
        h_s[...] = h0_ref[0]

    xa = xa_ref[0].astype(F32)
    ext = jnp.concatenate([prev_s[...], xa], axis=0)
    xc = cb_ref[...] + xa * cw_ref[CONV_WIDTH - 1:CONV_WIDTH, :]
    for s in range(1, CONV_WIDTH):
        shifted = pltpu.roll(ext, s, 0)[V7X_SUBLANES:, :]
        xc = xc + shifted * cw_ref[CONV_WIDTH - 1 - s:CONV_WIDTH - s, :]
    prev_s[...] = xa[tc - V7X_SUBLANES:, :]
    nbuf_ref[0] = xa[tc - (CONV_WIDTH - 1):, :]

    xcb = xc.astype(BF16)
    r = _sigmoid(jnp.dot(xcb, wa_ref[...], preferred_element_type=F32) + ba_ref[...])
    i = _sigmoid(jnp.dot(xcb, wx_ref[...], preferred_element_type=F32) + bx_ref[...])
    log_a = (-LRU_C * _softplus(-lam_ref[...])) * r
    a = jnp.exp(log_a)
    th = jnp.tanh(log_a)
    u = jnp.sqrt(-2.0 * th / (1.0 - th)) * (i * xc)

    row8 = lax.broadcasted_iota(jnp.int32, a.shape, 0) % V7X_SUBLANES
    s = 1
    while s < V7X_SUBLANES:
        inside = row8 >= s
        u = jnp.where(inside, u + a * pltpu.roll(u, s, 0), u)
        a = jnp.where(inside, a * pltpu.roll(a, s, 0), a)
        s *= 2
    a_s[...] = a
    u_s[...] = u

    def body(j, h):
        base = pl.multiple_of(j * V7X_SUBLANES, V7X_SUBLANES)
        hb = u_s[pl.ds(base, V7X_SUBLANES), :] + a_s[pl.ds(base, V7X_SUBLANES), :] * h
        u_s[pl.ds(base, V7X_SUBLANES), :] = hb
        return hb[V7X_SUBLANES - 1:, :]

    h = lax.fori_loop(0, tc // V7X_SUBLANES, body, h_s[...], unroll=4)
    h_s[...] = h
    hlast_ref[0] = h
    ya_ref[0] = u_s[...] * _gelu_tanh(ga_ref[0].astype(F32))


def _lru_branch(proj, conv_buf, h0, cw, cb, wa, ba, wx, bx, lam):
    bsz, t, _ = proj.shape
    w = LRU_WIDTH
    tc = _row_tile(t)
    const = lambda bi, ti: (0, 0)
    per_b = lambda bi, ti: (bi, 0, 0)
    return pl.pallas_call(
        functools.partial(_lru_kernel, tc=tc),
        grid=(bsz, t // tc),
        in_specs=[pl.BlockSpec((1, tc, w), lambda bi, ti: (bi, ti, 0)),
                  pl.BlockSpec((1, tc, w), lambda bi, ti: (bi, ti, 1)),
                  pl.BlockSpec((1, CONV_WIDTH - 1, w), per_b),
                  pl.BlockSpec((1, 1, w), per_b),
                  pl.BlockSpec((CONV_WIDTH, w), const), pl.BlockSpec((1, w), const),
                  pl.BlockSpec((w, w), const), pl.BlockSpec((1, w), const),
                  pl.BlockSpec((w, w), const), pl.BlockSpec((1, w), const),
                  pl.BlockSpec((1, w), const)],
        out_specs=[pl.BlockSpec((1, tc, w), lambda bi, ti: (bi, ti, 0)),
                   pl.BlockSpec((1, CONV_WIDTH - 1, w), per_b),
                   pl.BlockSpec((1, 1, w), per_b)],
        out_shape=[jax.ShapeDtypeStruct((bsz, t, w), F32),
                   jax.ShapeDtypeStruct((bsz, CONV_WIDTH - 1, w), F32),
                   jax.ShapeDtypeStruct((bsz, 1, w), F32)],
        scratch_shapes=[pltpu.VMEM((V7X_SUBLANES, w), F32), pltpu.VMEM((1, w), F32),
                        pltpu.VMEM((tc, w), F32), pltpu.VMEM((tc, w), F32)],
        compiler_params=_params(("parallel", "arbitrary")),
        name="lru_branch",
    )(proj, proj, conv_buf, h0, cw, cb, wa, ba, wx, bx, lam)


def _ret_kernel(q_ref, k_ref, v_ref, g_ref, cos_ref, sin_ref, s0_ref, gng_ref, gnb_ref,
                yb_ref, slast_ref, s_s, *, c, nch):
    t = pl.program_id(1)

    @pl.when(t == 0)
    def _():
        s_s[...] = s0_ref[0]

    half = RET_HEAD_DIM // 2
    ri = lax.broadcasted_iota(jnp.int32, (c, c), 0)
    ci = lax.broadcasted_iota(jnp.int32, (c, c), 1)
    diff = (ri - ci).astype(F32)
    it = lax.broadcasted_iota(jnp.int32, (c, 1), 0).astype(F32)
    log_g = [math.log1p(-(2.0 ** (-5.0 - h))) for h in range(RET_HEADS)]
    dmask = [jnp.where(diff >= 0, jnp.exp(lg * jnp.maximum(diff, 0.0)), 0.0) for lg in log_g]
    xi = [jnp.exp(lg * (it + 1.0)) for lg in log_g]
    zeta = [jnp.exp(lg * (c - 1.0 - it)) for lg in log_g]

    units = [(ch, h) for ch in range(nch) for h in range(RET_HEADS)]
    qr, inner, kv = {}, {}, {}
    for ch, h in units:
        rows = slice(ch * c, (ch + 1) * c)
        sl = slice(h * RET_HEAD_DIM, (h + 1) * RET_HEAD_DIM)
        cos2 = cos_ref[rows, :]
        sin2 = sin_ref[rows, :]
        qh = q_ref[0, rows, sl].astype(F32)
        kh = k_ref[0, rows, sl].astype(F32)
        vh = v_ref[0, rows, sl].astype(BF16)
        qr[ch, h] = (qh * cos2 + pltpu.roll(qh, half, 1) * sin2).astype(BF16)
        kr = (kh * cos2 + pltpu.roll(kh, half, 1) * sin2) * (RET_HEAD_DIM ** -0.5)
        scores = _dot_nt(qr[ch, h], kr) * dmask[h]
        inner[ch, h] = _dot(scores, vh)
        kv[ch, h] = _dot_tn(kr * zeta[h], vh)
    for h in range(RET_HEADS):
        sl = slice(h * RET_HEAD_DIM, (h + 1) * RET_HEAD_DIM)
        s = s_s[h]
        outs = []
        for ch in range(nch):
            outs.append(inner[ch, h] + _dot(qr[ch, h], s) * xi[h])
            s = math.exp(log_g[h] * c) * s + kv[ch, h]
        s_s[h] = s
        on = _layer_norm(jnp.concatenate(outs, axis=0), gng_ref[:, sl], gnb_ref[:, sl], LN_EPS)
        gh = g_ref[0, :, sl].astype(F32)
        yb_ref[0, :, sl] = on * (gh * _sigmoid(gh))
    slast_ref[0] = s_s[...]


def _ret_branch(proj, cos2, sin2, s0, gn_g, gn_b, c):
    bsz, t, _ = proj.shape
    w = RET_WIDTH
    dk = RET_HEAD_DIM
    nch = RET_CHUNKS_PER_STEP if t % (RET_CHUNKS_PER_STEP * c) == 0 else 1
    const = lambda bi, ti: (0, 0)
    per_b = lambda bi, ti: (bi, 0, 0, 0)
    col = lambda j: (lambda bi, ti: (bi, ti, j))
    tc = nch * c
    return pl.pallas_call(
        functools.partial(_ret_kernel, c=c, nch=nch),
        grid=(bsz, t // tc),
        in_specs=[pl.BlockSpec((1, tc, w), col(2)), pl.BlockSpec((1, tc, w), col(3)),
                  pl.BlockSpec((1, tc, w), col(4)), pl.BlockSpec((1, tc, w), col(5)),
                  pl.BlockSpec((tc, dk), lambda bi, ti: (ti, 0)), pl.BlockSpec((tc, dk), lambda bi, ti: (ti, 0)),
                  pl.BlockSpec((1, RET_HEADS, dk, dk), per_b),
                  pl.BlockSpec((1, w), const), pl.BlockSpec((1, w), const)],
        out_specs=[pl.BlockSpec((1, tc, w), lambda bi, ti: (bi, ti, 0)),
                   pl.BlockSpec((1, RET_HEADS, dk, dk), per_b)],
        out_shape=[jax.ShapeDtypeStruct((bsz, t, w), F32),
                   jax.ShapeDtypeStruct((bsz, RET_HEADS, dk, dk), F32)],
        scratch_shapes=[pltpu.VMEM((RET_HEADS, dk, dk), F32)],
        compiler_params=_params(("parallel", "arbitrary")),
        name="ret_branch",
    )(proj, proj, proj, proj, cos2, sin2, s0, gn_g, gn_b)


def _rwkv_proj_kernel(x_ref, sh_ref, mu_ref, wrkv_ref, w0_ref, w1_ref, w2_ref, a0_ref, a1_ref, a2_ref,
                      g1_ref, g2_ref, kk_ref, ka_ref, rk_ref, sel_ref, selt_ref,
                      r_o, lw_o, k_o, v_o, kk_o, b_o, g_o, bonus_o, prev_s, *, tm):
    t = pl.program_id(1)

    @pl.when(t == 0)
    def _():
        prev_s[...] = jnp.zeros(prev_s.shape, F32)
        prev_s[V7X_SUBLANES - 1:, :] = sh_ref[0]

    x = x_ref[0]
    ext = jnp.concatenate([prev_s[...], x], axis=0)
    xp = pltpu.roll(ext, 1, 0)[V7X_SUBLANES:, :]
    prev_s[...] = x[tm - V7X_SUBLANES:, :]
    dx = xp - x

    def mix(p):
        return (x + dx * mu_ref[p:p + 1, :]).astype(BF16)

    r = jnp.dot(mix(0), wrkv_ref[0], preferred_element_type=F32)
    k = jnp.dot(mix(1), wrkv_ref[1], preferred_element_type=F32)
    v = jnp.dot(mix(2), wrkv_ref[2], preferred_element_type=F32)
    wl = _dot(jnp.tanh(jnp.dot(mix(3), w1_ref[...], preferred_element_type=F32)), w2_ref[...])
    w = -_softplus(-(w0_ref[...] + wl)) - 0.5
    al = _dot(jnp.dot(mix(4), a1_ref[...], preferred_element_type=F32), a2_ref[...])
    iclr = _sigmoid(a0_ref[...] + al)
    gate = _dot(_sigmoid(jnp.dot(mix(5), g1_ref[...], preferred_element_type=F32)), g2_ref[...])

    kk = k * kk_ref[...]
    sq = jnp.dot((kk * kk).astype(BF16), sel_ref[...], preferred_element_type=F32)
    inv_norm = 1.0 / jnp.maximum(jnp.sqrt(sq), 1e-12)
    kk = kk * _head_bcast(inv_norm, selt_ref)
    k2 = k * (1.0 + (iclr - 1.0) * ka_ref[...])

    r_o[0] = r
    lw_o[0] = -jnp.exp(w)
    k_o[0] = k2
    v_o[0] = v.astype(BF16)
    kk_o[0] = kk
    b_o[0] = kk * iclr
    g_o[0] = gate
    bonus_o[0] = _seg_sum(r * k2 * rk_ref[...], sel_ref, selt_ref) * v


def _rwkv_proj(x, shift, mu, wrkv, w0, w1, w2, a0, a1, a2, g1, g2, k_k, k_a, r_k, sel, selt):
    bsz, t, d = x.shape
    tm = _row_tile(t, RWKV_PROJ_ROW_TILE)
    const2 = lambda bi, ti: (0, 0)
    const3 = lambda bi, ti: (0, 0, 0)
    full2 = lambda a: pl.BlockSpec(a.shape, const2)
    tile = pl.BlockSpec((1, tm, d), lambda bi, ti: (bi, ti, 0))
    out = jax.ShapeDtypeStruct((bsz, t, d), F32)
    return pl.pallas_call(
        functools.partial(_rwkv_proj_kernel, tm=tm),
        grid=(bsz, t // tm),
        in_specs=[tile, pl.BlockSpec((1, 1, d), lambda bi, ti: (bi, 0, 0)), full2(mu),
                  pl.BlockSpec(wrkv.shape, const3), full2(w0), full2(w1), full2(w2), full2(a0), full2(a1),
                  full2(a2), full2(g1), full2(g2), full2(k_k), full2(k_a), full2(r_k), full2(sel), full2(selt)],
        out_specs=[tile] * 8,
        out_shape=[out, out, out, jax.ShapeDtypeStruct((bsz, t, d), BF16), out, out, out, out],
        scratch_shapes=[pltpu.VMEM((V7X_SUBLANES, d), F32)],
        compiler_params=_params(("parallel", "arbitrary")),
        name="rwkv_proj",
    )(x, shift, mu, wrkv, w0, w1, w2, a0, a1, a2, g1, g2, k_k, k_a, r_k, sel, selt)


def _wkv_kernel(r_ref, lw_ref, k_ref, v_ref, kk_ref, b_ref, s0_ref, o_ref, slast_ref, s_s, *, c, nch):
    t = pl.program_id(1)
    n = RWKV_HEAD
    g = WKV_GROUP
    gl = g * n
    gc = g * c
    ngroups = RWKV_HEADS // g

    @pl.when(t == 0)
    def _():
        for h in range(RWKV_HEADS):
            s_s[:, h * n:(h + 1) * n] = s0_ref[0, h]

    iota = lambda shape, dim: lax.broadcasted_iota(jnp.int32, shape, dim)
    head_k = iota((1, gl), 1) // n
    head_j = iota((1, gc), 1) // c
    row_t = iota((c, gc), 0)
    col_j = iota((c, gc), 1) % c
    strict = row_t > col_j
    lower = row_t >= col_j
    hc = c // 2
    half_j = col_j // hc
    half_j = half_j[:1]
    blk_j = iota((1, gc), 1) // hc
    eye_h = jnp.where(iota((hc, gc), 0) == iota((hc, gc), 1) % hc, 1.0, 0.0).astype(F32)
    blk = (iota((gl, gl), 0) // n) == (iota((gl, gl), 1) // n)
    eye_l = iota((gl, gl), 0) == iota((gl, gl), 1)

    def bd(z, lane_head):
        zb = z.astype(BF16)
        return jnp.concatenate([jnp.where(lane_head == h, zb, jnp.zeros_like(zb)) for h in range(g)], axis=0)

    def bd_half(z):
        zb = z.astype(BF16)
        return jnp.concatenate([jnp.where(blk_j == q, zb, jnp.zeros_like(zb)) for q in range(2 * g)], axis=0)

    def mm(a, b):
        return jnp.dot(a.astype(BF16), b, preferred_element_type=F32)

    def mm_nt(a, b):
        return lax.dot_general(a.astype(BF16), b, _NT, preferred_element_type=F32)

    units = [(ch, gi) for ch in range(nch) for gi in range(ngroups)]
    u = {}
    for ch in range(nch):
        rows = slice(ch * c, (ch + 1) * c)
        lw = lw_ref[0, rows, :]
        cum = _cumsum_rows(lw)
        cum_last = cum[c - 1:c, :]
        e_neg = jnp.exp(-cum)
        e_end = jnp.exp(cum_last - cum)
        kk = kk_ref[0, rows, :]
        b = b_ref[0, rows, :]
        k = k_ref[0, rows, :]
        full = dict(a_t=kk * jnp.exp(cum - lw), b_t=b * e_neg, k_t=k * e_neg, r_t=r_ref[0, rows, :] * jnp.exp(cum),
                    b_e=b * e_end, k_e=k * e_end, d_end=jnp.exp(cum_last), v=v_ref[0, rows, :])
        for gi in range(ngroups):
            lanes = slice(gi * gl, (gi + 1) * gl)
            u[ch, gi] = {name: val[:, lanes] for name, val in full.items()}

    for key in units:
        d = u[key]
        x2 = jnp.concatenate([d['a_t'], d['r_t']], axis=0).astype(BF16)
        mb = mm_nt(x2, bd(d['b_t'], head_k))
        mk = mm_nt(x2, bd(d['k_t'], head_k))
        d['p_rb'] = jnp.where(lower, mb[c:], 0.0).astype(BF16)
        mk_lo = jnp.concatenate([jnp.where(strict, mk[:c], 0.0), jnp.where(lower, mk[c:], 0.0)], axis=0)
        wo = mm(mk_lo, bd(d['v'], head_k))
        d['w_h'] = wo[:c]
        d['o0'] = wo[c:]
        nmat = -jnp.where(strict, mb[:c], 0.0)
        d['n_low'] = jnp.where(jnp.logical_and(row_t >= hc, half_j == 0), nmat, 0.0)
        d['pw'] = jnp.where(half_j == 0, nmat[:hc], nmat[hc:])
        d['tinv'] = eye_h + d['pw']
    nsq = int(math.log2(hc)) - 1
    for level in range(nsq):
        for key in units:
            d = u[key]
            pwb = d['pw'].astype(BF16)
            w = bd_half(pwb)
            if level == 0:
                d['pw'] = mm(pwb, w)
            else:
                res = mm(jnp.concatenate([pwb, d['tinv'].astype(BF16)], axis=0), w)
                d['pw'] = res[:hc]
                d['tinv'] = d['tinv'] + res[hc:]
    for key in units:
        d = u[key]
        d['tinv'] = d['tinv'] + mm(d['tinv'], bd_half(d['pw']))
    for key in units:
        d = u[key]
        t_lo = jnp.where(half_j == 1, d['tinv'], 0.0)
        t_diag = jnp.concatenate([jnp.where(half_j == 0, d['tinv'], 0.0), t_lo], axis=0)
        corr = mm(mm(t_lo, bd(d['n_low'], head_j)), bd(t_diag, head_j))
        d['tinv'] = t_diag + jnp.concatenate([jnp.zeros((hc, gc), F32), corr], axis=0)
    for key in units:
        d = u[key]
        tb = d['tinv'].astype(BF16)
        d['a_hat'] = mm(tb, bd(d['a_t'], head_k))
        d['w_hat'] = mm(tb, bd(d['w_h'], head_k))
    for key in units:
        d = u[key]
        d['r_hat'] = d['r_t'] - mm(d['p_rb'], bd(d['a_hat'], head_k))
        d['o0'] = d['o0'] - mm(d['p_rb'], bd(d['w_hat'], head_k))
        b_e = d['b_e'].astype(BF16)
        gfull = lax.dot_general(d['a_hat'].astype(BF16), b_e, _TN, preferred_element_type=F32)
        d['gmat'] = (jnp.where(eye_l, d['d_end'], 0.0) - jnp.where(blk, gfull, 0.0)).astype(BF16)
        vw = jnp.concatenate([d['v'].astype(BF16), d['w_hat'].astype(BF16)], axis=0)
        kb = jnp.concatenate([d['k_e'].astype(BF16), -b_e], axis=0)
        zf = lax.dot_general(vw, kb, _TN, preferred_element_type=F32)
        hmat = jnp.where(head_k == 0, zf[:n], 0.0)
        for h in range(1, g):
            hmat = hmat + jnp.where(head_k == h, zf[h * n:(h + 1) * n], 0.0)
        d['hmat'] = hmat

    for gi in range(ngroups):
        lanes = slice(gi * gl, (gi + 1) * gl)
        s = s_s[:, lanes]
        for ch in range(nch):
            d = u[ch, gi]
            o_ref[0, ch * c:(ch + 1) * c, lanes] = d['o0'] + mm_nt(d['r_hat'], bd(s, head_k))
            s = mm(s, d['gmat']) + d['hmat']
        s_s[:, lanes] = s

    @pl.when(t == pl.num_programs(1) - 1)
    def _():
        for h in range(RWKV_HEADS):
            slast_ref[0, h] = s_s[:, h * n:(h + 1) * n]


def _wkv_scan(r, lw, k, v, kk, b, s0, c):
    bsz, t, d = r.shape
    n = RWKV_HEAD
    nch = WKV_CHUNKS_PER_STEP if t % (WKV_CHUNKS_PER_STEP * c) == 0 else 1
    tile = pl.BlockSpec((1, nch * c, d), lambda bi, ti: (bi, ti, 0))
    st = pl.BlockSpec((1, RWKV_HEADS, n, n), lambda bi, ti: (bi, 0, 0, 0))
    return pl.pallas_call(
        functools.partial(_wkv_kernel, c=c, nch=nch),
        grid=(bsz, t // (nch * c)),
        in_specs=[tile] * 6 + [st],
        out_specs=[tile, st],
        out_shape=[jax.ShapeDtypeStruct((bsz, t, d), F32), jax.ShapeDtypeStruct(s0.shape, F32)],
        scratch_shapes=[pltpu.VMEM((n, d), F32)],
        compiler_params=_params(("parallel", "arbitrary")),
        name="wkv_scan",
    )(r, lw, k, v, kk, b, s0)


def _wkv_out_kernel(x_ref, o_ref, bonus_ref, gate_ref, gng_ref, gnb_ref, sel_ref, selt_ref, w_ref,
                    g_ref, b_ref, y_ref):
    o = o_ref[...]
    inv_n = 1.0 / RWKV_HEAD
    mu = _seg_sum(o, sel_ref, selt_ref) * inv_n
    oc = o - mu
    var = _seg_sum(oc * oc, sel_ref, selt_ref) * inv_n
    on = oc * lax.rsqrt(var + RWKV_GN_EPS) * gng_ref[...] + gnb_ref[...]
    z = (on + bonus_ref[...]) * gate_ref[...]
    y = jnp.dot(z.astype(BF16), w_ref[...], preferred_element_type=F32)
    y_ref[...] = _layer_norm(ALPHA * x_ref[...] + y, g_ref[...], b_ref[...], LN_EPS)


def _wkv_out_postnorm(x, o, bonus, gate, gn_g, gn_b, sel, selt, w, g, b):
    n, d = x.shape
    tm = _row_tile(n)
    row = pl.BlockSpec((tm, d), lambda i: (i, 0))
    const = lambda i: (0, 0)
    full = lambda a: pl.BlockSpec(a.shape, const)
    return pl.pallas_call(
        _wkv_out_kernel,
        grid=(n // tm,),
        in_specs=[row, row, row, row, full(gn_g), full(gn_b), full(sel), full(selt), full(w), full(g), full(b)],
        out_specs=row,
        out_shape=jax.ShapeDtypeStruct((n, d), F32),
        compiler_params=_params(("parallel",)),
        name="wkv_out_postnorm",
    )(x, o, bonus, gate, gn_g, gn_b, sel, selt, w, g, b)


def _prep_weights(ln_g, ln_b, ffn_up, ffn_down, xa_q, xa_o, l0_w_in, l0_conv_w, l0_conv_b, l0_lru_wa,
                  l0_lru_ba, l0_lru_wx, l0_lru_bx, l0_lru_lambda, l0_ret_gn_g, l0_ret_gn_b, l0_w_out,
                  l1_mu, l1_w_rkv, l1_w0, l1_w1, l1_w2, l1_a0, l1_a1, l1_a2, l1_g1, l1_g2, l1_k_k, l1_k_a,
                  l1_r_k, l1_gn_g, l1_gn_b, l1_w_out):
    d = D_MODEL
    row = lambda a: a.reshape(1, -1).astype(F32)

    def block_diag(w):
        eye = jnp.eye(LRU_BLOCKS, dtype=w.dtype)
        return jnp.einsum('gij,gh->gihj', w, eye).reshape(LRU_WIDTH, LRU_WIDTH).astype(BF16)

    head_of_col = jnp.arange(d) // RWKV_HEAD
    sel = (head_of_col[:, None] == jnp.arange(SEL_WIDTH)[None, :]).astype(BF16)
    sel_row = jnp.arange(SEL_WIDTH)
    selt = ((sel_row[:, None] % RWKV_HEADS == head_of_col[None, :])
            & (sel_row[:, None] < 2 * RWKV_HEADS)).astype(BF16)
    return dict(
        ln_g=ln_g, ln_b=ln_b, ffn_up=ffn_up.astype(BF16), ffn_down=ffn_down.astype(BF16),
        xa_q=(xa_q * (MEM_HEAD_DIM ** -0.5)).astype(BF16), xa_o=xa_o.astype(BF16),
        w_in=l0_w_in.astype(BF16), conv_w=l0_conv_w, conv_b=row(l0_conv_b),
        lru_wa=block_diag(l0_lru_wa), lru_ba=row(l0_lru_ba), lru_wx=block_diag(l0_lru_wx),
        lru_bx=row(l0_lru_bx), lru_lam=row(l0_lru_lambda),
        ret_gn_g=row(l0_ret_gn_g), ret_gn_b=row(l0_ret_gn_b),
        w_out_a=l0_w_out[:LRU_WIDTH].astype(BF16), w_out_b=l0_w_out[LRU_WIDTH:].astype(BF16),
        mu=l1_mu, w_rkv=l1_w_rkv.astype(BF16), w0=row(l1_w0), w1=l1_w1.astype(BF16), w2=l1_w2.astype(BF16),
        a0=row(l1_a0), a1=l1_a1.astype(BF16), a2=l1_a2.astype(BF16), g1=l1_g1.astype(BF16),
        g2=l1_g2.astype(BF16), k_k=row(l1_k_k), k_a=row(l1_k_a), r_k=row(l1_r_k),
        gn_g=row(l1_gn_g), gn_b=row(l1_gn_b), w_out_c=l1_w_out.astype(BF16), sel=sel, selt=selt,
    )


def _rotary_tables(pos):
    half = RET_HEAD_DIM // 2
    inv_freq = ROPE_BASE ** (-jnp.arange(half, dtype=F32) / half)
    ang = pos.astype(F32)[:, None] * inv_freq[None, :]
    cos = jnp.cos(ang)
    sin = jnp.sin(ang)
    return jnp.concatenate([cos, cos], axis=-1), jnp.concatenate([-sin, sin], axis=-1)


def _run_trunk(x, pos, mem_k, mem_v, states, p):
    bsz, t, d = x.shape
    n = bsz * t
    (conv_buf, h0, s_ret), (shift, s_wkv) = states
    lng = lambda l, j: p['ln_g'][l, j].reshape(1, d)
    lnb = lambda l, j: p['ln_b'][l, j].reshape(1, d)
    flat = lambda a: a.reshape(n, a.shape[-1])
    chunk = min(CHUNK, t)
    ffn = lambda xx, l, j, nj: _ffn_postnorm(xx, p['ffn_up'], p['ffn_down'], l, j, lng(l, nj), lnb(l, nj))

    x = ffn(flat(x), 0, 0, 0)
    proj = _matmul(x, p['w_in'], BF16).reshape(bsz, t, -1)
    ya, new_buf, h_last = _lru_branch(proj, conv_buf, h0.reshape(bsz, 1, LRU_WIDTH), p['conv_w'], p['conv_b'],
                                      p['lru_wa'], p['lru_ba'], p['lru_wx'], p['lru_bx'], p['lru_lam'])
    cos2, sin2 = _rotary_tables(pos)
    yb, s_ret_new = _ret_branch(proj, cos2, sin2, s_ret, p['ret_gn_g'], p['ret_gn_b'], chunk)
    x = _ab_out_postnorm(x, flat(ya), flat(yb), p['w_out_a'], p['w_out_b'], lng(0, 1), lnb(0, 1))
    x = _xattn_postnorm(x.reshape(bsz, t, d), mem_k, mem_v, p['xa_q'], p['xa_o'], 0, lng(0, 2), lnb(0, 2))
    x = ffn(flat(x), 0, 1, 3)

    x = ffn(x, 1, 0, 0)
    x3 = x.reshape(bsz, t, d)
    r, lw, k2, v, kk, b, gate, bonus = _rwkv_proj(
        x3, shift, p['mu'], p['w_rkv'], p['w0'], p['w1'], p['w2'], p['a0'], p['a1'], p['a2'], p['g1'], p['g2'],
        p['k_k'], p['k_a'], p['r_k'], p['sel'], p['selt'])
    o, s_wkv_new = _wkv_scan(r, lw, k2, v, kk, b, s_wkv, chunk)
    new_shift = x3[:, t - 1:, :]
    x = _wkv_out_postnorm(x, flat(o), flat(bonus), flat(gate), p['gn_g'], p['gn_b'], p['sel'], p['selt'],
                          p['w_out_c'], lng(1, 1), lnb(1, 1))
    x = _xattn_postnorm(x.reshape(bsz, t, d), mem_k, mem_v, p['xa_q'], p['xa_o'], 1, lng(1, 2), lnb(1, 2))
    x = ffn(flat(x), 1, 1, 3)
    new_states = ((new_buf, h_last.reshape(bsz, LRU_WIDTH), s_ret_new), (new_shift, s_wkv_new))
    return x.reshape(bsz, t, d), new_states


def kernel(x_prompt, x_sample, mem_prompt, state_conv0, state_lru0, state_ret0, state_shift1, state_wkv1,
           cache_mem_k, cache_mem_v, ln_g, ln_b, ffn_up, ffn_down, xa_q, xa_k, xa_v, xa_o,
           l0_w_in, l0_conv_w, l0_conv_b, l0_lru_wa, l0_lru_ba, l0_lru_wx, l0_lru_bx, l0_lru_lambda,
           l0_ret_gn_g, l0_ret_gn_b, l0_w_out, l1_mu, l1_w_rkv, l1_w0, l1_w1, l1_w2, l1_a0, l1_a1, l1_a2,
           l1_g1, l1_g2, l1_k_k, l1_k_a, l1_r_k, l1_gn_g, l1_gn_b, l1_w_out):
    d = D_MODEL
    p = _prep_weights(ln_g, ln_b, ffn_up, ffn_down, xa_q, xa_o, l0_w_in, l0_conv_w, l0_conv_b, l0_lru_wa,
                      l0_lru_ba, l0_lru_wx, l0_lru_bx, l0_lru_lambda, l0_ret_gn_g, l0_ret_gn_b, l0_w_out,
                      l1_mu, l1_w_rkv, l1_w0, l1_w1, l1_w2, l1_a0, l1_a1, l1_a2, l1_g1, l1_g2, l1_k_k, l1_k_a,
                      l1_r_k, l1_gn_g, l1_gn_b, l1_w_out)

    bp, tp, _ = x_prompt.shape
    dt = x_prompt.dtype
    mem_k_p = _mem_proj(mem_prompt, xa_k.astype(BF16))
    mem_v_p = _mem_proj(mem_prompt, xa_v.astype(BF16))
    zero_states = ((jnp.zeros((bp, CONV_WIDTH - 1, LRU_WIDTH), dt), jnp.zeros((bp, LRU_WIDTH), dt),
                    jnp.zeros((bp, RET_HEADS, RET_HEAD_DIM, RET_HEAD_DIM), dt)),
                   (jnp.zeros((bp, 1, d), dt), jnp.zeros((bp, RWKV_HEADS, RWKV_HEAD, RWKV_HEAD), dt)))
    y_prompt, st_p = _run_trunk(x_prompt, jnp.arange(tp, dtype=jnp.int32), mem_k_p, mem_v_p, zero_states, p)

    bs, ts, _ = x_sample.shape
    pos_s = PAST_LEN + jnp.arange(ts, dtype=jnp.int32)
    sample_states = ((state_conv0, state_lru0, state_ret0), (state_shift1, state_wkv1))
    y_sample, st_s = _run_trunk(x_sample, pos_s, cache_mem_k, cache_mem_v, sample_states, p)

    (p_conv0, p_lru0, p_ret0), (p_shift1, p_wkv1) = st_p
    (s_conv0, s_lru0, s_ret0), (s_shift1, s_wkv1) = st_s
    return (y_prompt, y_sample, mem_k_p, mem_v_p,
            p_conv0, p_lru0, p_ret0, p_shift1, p_wkv1, s_conv0, s_lru0, s_ret0, s_shift1, s_wkv1)
```

```python
import functools
import math

import jax
import jax.numpy as jnp
from jax import lax
from jax.experimental import pallas as pl
from jax.experimental.pallas import tpu as pltpu

D_MODEL = 1024
DEPTH = 2
PAST_LEN = 4096
CHUNK = 64
N_MEM = 256
MEM_HEADS = 4
MEM_HEAD_DIM = D_MODEL // MEM_HEADS
D_FF = 2816
LRU_WIDTH = 512
LRU_BLOCKS = 8
LRU_BLOCK = LRU_WIDTH // LRU_BLOCKS
CONV_WIDTH = 4
LRU_C = 8.0
RET_HEADS = 4
RET_WIDTH = 512
RET_HEAD_DIM = RET_WIDTH // RET_HEADS
ROPE_BASE = 10000.0
RWKV_HEAD = 64
RWKV_HEADS = D_MODEL // RWKV_HEAD
LN_EPS = 1e-5
RWKV_GN_EPS = 64e-5
ALPHA = (2 * DEPTH) ** 0.25

F32 = jnp.float32
BF16 = jnp.bfloat16

V7X_SUBLANES = 8
V7X_LANES = 128
VMEM_LIMIT_BYTES = 56 * 1024 * 1024
FFN_CHUNK = 256
ROW_TILE = 1024
FFN_ROW_TILE = 1024
RWKV_PROJ_ROW_TILE = 512
SEL_WIDTH = V7X_LANES
WKV_GROUP = 4
WKV_CHUNKS_PER_STEP = 4
RET_CHUNKS_PER_STEP = 8

_NT = (((1,), (1,)), ((), ()))
_TN = (((0,), (0,)), ((), ()))


def _dot(a, b):
    return jnp.dot(a.astype(BF16), b.astype(BF16), preferred_element_type=F32)


def _dot_nt(a, b):
    return lax.dot_general(a.astype(BF16), b.astype(BF16), _NT, preferred_element_type=F32)


def _dot_tn(a, b):
    return lax.dot_general(a.astype(BF16), b.astype(BF16), _TN, preferred_element_type=F32)


def _head_bcast(vals, selt_ref):
    lane = lax.broadcasted_iota(jnp.int32, vals.shape, 1)
    vals = jnp.where(lane < RWKV_HEADS, vals, 0.0)
    hi = vals.astype(BF16).astype(F32)
    packed = hi + pltpu.roll(vals - hi, RWKV_HEADS, 1)
    return jnp.dot(packed.astype(BF16), selt_ref[...], preferred_element_type=F32)


def _seg_sum(z, sel_ref, selt_ref):
    sums = jnp.dot(z.astype(BF16), sel_ref[...], preferred_element_type=F32)
    return _head_bcast(sums, selt_ref)


def _cumsum_rows(x):
    rows = x.shape[0]
    row8 = lax.broadcasted_iota(jnp.int32, x.shape, 0) % V7X_SUBLANES
    s = 1
    while s < V7X_SUBLANES:
        x = jnp.where(row8 >= s, x + pltpu.roll(x, s, 0), x)
        s *= 2
    out = [x[:V7X_SUBLANES]]
    for j in range(1, rows // V7X_SUBLANES):
        out.append(x[j * V7X_SUBLANES:(j + 1) * V7X_SUBLANES] + out[-1][V7X_SUBLANES - 1:])
    return jnp.concatenate(out, axis=0)


def _layer_norm(y, g, b, eps):
    mu = jnp.mean(y, axis=-1, keepdims=True)
    yc = y - mu
    var = jnp.mean(yc * yc, axis=-1, keepdims=True)
    return yc * lax.rsqrt(var + eps) * g + b


def _sigmoid(x):
    return 1.0 / (1.0 + jnp.exp(-x))


def _softplus(x):
    return jnp.maximum(x, 0.0) + jnp.log1p(jnp.exp(-jnp.abs(x)))


def _gelu_tanh(x):
    return 0.5 * x * (1.0 + jnp.tanh(math.sqrt(2.0 / math.pi) * (x + 0.044715 * (x * x * x))))


def _params(sem):
    return pltpu.CompilerParams(dimension_semantics=sem, vmem_limit_bytes=VMEM_LIMIT_BYTES)


def _row_tile(n, pref=ROW_TILE):
    return pref if n % pref == 0 else n


def _ffn_kernel(x_ref, wup_ref, wdn_ref, g_ref, b_ref, o_ref):
    x = x_ref[...]
    xb = x.astype(BF16)
    acc = jnp.zeros(x.shape, F32)
    for lo in range(0, D_FF, FFN_CHUNK):
        hg = jnp.dot(xb, wup_ref[:, lo:lo + FFN_CHUNK], preferred_element_type=F32)
        hu = jnp.dot(xb, wup_ref[:, D_FF + lo:D_FF + lo + FFN_CHUNK], preferred_element_type=F32)
        h = hg * _sigmoid(hg) * hu
        acc = acc + jnp.dot(h.astype(BF16), wdn_ref[lo:lo + FFN_CHUNK, :], preferred_element_type=F32)
    o_ref[...] = _layer_norm(ALPHA * x + 0.5 * acc, g_ref[...], b_ref[...], LN_EPS)


def _ffn_postnorm(x, w_up, w_down, layer, which, g, b):
    n, d = x.shape
    tm = _row_tile(n, FFN_ROW_TILE)
    pick = lambda i: (layer, which, 0, 0)
    return pl.pallas_call(
        _ffn_kernel,
        grid=(n // tm,),
        in_specs=[
            pl.BlockSpec((tm, d), lambda i: (i, 0)),
            pl.BlockSpec((None, None, d, 2 * D_FF), pick, pipeline_mode=pl.Buffered(1)),
            pl.BlockSpec((None, None, D_FF, d), pick, pipeline_mode=pl.Buffered(1)),
            pl.BlockSpec((1, d), lambda i: (0, 0)),
            pl.BlockSpec((1, d), lambda i: (0, 0)),
        ],
        out_specs=pl.BlockSpec((tm, d), lambda i: (i, 0)),
        out_shape=jax.ShapeDtypeStruct((n, d), F32),
        compiler_params=_params(("parallel",)),
        name="ffn_postnorm",
    )(x, w_up, w_down, g, b)


def _mm_kernel(x_ref, w_ref, o_ref):
    o_ref[...] = jnp.dot(x_ref[...].astype(BF16), w_ref[...], preferred_element_type=F32).astype(o_ref.dtype)


def _matmul(x, w, out_dtype):
    n, k = x.shape
    m = w.shape[1]
    tm = _row_tile(n)
    return pl.pallas_call(
        _mm_kernel,
        grid=(n // tm,),
        in_specs=[pl.BlockSpec((tm, k), lambda i: (i, 0)), pl.BlockSpec((k, m), lambda i: (0, 0))],
        out_specs=pl.BlockSpec((tm, m), lambda i: (i, 0)),
        out_shape=jax.ShapeDtypeStruct((n, m), out_dtype),
        compiler_params=_params(("parallel",)),
        name="proj",
    )(x, w)


def _mem_proj_kernel(x_ref, w_ref, o_ref):
    y = jnp.dot(x_ref[...].astype(BF16), w_ref[...], preferred_element_type=F32)
    for h in range(MEM_HEADS):
        o_ref[:, h, :] = y[:, h * MEM_HEAD_DIM:(h + 1) * MEM_HEAD_DIM]


def _mem_proj(mem, w):
    bsz, m, d = mem.shape
    depth = w.shape[0]
    return pl.pallas_call(
        _mem_proj_kernel,
        grid=(depth, bsz),
        in_specs=[pl.BlockSpec((None, m, d), lambda l, bi: (bi, 0, 0)),
                  pl.BlockSpec((None, d, d), lambda l, bi: (l, 0, 0))],
        out_specs=pl.BlockSpec((None, None, m, MEM_HEADS, MEM_HEAD_DIM), lambda l, bi: (l, bi, 0, 0, 0)),
        out_shape=jax.ShapeDtypeStruct((depth, bsz, m, MEM_HEADS, MEM_HEAD_DIM), F32),
        compiler_params=_params(("parallel", "parallel")),
        name="mem_proj",
    )(mem, w)


def _ab_out_kernel(x_ref, ya_ref, yb_ref, wa_ref, wb_ref, g_ref, b_ref, o_ref):
    y = (jnp.dot(ya_ref[...].astype(BF16), wa_ref[...], preferred_element_type=F32)
         + jnp.dot(yb_ref[...].astype(BF16), wb_ref[...], preferred_element_type=F32))
    o_ref[...] = _layer_norm(ALPHA * x_ref[...] + y, g_ref[...], b_ref[...], LN_EPS)


def _ab_out_postnorm(x, ya, yb, wa, wb, g, b):
    n, d = x.shape
    ka = ya.shape[1]
    kb = yb.shape[1]
    tm = _row_tile(n)
    row = lambda i: (i, 0)
    const = lambda i: (0, 0)
    return pl.pallas_call(
        _ab_out_kernel,
        grid=(n // tm,),
        in_specs=[pl.BlockSpec((tm, d), row), pl.BlockSpec((tm, ka), row), pl.BlockSpec((tm, kb), row),
                  pl.BlockSpec((ka, d), const), pl.BlockSpec((kb, d), const),
                  pl.BlockSpec((1, d), const), pl.BlockSpec((1, d), const)],
        out_specs=pl.BlockSpec((tm, d), row),
        out_shape=jax.ShapeDtypeStruct((n, d), F32),
        compiler_params=_params(("parallel",)),
        name="ab_out_postnorm",
    )(x, ya, yb, wa, wb, g, b)


def _xattn_kernel(x_ref, k_ref, v_ref, wq_ref, wo_ref, g_ref, b_ref, o_ref, kb_s, vb_s):
    @pl.when(pl.program_id(1) == 0)
    def _():
        for h in range(MEM_HEADS):
            kb_s[h] = k_ref[:, h, :].astype(BF16)
            vb_s[h] = v_ref[:, h, :].astype(BF16)

    x = x_ref[0]
    q = jnp.dot(x.astype(BF16), wq_ref[...], preferred_element_type=F32)
    qb = q.astype(BF16)
    outs = []
    for h in range(MEM_HEADS):
        sl = slice(h * MEM_HEAD_DIM, (h + 1) * MEM_HEAD_DIM)
        s = lax.dot_general(qb[:, sl], kb_s[h], _NT, preferred_element_type=F32)
        m = jnp.max(s, axis=-1, keepdims=True)
        p = jnp.exp(s - m)
        p = p * (1.0 / jnp.sum(p, axis=-1, keepdims=True))
        outs.append(jnp.dot(p.astype(BF16), vb_s[h], preferred_element_type=F32))
    o = jnp.concatenate(outs, axis=-1)
    y = jnp.dot(o.astype(BF16), wo_ref[...], preferred_element_type=F32)
    o_ref[0] = _layer_norm(ALPHA * x + y, g_ref[...], b_ref[...], LN_EPS)


def _xattn_postnorm(x, mem_k, mem_v, wq, wo, layer, g, b):
    bsz, t, d = x.shape
    tm = _row_tile(t)
    const = lambda bi, ti: (0, 0)
    mem_spec = pl.BlockSpec((None, None, N_MEM, MEM_HEADS, MEM_HEAD_DIM), lambda bi, ti: (layer, bi, 0, 0, 0))
    w_spec = pl.BlockSpec((None, d, d), lambda bi, ti: (layer, 0, 0))
    return pl.pallas_call(
        _xattn_kernel,
        grid=(bsz, t // tm),
        in_specs=[pl.BlockSpec((1, tm, d), lambda bi, ti: (bi, ti, 0)),
                  mem_spec, mem_spec, w_spec, w_spec,
                  pl.BlockSpec((1, d), const), pl.BlockSpec((1, d), const)],
        out_specs=pl.BlockSpec((1, tm, d), lambda bi, ti: (bi, ti, 0)),
        out_shape=jax.ShapeDtypeStruct((bsz, t, d), F32),
        scratch_shapes=[pltpu.VMEM((MEM_HEADS, N_MEM, MEM_HEAD_DIM), BF16)] * 2,
        compiler_params=_params(("parallel", "arbitrary")),
        name="xattn_postnorm",
    )(x, mem_k, mem_v, wq, wo, g, b)


def _lru_kernel(xa_ref, ga_ref, cbuf_ref, h0_ref, cw_ref, cb_ref, wa_ref, ba_ref, wx_ref, bx_ref, lam_ref,
                ya_ref, nbuf_ref, hlast_ref, prev_s, h_s, a_s, u_s, *, tc):
    t = pl.program_id(1)

    @pl.when(t == 0)
    def _():
        prev_s[...] = jnp.zeros(prev_s.shape, F32)
        prev_s[V7X_SUBLANES - (CONV_WIDTH - 1):, :] = cbuf_ref[0]
        h_s[...] = h0_ref[0]

    xa = xa_ref[0].astype(F32)
    ext = jnp.concatenate([prev_s[...], xa], axis=0)
    xc = cb_ref[...] + xa * cw_ref[CONV_WIDTH - 1:CONV_WIDTH, :]
    for s in range(1, CONV_WIDTH):
        shifted = pltpu.roll(ext, s, 0)[V7X_SUBLANES:, :]
        xc = xc + shifted * cw_ref[CONV_WIDTH - 1 - s:CONV_WIDTH - s, :]
    prev_s[...] = xa[tc - V7X_SUBLANES:, :]
    nbuf_ref[0] = xa[tc - (CONV_WIDTH - 1):, :]

    xcb = xc.astype(BF16)
    r = _sigmoid(jnp.dot(xcb, wa_ref[...], preferred_element_type=F32) + ba_ref[...])
    i = _sigmoid(jnp.dot(xcb, wx_ref[...], preferred_element_type=F32) + bx_ref[...])
    log_a = (-LRU_C * _softplus(-lam_ref[...])) * r
    a = jnp.exp(log_a)
    th = jnp.tanh(log_a)
    u = jnp.sqrt(-2.0 * th / (1.0 - th)) * (i * xc)

    row8 = lax.broadcasted_iota(jnp.int32, a.shape, 0) % V7X_SUBLANES
    s = 1
    while s < V7X_SUBLANES:
        inside = row8 >= s
        u = jnp.where(inside, u + a * pltpu.roll(u, s, 0), u)
        a = jnp.where(inside, a * pltpu.roll(a, s, 0), a)
        s *= 2
    a_s[...] = a
    u_s[...] = u

    def body(j, h):
        base = pl.multiple_of(j * V7X_SUBLANES, V7X_SUBLANES)
        hb = u_s[pl.ds(base, V7X_SUBLANES), :] + a_s[pl.ds(base, V7X_SUBLANES), :] * h
        u_s[pl.ds(base, V7X_SUBLANES), :] = hb
        return hb[V7X_SUBLANES - 1:, :]

    h = lax.fori_loop(0, tc // V7X_SUBLANES, body, h_s[...], unroll=4)
    h_s[...] = h
    hlast_ref[0] = h
    ya_ref[0] = u_s[...] * _gelu_tanh(ga_ref[0].astype(F32))


def _lru_branch(proj, conv_buf, h0, cw, cb, wa, ba, wx, bx, lam):
    bsz, t, _ = proj.shape
    w = LRU_WIDTH
    tc = _row_tile(t)
    const = lambda bi, ti: (0, 0)
    per_b = lambda bi, ti: (bi, 0, 0)
    return pl.pallas_call(
        functools.partial(_lru_kernel, tc=tc),
        grid=(bsz, t // tc),
        in_specs=[pl.BlockSpec((1, tc, w), lambda bi, ti: (bi, ti, 0)),
                  pl.BlockSpec((1, tc, w), lambda bi, ti: (bi, ti, 1)),
                  pl.BlockSpec((1, CONV_WIDTH - 1, w), per_b),
                  pl.BlockSpec((1, 1, w), per_b),
                  pl.BlockSpec((CONV_WIDTH, w), const), pl.BlockSpec((1, w), const),
                  pl.BlockSpec((w, w), const), pl.BlockSpec((1, w), const),
                  pl.BlockSpec((w, w), const), pl.BlockSpec((1, w), const),
                  pl.BlockSpec((1, w), const)],
        out_specs=[pl.BlockSpec((1, tc, w), lambda bi, ti: (bi, ti, 0)),
                   pl.BlockSpec((1, CONV_WIDTH - 1, w), per_b),
                   pl.BlockSpec((1, 1, w), per_b)],
        out_shape=[jax.ShapeDtypeStruct((bsz, t, w), F32),
                   jax.ShapeDtypeStruct((bsz, CONV_WIDTH - 1, w), F32),
                   jax.ShapeDtypeStruct((bsz, 1, w), F32)],
        scratch_shapes=[pltpu.VMEM((V7X_SUBLANES, w), F32), pltpu.VMEM((1, w), F32),
                        pltpu.VMEM((tc, w), F32), pltpu.VMEM((tc, w), F32)],
        compiler_params=_params(("parallel", "arbitrary")),
        name="lru_branch",
    )(proj, proj, conv_buf, h0, cw, cb, wa, ba, wx, bx, lam)


def _ret_kernel(q_ref, k_ref, v_ref, g_ref, cos_ref, sin_ref, s0_ref, gng_ref, gnb_ref,
                yb_ref, slast_ref, s_s, *, c, nch):
    t = pl.program_id(1)

    @pl.when(t == 0)
    def _():
        s_s[...] = s0_ref[0]

    half = RET_HEAD_DIM // 2
    ri = lax.broadcasted_iota(jnp.int32, (c, c), 0)
    ci = lax.broadcasted_iota(jnp.int32, (c, c), 1)
    diff = (ri - ci).astype(F32)
    it = lax.broadcasted_iota(jnp.int32, (c, 1), 0).astype(F32)
    log_g = [math.log1p(-(2.0 ** (-5.0 - h))) for h in range(RET_HEADS)]
    dmask = [jnp.where(diff >= 0, jnp.exp(lg * jnp.maximum(diff, 0.0)), 0.0) for lg in log_g]
    xi = [jnp.exp(lg * (it + 1.0)) for lg in log_g]
    zeta = [jnp.exp(lg * (c - 1.0 - it)) for lg in log_g]

    units = [(ch, h) for ch in range(nch) for h in range(RET_HEADS)]
    qr, inner, kv = {}, {}, {}
    for ch, h in units:
        rows = slice(ch * c, (ch + 1) * c)
        sl = slice(h * RET_HEAD_DIM, (h + 1) * RET_HEAD_DIM)
        cos2 = cos_ref[rows, :]
        sin2 = sin_ref[rows, :]
        qh = q_ref[0, rows, sl].astype(F32)
        kh = k_ref[0, rows, sl].astype(F32)
        vh = v_ref[0, rows, sl].astype(BF16)
        qr[ch, h] = (qh * cos2 + pltpu.roll(qh, half, 1) * sin2).astype(BF16)
        kr = (kh * cos2 + pltpu.roll(kh, half, 1) * sin2) * (RET_HEAD_DIM ** -0.5)
        scores = _dot_nt(qr[ch, h], kr) * dmask[h]
        inner[ch, h] = _dot(scores, vh)
        kv[ch, h] = _dot_tn(kr * zeta[h], vh)
    for h in range(RET_HEADS):
        sl = slice(h * RET_HEAD_DIM, (h + 1) * RET_HEAD_DIM)
        s = s_s[h]
        outs = []
        for ch in range(nch):
            outs.append(inner[ch, h] + _dot(qr[ch, h], s) * xi[h])
            s = math.exp(log_g[h] * c) * s + kv[ch, h]
        s_s[h] = s
        on = _layer_norm(jnp.concatenate(outs, axis=0), gng_ref[:, sl], gnb_ref[:, sl], LN_EPS)
        gh = g_ref[0, :, sl].astype(F32)
        yb_ref[0, :, sl] = on * (gh * _sigmoid(gh))
    slast_ref[0] = s_s[...]


def _ret_branch(proj, cos2, sin2, s0, gn_g, gn_b, c):
    bsz, t, _ = proj.shape
    w = RET_WIDTH
    dk = RET_HEAD_DIM
    nch = RET_CHUNKS_PER_STEP if t % (RET_CHUNKS_PER_STEP * c) == 0 else 1
    const = lambda bi, ti: (0, 0)
    per_b = lambda bi, ti: (bi, 0, 0, 0)
    col = lambda j: (lambda bi, ti: (bi, ti, j))
    tc = nch * c
    return pl.pallas_call(
        functools.partial(_ret_kernel, c=c, nch=nch),
        grid=(bsz, t // tc),
        in_specs=[pl.BlockSpec((1, tc, w), col(2)), pl.BlockSpec((1, tc, w), col(3)),
                  pl.BlockSpec((1, tc, w), col(4)), pl.BlockSpec((1, tc, w), col(5)),
                  pl.BlockSpec((tc, dk), lambda bi, ti: (ti, 0)), pl.BlockSpec((tc, dk), lambda bi, ti: (ti, 0)),
                  pl.BlockSpec((1, RET_HEADS, dk, dk), per_b),
                  pl.BlockSpec((1, w), const), pl.BlockSpec((1, w), const)],
        out_specs=[pl.BlockSpec((1, tc, w), lambda bi, ti: (bi, ti, 0)),
                   pl.BlockSpec((1, RET_HEADS, dk, dk), per_b)],
        out_shape=[jax.ShapeDtypeStruct((bsz, t, w), F32),
                   jax.ShapeDtypeStruct((bsz, RET_HEADS, dk, dk), F32)],
        scratch_shapes=[pltpu.VMEM((RET_HEADS, dk, dk), F32)],
        compiler_params=_params(("parallel", "arbitrary")),
        name="ret_branch",
    )(proj, proj, proj, proj, cos2, sin2, s0, gn_g, gn_b)


def _rwkv_proj_kernel(x_ref, sh_ref, mu_ref, wrkv_ref, w0_ref, w1_ref, w2_ref, a0_ref, a1_ref, a2_ref,
                      g1_ref, g2_ref, kk_ref, ka_ref, rk_ref, sel_ref, selt_ref,
                      r_o, lw_o, k_o, v_o, kk_o, b_o, g_o, bonus_o, prev_s, *, tm):
    t = pl.program_id(1)

    @pl.when(t == 0)
    def _():
        prev_s[...] = jnp.zeros(prev_s.shape, F32)
        prev_s[V7X_SUBLANES - 1:, :] = sh_ref[0]

    x = x_ref[0]
    ext = jnp.concatenate([prev_s[...], x], axis=0)
    xp = pltpu.roll(ext, 1, 0)[V7X_SUBLANES:, :]
    prev_s[...] = x[tm - V7X_SUBLANES:, :]
    dx = xp - x

    def mix(p):
        return (x + dx * mu_ref[p:p + 1, :]).astype(BF16)

    r = jnp.dot(mix(0), wrkv_ref[0], preferred_element_type=F32)
    k = jnp.dot(mix(1), wrkv_ref[1], preferred_element_type=F32)
    v = jnp.dot(mix(2), wrkv_ref[2], preferred_element_type=F32)
    wl = _dot(jnp.tanh(jnp.dot(mix(3), w1_ref[...], preferred_element_type=F32)), w2_ref[...])
    w = -_softplus(-(w0_ref[...] + wl)) - 0.5
    al = _dot(jnp.dot(mix(4), a1_ref[...], preferred_element_type=F32), a2_ref[...])
    iclr = _sigmoid(a0_ref[...] + al)
    gate = _dot(_sigmoid(jnp.dot(mix(5), g1_ref[...], preferred_element_type=F32)), g2_ref[...])

    kk = k * kk_ref[...]
    sq = jnp.dot((kk * kk).astype(BF16), sel_ref[...], preferred_element_type=F32)
    inv_norm = 1.0 / jnp.maximum(jnp.sqrt(sq), 1e-12)
    kk = kk * _head_bcast(inv_norm, selt_ref)
    k2 = k * (1.0 + (iclr - 1.0) * ka_ref[...])

    r_o[0] = r
    lw_o[0] = -jnp.exp(w)
    k_o[0] = k2
    v_o[0] = v.astype(BF16)
    kk_o[0] = kk
    b_o[0] = kk * iclr
    g_o[0] = gate
    bonus_o[0] = _seg_sum(r * k2 * rk_ref[...], sel_ref, selt_ref) * v


def _rwkv_proj(x, shift, mu, wrkv, w0, w1, w2, a0, a1, a2, g1, g2, k_k, k_a, r_k, sel, selt):
    bsz, t, d = x.shape
    tm = _row_tile(t, RWKV_PROJ_ROW_TILE)
    const2 = lambda bi, ti: (0, 0)
    const3 = lambda bi, ti: (0, 0, 0)
    full2 = lambda a: pl.BlockSpec(a.shape, const2)
    tile = pl.BlockSpec((1, tm, d), lambda bi, ti: (bi, ti, 0))
    out = jax.ShapeDtypeStruct((bsz, t, d), F32)
    return pl.pallas_call(
        functools.partial(_rwkv_proj_kernel, tm=tm),
        grid=(bsz, t // tm),
        in_specs=[tile, pl.BlockSpec((1, 1, d), lambda bi, ti: (bi, 0, 0)), full2(mu),
                  pl.BlockSpec(wrkv.shape, const3), full2(w0), full2(w1), full2(w2), full2(a0), full2(a1),
                  full2(a2), full2(g1), full2(g2), full2(k_k), full2(k_a), full2(r_k), full2(sel), full2(selt)],
        out_specs=[tile] * 8,
        out_shape=[out, out, out, jax.ShapeDtypeStruct((bsz, t, d), BF16), out, out, out, out],
        scratch_shapes=[pltpu.VMEM((V7X_SUBLANES, d), F32)],
        compiler_params=_params(("parallel", "arbitrary")),
        name="rwkv_proj",
    )(x, shift, mu, wrkv, w0, w1, w2, a0, a1, a2, g1, g2, k_k, k_a, r_k, sel, selt)


def _wkv_kernel(r_ref, lw_ref, k_ref, v_ref, kk_ref, b_ref, s0_ref, o_ref, slast_ref,
                s_s, o0_s, rhat_s, gmat_s, hmat_s, *, c, nch):
    t = pl.program_id(1)
    last = pl.num_programs(1) - 1
    n = RWKV_HEAD
    block_refs = (r_ref, lw_ref, k_ref, v_ref, kk_ref, b_ref, o_ref, s_s, o0_s, rhat_s, gmat_s, hmat_s)

    @pl.when(t == 0)
    def _():
        for h in range(RWKV_HEADS):
            s_s[:, h * n:(h + 1) * n] = s0_ref[0, h]
        o0_s[...] = jnp.zeros(o0_s.shape, o0_s.dtype)
        rhat_s[...] = jnp.zeros(rhat_s.shape, rhat_s.dtype)
        gmat_s[...] = jnp.zeros(gmat_s.shape, gmat_s.dtype)
        hmat_s[...] = jnp.zeros(hmat_s.shape, hmat_s.dtype)

    @pl.when(t < last)
    def _():
        _wkv_block(*block_refs, c=c, nch=nch, prepare=True)

    @pl.when(t == last)
    def _():
        _wkv_block(*block_refs, c=c, nch=nch, prepare=False)
        for h in range(RWKV_HEADS):
            slast_ref[0, h] = s_s[:, h * n:(h + 1) * n]


def _wkv_block(r_ref, lw_ref, k_ref, v_ref, kk_ref, b_ref, o_ref, s_s, o0_s, rhat_s, gmat_s, hmat_s, *,
               c, nch, prepare):
    t = pl.program_id(1)
    n = RWKV_HEAD
    g = WKV_GROUP
    gl = g * n
    gc = g * c
    ngroups = RWKV_HEADS // g

    iota = lambda shape, dim: lax.broadcasted_iota(jnp.int32, shape, dim)
    head_k = iota((1, gl), 1) // n
    head_j = iota((1, gc), 1) // c
    row_t = iota((c, gc), 0)
    col_j = iota((c, gc), 1) % c
    strict = row_t > col_j
    lower = row_t >= col_j
    hc = c // 2
    half_j = col_j // hc
    half_j = half_j[:1]
    blk_j = iota((1, gc), 1) // hc
    eye_h = jnp.where(iota((hc, gc), 0) == iota((hc, gc), 1) % hc, 1.0, 0.0).astype(F32)
    blk = (iota((gl, gl), 0) // n) == (iota((gl, gl), 1) // n)
    eye_l = iota((gl, gl), 0) == iota((gl, gl), 1)

    def bd(z, lane_head):
        zb = z.astype(BF16)
        return jnp.concatenate([jnp.where(lane_head == h, zb, jnp.zeros_like(zb)) for h in range(g)], axis=0)

    def bd_half(z):
        zb = z.astype(BF16)
        return jnp.concatenate([jnp.where(blk_j == q, zb, jnp.zeros_like(zb)) for q in range(2 * g)], axis=0)

    def mm(a, b):
        return jnp.dot(a.astype(BF16), b, preferred_element_type=F32)

    def mm_nt(a, b):
        return lax.dot_general(a.astype(BF16), b, _NT, preferred_element_type=F32)

    started = t > 0
    group_lanes = [slice(gi * gl, (gi + 1) * gl) for gi in range(ngroups)]
    state = [s_s[:, lanes] for lanes in group_lanes]

    def recurrence_step(ch):
        rows = slice(ch * c, (ch + 1) * c)
        for gi, lanes in enumerate(group_lanes):
            s = state[gi]
            o_ref[0, rows, lanes] = o0_s[rows, lanes] + mm_nt(rhat_s[rows, lanes], bd(s, head_k))
            s = mm(s, gmat_s[ch * ngroups + gi]) + hmat_s[ch, :, lanes]
            if ch == nch - 1:
                s_s[:, lanes] = jnp.where(started, s, s_s[:, lanes])
            state[gi] = s

    if not prepare:
        for ch in range(nch):
            recurrence_step(ch)
        return

    units = [(ch, gi) for ch in range(nch) for gi in range(ngroups)]
    u = {}
    for ch in range(nch):
        rows = slice(ch * c, (ch + 1) * c)
        lw = lw_ref[0, rows, :]
        cum = _cumsum_rows(lw)
        cum_last = cum[c - 1:c, :]
        e_neg = jnp.exp(-cum)
        e_end = jnp.exp(cum_last - cum)
        kk = kk_ref[0, rows, :]
        b = b_ref[0, rows, :]
        k = k_ref[0, rows, :]
        full = dict(a_t=kk * jnp.exp(cum - lw), b_t=b * e_neg, k_t=k * e_neg, r_t=r_ref[0, rows, :] * jnp.exp(cum),
                    b_e=b * e_end, k_e=k * e_end, d_end=jnp.exp(cum_last), v=v_ref[0, rows, :])
        for gi in range(ngroups):
            lanes = slice(gi * gl, (gi + 1) * gl)
            u[ch, gi] = {name: val[:, lanes] for name, val in full.items()}

    for key in units:
        d = u[key]
        x2 = jnp.concatenate([d['a_t'], d['r_t']], axis=0).astype(BF16)
        mb = mm_nt(x2, bd(d['b_t'], head_k))
        mk = mm_nt(x2, bd(d['k_t'], head_k))
        d['p_rb'] = jnp.where(lower, mb[c:], 0.0).astype(BF16)
        mk_lo = jnp.concatenate([jnp.where(strict, mk[:c], 0.0), jnp.where(lower, mk[c:], 0.0)], axis=0)
        wo = mm(mk_lo, bd(d['v'], head_k))
        d['w_h'] = wo[:c]
        d['o0'] = wo[c:]
        nmat = -jnp.where(strict, mb[:c], 0.0)
        d['n_low'] = jnp.where(jnp.logical_and(row_t >= hc, half_j == 0), nmat, 0.0)
        d['pw'] = jnp.where(half_j == 0, nmat[:hc], nmat[hc:])
        d['tinv'] = eye_h + d['pw']
        if key[1] == ngroups - 1:
            recurrence_step(key[0])
    nsq = int(math.log2(hc)) - 1
    for level in range(nsq):
        for key in units:
            d = u[key]
            pwb = d['pw'].astype(BF16)
            w = bd_half(pwb)
            if level == 0:
                d['pw'] = mm(pwb, w)
            else:
                res = mm(jnp.concatenate([pwb, d['tinv'].astype(BF16)], axis=0), w)
                d['pw'] = res[:hc]
                d['tinv'] = d['tinv'] + res[hc:]
    for key in units:
        d = u[key]
        d['tinv'] = d['tinv'] + mm(d['tinv'], bd_half(d['pw']))
    for key in units:
        d = u[key]
        t_lo = jnp.where(half_j == 1, d['tinv'], 0.0)
        t_diag = jnp.concatenate([jnp.where(half_j == 0, d['tinv'], 0.0), t_lo], axis=0)
        corr = mm(mm(t_lo, bd(d['n_low'], head_j)), bd(t_diag, head_j))
        d['tinv'] = t_diag + jnp.concatenate([jnp.zeros((hc, gc), F32), corr], axis=0)
    for key in units:
        d = u[key]
        tb = d['tinv'].astype(BF16)
        d['a_hat'] = mm(tb, bd(d['a_t'], head_k))
        d['w_hat'] = mm(tb, bd(d['w_h'], head_k))
    for key in units:
        d = u[key]
        d['r_hat'] = d['r_t'] - mm(d['p_rb'], bd(d['a_hat'], head_k))
        d['o0'] = d['o0'] - mm(d['p_rb'], bd(d['w_hat'], head_k))
        b_e = d['b_e'].astype(BF16)
        gfull = lax.dot_general(d['a_hat'].astype(BF16), b_e, _TN, preferred_element_type=F32)
        d['gmat'] = (jnp.where(eye_l, d['d_end'], 0.0) - jnp.where(blk, gfull, 0.0)).astype(BF16)
        vw = jnp.concatenate([d['v'].astype(BF16), d['w_hat'].astype(BF16)], axis=0)
        kb = jnp.concatenate([d['k_e'].astype(BF16), -b_e], axis=0)
        zf = lax.dot_general(vw, kb, _TN, preferred_element_type=F32)
        hmat = jnp.where(head_k == 0, zf[:n], 0.0)
        for h in range(1, g):
            hmat = hmat + jnp.where(head_k == h, zf[h * n:(h + 1) * n], 0.0)
        d['hmat'] = hmat

    for (ch, gi) in units:
        d = u[ch, gi]
        lanes = slice(gi * gl, (gi + 1) * gl)
        o0_s[ch * c:(ch + 1) * c, lanes] = d['o0']
        rhat_s[ch * c:(ch + 1) * c, lanes] = d['r_hat'].astype(BF16)
        gmat_s[ch * ngroups + gi] = d['gmat']
        hmat_s[ch, :, lanes] = d['hmat']


def _wkv_scan(r, lw, k, v, kk, b, s0, c):
    bsz, t, d = r.shape
    n = RWKV_HEAD
    nch = WKV_CHUNKS_PER_STEP if t % (WKV_CHUNKS_PER_STEP * c) == 0 else 1
    nsteps = t // (nch * c)
    g = WKV_GROUP
    tile_in = pl.BlockSpec((1, nch * c, d), lambda bi, ti: (bi, jnp.minimum(ti, nsteps - 1), 0))
    tile_out = pl.BlockSpec((1, nch * c, d), lambda bi, ti: (bi, jnp.maximum(ti - 1, 0), 0))
    st = pl.BlockSpec((1, RWKV_HEADS, n, n), lambda bi, ti: (bi, 0, 0, 0))
    return pl.pallas_call(
        functools.partial(_wkv_kernel, c=c, nch=nch),
        grid=(bsz, nsteps + 1),
        in_specs=[tile_in] * 6 + [st],
        out_specs=[tile_out, st],
        out_shape=[jax.ShapeDtypeStruct((bsz, t, d), F32), jax.ShapeDtypeStruct(s0.shape, F32)],
        scratch_shapes=[pltpu.VMEM((n, d), F32), pltpu.VMEM((nch * c, d), F32), pltpu.VMEM((nch * c, d), BF16),
                        pltpu.VMEM((nch * (RWKV_HEADS // g), g * n, g * n), BF16), pltpu.VMEM((nch, n, d), F32)],
        compiler_params=_params(("parallel", "arbitrary")),
        name="wkv_scan",
    )(r, lw, k, v, kk, b, s0)


def _wkv_out_kernel(x_ref, o_ref, bonus_ref, gate_ref, gng_ref, gnb_ref, sel_ref, selt_ref, w_ref,
                    g_ref, b_ref, y_ref):
    o = o_ref[...]
    inv_n = 1.0 / RWKV_HEAD
    mu = _seg_sum(o, sel_ref, selt_ref) * inv_n
    oc = o - mu
    var = _seg_sum(oc * oc, sel_ref, selt_ref) * inv_n
    on = oc * lax.rsqrt(var + RWKV_GN_EPS) * gng_ref[...] + gnb_ref[...]
    z = (on + bonus_ref[...]) * gate_ref[...]
    y = jnp.dot(z.astype(BF16), w_ref[...], preferred_element_type=F32)
    y_ref[...] = _layer_norm(ALPHA * x_ref[...] + y, g_ref[...], b_ref[...], LN_EPS)


def _wkv_out_postnorm(x, o, bonus, gate, gn_g, gn_b, sel, selt, w, g, b):
    n, d = x.shape
    tm = _row_tile(n)
    row = pl.BlockSpec((tm, d), lambda i: (i, 0))
    const = lambda i: (0, 0)
    full = lambda a: pl.BlockSpec(a.shape, const)
    return pl.pallas_call(
        _wkv_out_kernel,
        grid=(n // tm,),
        in_specs=[row, row, row, row, full(gn_g), full(gn_b), full(sel), full(selt), full(w), full(g), full(b)],
        out_specs=row,
        out_shape=jax.ShapeDtypeStruct((n, d), F32),
        compiler_params=_params(("parallel",)),
        name="wkv_out_postnorm",
    )(x, o, bonus, gate, gn_g, gn_b, sel, selt, w, g, b)


def _prep_weights(ln_g, ln_b, ffn_up, ffn_down, xa_q, xa_o, l0_w_in, l0_conv_w, l0_conv_b, l0_lru_wa,
                  l0_lru_ba, l0_lru_wx, l0_lru_bx, l0_lru_lambda, l0_ret_gn_g, l0_ret_gn_b, l0_w_out,
                  l1_mu, l1_w_rkv, l1_w0, l1_w1, l1_w2, l1_a0, l1_a1, l1_a2, l1_g1, l1_g2, l1_k_k, l1_k_a,
                  l1_r_k, l1_gn_g, l1_gn_b, l1_w_out):
    d = D_MODEL
    row = lambda a: a.reshape(1, -1).astype(F32)

    def block_diag(w):
        eye = jnp.eye(LRU_BLOCKS, dtype=w.dtype)
        return jnp.einsum('gij,gh->gihj', w, eye).reshape(LRU_WIDTH, LRU_WIDTH).astype(BF16)

    head_of_col = jnp.arange(d) // RWKV_HEAD
    sel = (head_of_col[:, None] == jnp.arange(SEL_WIDTH)[None, :]).astype(BF16)
    sel_row = jnp.arange(SEL_WIDTH)
    selt = ((sel_row[:, None] % RWKV_HEADS == head_of_col[None, :])
            & (sel_row[:, None] < 2 * RWKV_HEADS)).astype(BF16)
    return dict(
        ln_g=ln_g, ln_b=ln_b, ffn_up=ffn_up.astype(BF16), ffn_down=ffn_down.astype(BF16),
        xa_q=(xa_q * (MEM_HEAD_DIM ** -0.5)).astype(BF16), xa_o=xa_o.astype(BF16),
        w_in=l0_w_in.astype(BF16), conv_w=l0_conv_w, conv_b=row(l0_conv_b),
        lru_wa=block_diag(l0_lru_wa), lru_ba=row(l0_lru_ba), lru_wx=block_diag(l0_lru_wx),
        lru_bx=row(l0_lru_bx), lru_lam=row(l0_lru_lambda),
        ret_gn_g=row(l0_ret_gn_g), ret_gn_b=row(l0_ret_gn_b),
        w_out_a=l0_w_out[:LRU_WIDTH].astype(BF16), w_out_b=l0_w_out[LRU_WIDTH:].astype(BF16),
        mu=l1_mu, w_rkv=l1_w_rkv.astype(BF16), w0=row(l1_w0), w1=l1_w1.astype(BF16), w2=l1_w2.astype(BF16),
        a0=row(l1_a0), a1=l1_a1.astype(BF16), a2=l1_a2.astype(BF16), g1=l1_g1.astype(BF16),
        g2=l1_g2.astype(BF16), k_k=row(l1_k_k), k_a=row(l1_k_a), r_k=row(l1_r_k),
        gn_g=row(l1_gn_g), gn_b=row(l1_gn_b), w_out_c=l1_w_out.astype(BF16), sel=sel, selt=selt,
    )


def _rotary_tables(pos):
    half = RET_HEAD_DIM // 2
    inv_freq = ROPE_BASE ** (-jnp.arange(half, dtype=F32) / half)
    ang = pos.astype(F32)[:, None] * inv_freq[None, :]
    cos = jnp.cos(ang)
    sin = jnp.sin(ang)
    return jnp.concatenate([cos, cos], axis=-1), jnp.concatenate([-sin, sin], axis=-1)


def _run_trunk(x, pos, mem_k, mem_v, states, p):
    bsz, t, d = x.shape
    n = bsz * t
    (conv_buf, h0, s_ret), (shift, s_wkv) = states
    lng = lambda l, j: p['ln_g'][l, j].reshape(1, d)
    lnb = lambda l, j: p['ln_b'][l, j].reshape(1, d)
    flat = lambda a: a.reshape(n, a.shape[-1])
    chunk = min(CHUNK, t)
    ffn = lambda xx, l, j, nj: _ffn_postnorm(xx, p['ffn_up'], p['ffn_down'], l, j, lng(l, nj), lnb(l, nj))

    x = ffn(flat(x), 0, 0, 0)
    proj = _matmul(x, p['w_in'], BF16).reshape(bsz, t, -1)
    ya, new_buf, h_last = _lru_branch(proj, conv_buf, h0.reshape(bsz, 1, LRU_WIDTH), p['conv_w'], p['conv_b'],
                                      p['lru_wa'], p['lru_ba'], p['lru_wx'], p['lru_bx'], p['lru_lam'])
    cos2, sin2 = _rotary_tables(pos)
    yb, s_ret_new = _ret_branch(proj, cos2, sin2, s_ret, p['ret_gn_g'], p['ret_gn_b'], chunk)
    x = _ab_out_postnorm(x, flat(ya), flat(yb), p['w_out_a'], p['w_out_b'], lng(0, 1), lnb(0, 1))
    x = _xattn_postnorm(x.reshape(bsz, t, d), mem_k, mem_v, p['xa_q'], p['xa_o'], 0, lng(0, 2), lnb(0, 2))
    x = ffn(flat(x), 0, 1, 3)

    x = ffn(x, 1, 0, 0)
    x3 = x.reshape(bsz, t, d)
    r, lw, k2, v, kk, b, gate, bonus = _rwkv_proj(
        x3, shift, p['mu'], p['w_rkv'], p['w0'], p['w1'], p['w2'], p['a0'], p['a1'], p['a2'], p['g1'], p['g2'],
        p['k_k'], p['k_a'], p['r_k'], p['sel'], p['selt'])
    o, s_wkv_new = _wkv_scan(r, lw, k2, v, kk, b, s_wkv, chunk)
    new_shift = x3[:, t - 1:, :]
    x = _wkv_out_postnorm(x, flat(o), flat(bonus), flat(gate), p['gn_g'], p['gn_b'], p['sel'], p['selt'],
                          p['w_out_c'], lng(1, 1), lnb(1, 1))
    x = _xattn_postnorm(x.reshape(bsz, t, d), mem_k, mem_v, p['xa_q'], p['xa_o'], 1, lng(1, 2), lnb(1, 2))
    x = ffn(flat(x), 1, 1, 3)
    new_states = ((new_buf, h_last.reshape(bsz, LRU_WIDTH), s_ret_new), (new_shift, s_wkv_new))
    return x.reshape(bsz, t, d), new_states


def kernel(x_prompt, x_sample, mem_prompt, state_conv0, state_lru0, state_ret0, state_shift1, state_wkv1,
           cache_mem_k, cache_mem_v, ln_g, ln_b, ffn_up, ffn_down, xa_q, xa_k, xa_v, xa_o,
           l0_w_in, l0_conv_w, l0_conv_b, l0_lru_wa, l0_lru_ba, l0_lru_wx, l0_lru_bx, l0_lru_lambda,
           l0_ret_gn_g, l0_ret_gn_b, l0_w_out, l1_mu, l1_w_rkv, l1_w0, l1_w1, l1_w2, l1_a0, l1_a1, l1_a2,
           l1_g1, l1_g2, l1_k_k, l1_k_a, l1_r_k, l1_gn_g, l1_gn_b, l1_w_out):
    d = D_MODEL
    p = _prep_weights(ln_g, ln_b, ffn_up, ffn_down, xa_q, xa_o, l0_w_in, l0_conv_w, l0_conv_b, l0_lru_wa,
                      l0_lru_ba, l0_lru_wx, l0_lru_bx, l0_lru_lambda, l0_ret_gn_g, l0_ret_gn_b, l0_w_out,
                      l1_mu, l1_w_rkv, l1_w0, l1_w1, l1_w2, l1_a0, l1_a1, l1_a2, l1_g1, l1_g2, l1_k_k, l1_k_a,
                      l1_r_k, l1_gn_g, l1_gn_b, l1_w_out)

    bp, tp, _ = x_prompt.shape
    dt = x_prompt.dtype
    mem_k_p = _mem_proj(mem_prompt, xa_k.astype(BF16))
    mem_v_p = _mem_proj(mem_prompt, xa_v.astype(BF16))
    zero_states = ((jnp.zeros((bp, CONV_WIDTH - 1, LRU_WIDTH), dt), jnp.zeros((bp, LRU_WIDTH), dt),
                    jnp.zeros((bp, RET_HEADS, RET_HEAD_DIM, RET_HEAD_DIM), dt)),
                   (jnp.zeros((bp, 1, d), dt), jnp.zeros((bp, RWKV_HEADS, RWKV_HEAD, RWKV_HEAD), dt)))
    y_prompt, st_p = _run_trunk(x_prompt, jnp.arange(tp, dtype=jnp.int32), mem_k_p, mem_v_p, zero_states, p)

    bs, ts, _ = x_sample.shape
    pos_s = PAST_LEN + jnp.arange(ts, dtype=jnp.int32)
    sample_states = ((state_conv0, state_lru0, state_ret0), (state_shift1, state_wkv1))
    y_sample, st_s = _run_trunk(x_sample, pos_s, cache_mem_k, cache_mem_v, sample_states, p)

    (p_conv0, p_lru0, p_ret0), (p_shift1, p_wkv1) = st_p
    (s_conv0, s_lru0, s_ret0), (s_shift1, s_wkv1) = st_s
    return (y_prompt, y_sample, mem_k_p, mem_v_p,
            p_conv0, p_lru0, p_ret0, p_shift1, p_wkv1, s_conv0, s_lru0, s_ret0, s_shift1, s_wkv1)
```

```python
import functools
import math

import jax
import jax.numpy as jnp
from jax import lax
from jax.experimental import pallas as pl
from jax.experimental.pallas import tpu as pltpu

D_MODEL = 1024
DEPTH = 2
PAST_LEN = 4096
CHUNK = 64
N_MEM = 256
MEM_HEADS = 4
MEM_HEAD_DIM = D_MODEL // MEM_HEADS
D_FF = 2816
LRU_WIDTH = 512
LRU_BLOCKS = 8
LRU_BLOCK = LRU_WIDTH // LRU_BLOCKS
CONV_WIDTH = 4
LRU_C = 8.0
RET_HEADS = 4
RET_WIDTH = 512
RET_HEAD_DIM = RET_WIDTH // RET_HEADS
ROPE_BASE = 10000.0
RWKV_HEAD = 64
RWKV_HEADS = D_MODEL // RWKV_HEAD
LN_EPS = 1e-5
RWKV_GN_EPS = 64e-5
ALPHA = (2 * DEPTH) ** 0.25

F32 = jnp.float32
BF16 = jnp.bfloat16

V7X_SUBLANES = 8
V7X_LANES = 128
VMEM_LIMIT_BYTES = 56 * 1024 * 1024
FFN_CHUNK = 256
ROW_TILE = 1024
FFN_ROW_TILE = 1024
RWKV_PROJ_ROW_TILE = 512
SEL_WIDTH = V7X_LANES
WKV_GROUP = 4
WKV_CHUNKS_PER_STEP = 4
RET_CHUNKS_PER_STEP = 8

_NT = (((1,), (1,)), ((), ()))
_TN = (((0,), (0,)), ((), ()))


def _dot(a, b):
    return jnp.dot(a.astype(BF16), b.astype(BF16), preferred_element_type=F32)


def _dot_nt(a, b):
    return lax.dot_general(a.astype(BF16), b.astype(BF16), _NT, preferred_element_type=F32)


def _dot_tn(a, b):
    return lax.dot_general(a.astype(BF16), b.astype(BF16), _TN, preferred_element_type=F32)


def _head_bcast(vals, selt_ref):
    lane = lax.broadcasted_iota(jnp.int32, vals.shape, 1)
    vals = jnp.where(lane < RWKV_HEADS, vals, 0.0)
    hi = vals.astype(BF16).astype(F32)
    packed = hi + pltpu.roll(vals - hi, RWKV_HEADS, 1)
    return jnp.dot(packed.astype(BF16), selt_ref[...], preferred_element_type=F32)


def _seg_sum(z, sel_ref, selt_ref):
    sums = jnp.dot(z.astype(BF16), sel_ref[...], preferred_element_type=F32)
    return _head_bcast(sums, selt_ref)


def _cumsum_rows(x):
    rows = x.shape[0]
    row8 = lax.broadcasted_iota(jnp.int32, x.shape, 0) % V7X_SUBLANES
    s = 1
    while s < V7X_SUBLANES:
        x = jnp.where(row8 >= s, x + pltpu.roll(x, s, 0), x)
        s *= 2
    out = [x[:V7X_SUBLANES]]
    for j in range(1, rows // V7X_SUBLANES):
        out.append(x[j * V7X_SUBLANES:(j + 1) * V7X_SUBLANES] + out[-1][V7X_SUBLANES - 1:])
    return jnp.concatenate(out, axis=0)


def _layer_norm(y, g, b, eps):
    mu = jnp.mean(y, axis=-1, keepdims=True)
    yc = y - mu
    var = jnp.mean(yc * yc, axis=-1, keepdims=True)
    return yc * lax.rsqrt(var + eps) * g + b


def _sigmoid(x):
    return 1.0 / (1.0 + jnp.exp(-x))


def _softplus(x):
    return jnp.maximum(x, 0.0) + jnp.log1p(jnp.exp(-jnp.abs(x)))


def _gelu_tanh(x):
    return 0.5 * x * (1.0 + jnp.tanh(math.sqrt(2.0 / math.pi) * (x + 0.044715 * (x * x * x))))


def _params(sem):
    return pltpu.CompilerParams(dimension_semantics=sem, vmem_limit_bytes=VMEM_LIMIT_BYTES)


def _row_tile(n, pref=ROW_TILE):
    return pref if n % pref == 0 else n


def _ffn_kernel(x_ref, wup_ref, wdn_ref, g_ref, b_ref, o_ref):
    x = x_ref[...]
    xb = x.astype(BF16)
    acc = jnp.zeros(x.shape, F32)
    for lo in range(0, D_FF, FFN_CHUNK):
        hg = jnp.dot(xb, wup_ref[:, lo:lo + FFN_CHUNK], preferred_element_type=F32)
        hu = jnp.dot(xb, wup_ref[:, D_FF + lo:D_FF + lo + FFN_CHUNK], preferred_element_type=F32)
        h = hg * _sigmoid(hg) * hu
        acc = acc + jnp.dot(h.astype(BF16), wdn_ref[lo:lo + FFN_CHUNK, :], preferred_element_type=F32)
    o_ref[...] = _layer_norm(ALPHA * x + 0.5 * acc, g_ref[...], b_ref[...], LN_EPS)


def _ffn_postnorm(x, w_up, w_down, layer, which, g, b):
    n, d = x.shape
    tm = _row_tile(n, FFN_ROW_TILE)
    pick = lambda i: (layer, which, 0, 0)
    return pl.pallas_call(
        _ffn_kernel,
        grid=(n // tm,),
        in_specs=[
            pl.BlockSpec((tm, d), lambda i: (i, 0)),
            pl.BlockSpec((None, None, d, 2 * D_FF), pick, pipeline_mode=pl.Buffered(1)),
            pl.BlockSpec((None, None, D_FF, d), pick, pipeline_mode=pl.Buffered(1)),
            pl.BlockSpec((1, d), lambda i: (0, 0)),
            pl.BlockSpec((1, d), lambda i: (0, 0)),
        ],
        out_specs=pl.BlockSpec((tm, d), lambda i: (i, 0)),
        out_shape=jax.ShapeDtypeStruct((n, d), F32),
        compiler_params=_params(("parallel",)),
        name="ffn_postnorm",
    )(x, w_up, w_down, g, b)


def _mm_kernel(x_ref, w_ref, o_ref):
    o_ref[...] = jnp.dot(x_ref[...].astype(BF16), w_ref[...], preferred_element_type=F32).astype(o_ref.dtype)


def _matmul(x, w, out_dtype):
    n, k = x.shape
    m = w.shape[1]
    tm = _row_tile(n)
    return pl.pallas_call(
        _mm_kernel,
        grid=(n // tm,),
        in_specs=[pl.BlockSpec((tm, k), lambda i: (i, 0)), pl.BlockSpec((k, m), lambda i: (0, 0))],
        out_specs=pl.BlockSpec((tm, m), lambda i: (i, 0)),
        out_shape=jax.ShapeDtypeStruct((n, m), out_dtype),
        compiler_params=_params(("parallel",)),
        name="proj",
    )(x, w)


def _mem_proj_kernel(x_ref, w_ref, o_ref, oh_ref):
    y = jnp.dot(x_ref[...].astype(BF16), w_ref[...], preferred_element_type=F32)
    for h in range(MEM_HEADS):
        yh = y[:, h * MEM_HEAD_DIM:(h + 1) * MEM_HEAD_DIM]
        o_ref[:, h, :] = yh
        oh_ref[h] = yh.astype(BF16)


def _mem_proj(mem, w):
    bsz, m, d = mem.shape
    depth = w.shape[0]
    return pl.pallas_call(
        _mem_proj_kernel,
        grid=(depth, bsz),
        in_specs=[pl.BlockSpec((None, m, d), lambda l, bi: (bi, 0, 0)),
                  pl.BlockSpec((None, d, d), lambda l, bi: (l, 0, 0))],
        out_specs=[pl.BlockSpec((None, None, m, MEM_HEADS, MEM_HEAD_DIM), lambda l, bi: (l, bi, 0, 0, 0)),
                   pl.BlockSpec((None, None, MEM_HEADS, m, MEM_HEAD_DIM), lambda l, bi: (l, bi, 0, 0, 0))],
        out_shape=[jax.ShapeDtypeStruct((depth, bsz, m, MEM_HEADS, MEM_HEAD_DIM), F32),
                   jax.ShapeDtypeStruct((depth, bsz, MEM_HEADS, m, MEM_HEAD_DIM), BF16)],
        compiler_params=_params(("parallel", "parallel")),
        name="mem_proj",
    )(mem, w)


def _ab_out_kernel(x_ref, ya_ref, yb_ref, wa_ref, wb_ref, g_ref, b_ref, o_ref):
    y = (jnp.dot(ya_ref[...].astype(BF16), wa_ref[...], preferred_element_type=F32)
         + jnp.dot(yb_ref[...].astype(BF16), wb_ref[...], preferred_element_type=F32))
    o_ref[...] = _layer_norm(ALPHA * x_ref[...] + y, g_ref[...], b_ref[...], LN_EPS)


def _ab_out_postnorm(x, ya, yb, wa, wb, g, b):
    n, d = x.shape
    ka = ya.shape[1]
    kb = yb.shape[1]
    tm = _row_tile(n)
    row = lambda i: (i, 0)
    const = lambda i: (0, 0)
    return pl.pallas_call(
        _ab_out_kernel,
        grid=(n // tm,),
        in_specs=[pl.BlockSpec((tm, d), row), pl.BlockSpec((tm, ka), row), pl.BlockSpec((tm, kb), row),
                  pl.BlockSpec((ka, d), const), pl.BlockSpec((kb, d), const),
                  pl.BlockSpec((1, d), const), pl.BlockSpec((1, d), const)],
        out_specs=pl.BlockSpec((tm, d), row),
        out_shape=jax.ShapeDtypeStruct((n, d), F32),
        compiler_params=_params(("parallel",)),
        name="ab_out_postnorm",
    )(x, ya, yb, wa, wb, g, b)


def _xattn_kernel(x_ref, k_ref, v_ref, wq_ref, wo_ref, g_ref, b_ref, o_ref):
    x = x_ref[0]
    q = jnp.dot(x.astype(BF16), wq_ref[...], preferred_element_type=F32)
    qb = q.astype(BF16)
    outs = []
    for h in range(MEM_HEADS):
        sl = slice(h * MEM_HEAD_DIM, (h + 1) * MEM_HEAD_DIM)
        s = lax.dot_general(qb[:, sl], k_ref[h], _NT, preferred_element_type=F32)
        m = jnp.max(s, axis=-1, keepdims=True)
        p = jnp.exp(s - m)
        p = p * (1.0 / jnp.sum(p, axis=-1, keepdims=True))
        outs.append(jnp.dot(p.astype(BF16), v_ref[h], preferred_element_type=F32))
    o = jnp.concatenate(outs, axis=-1)
    y = jnp.dot(o.astype(BF16), wo_ref[...], preferred_element_type=F32)
    o_ref[0] = _layer_norm(ALPHA * x + y, g_ref[...], b_ref[...], LN_EPS)


def _xattn_postnorm(x, mem_k, mem_v, wq, wo, layer, g, b):
    bsz, t, d = x.shape
    tm = _row_tile(t)
    const = lambda bi, ti: (0, 0)
    mem_spec = pl.BlockSpec((None, None, MEM_HEADS, N_MEM, MEM_HEAD_DIM), lambda bi, ti: (layer, bi, 0, 0, 0))
    w_spec = pl.BlockSpec((None, d, d), lambda bi, ti: (layer, 0, 0))
    return pl.pallas_call(
        _xattn_kernel,
        grid=(bsz, t // tm),
        in_specs=[pl.BlockSpec((1, tm, d), lambda bi, ti: (bi, ti, 0)),
                  mem_spec, mem_spec, w_spec, w_spec,
                  pl.BlockSpec((1, d), const), pl.BlockSpec((1, d), const)],
        out_specs=pl.BlockSpec((1, tm, d), lambda bi, ti: (bi, ti, 0)),
        out_shape=jax.ShapeDtypeStruct((bsz, t, d), F32),
        compiler_params=_params(("parallel", "parallel")),
        name="xattn_postnorm",
    )(x, mem_k, mem_v, wq, wo, g, b)


def _lru_kernel(xa_ref, ga_ref, cbuf_ref, h0_ref, cw_ref, cb_ref, wa_ref, ba_ref, wx_ref, bx_ref, lam_ref,
                ya_ref, nbuf_ref, hlast_ref, prev_s, h_s, a_s, u_s, *, tc):
    t = pl.program_id(1)

    @pl.when(t == 0)
    def _():
        prev_s[...] = jnp.zeros(prev_s.shape, F32)
        prev_s[V7X_SUBLANES - (CONV_WIDTH - 1):, :] = cbuf_ref[0]
        h_s[...] = h0_ref[0]

    xa = xa_ref[0].astype(F32)
    ext = jnp.concatenate([prev_s[...], xa], axis=0)
    xc = cb_ref[...] + xa * cw_ref[CONV_WIDTH - 1:CONV_WIDTH, :]
    for s in range(1, CONV_WIDTH):
        shifted = pltpu.roll(ext, s, 0)[V7X_SUBLANES:, :]
        xc = xc + shifted * cw_ref[CONV_WIDTH - 1 - s:CONV_WIDTH - s, :]
    prev_s[...] = xa[tc - V7X_SUBLANES:, :]
    nbuf_ref[0] = xa[tc - (CONV_WIDTH - 1):, :]

    xcb = xc.astype(BF16)
    r = _sigmoid(jnp.dot(xcb, wa_ref[...], preferred_element_type=F32) + ba_ref[...])
    i = _sigmoid(jnp.dot(xcb, wx_ref[...], preferred_element_type=F32) + bx_ref[...])
    log_a = (-LRU_C * _softplus(-lam_ref[...])) * r
    a = jnp.exp(log_a)
    th = jnp.tanh(log_a)
    u = jnp.sqrt(-2.0 * th / (1.0 - th)) * (i * xc)

    row8 = lax.broadcasted_iota(jnp.int32, a.shape, 0) % V7X_SUBLANES
    s = 1
    while s < V7X_SUBLANES:
        inside = row8 >= s
        u = jnp.where(inside, u + a * pltpu.roll(u, s, 0), u)
        a = jnp.where(inside, a * pltpu.roll(a, s, 0), a)
        s *= 2
    a_s[...] = a
    u_s[...] = u

    def body(j, h):
        base = pl.multiple_of(j * V7X_SUBLANES, V7X_SUBLANES)
        hb = u_s[pl.ds(base, V7X_SUBLANES), :] + a_s[pl.ds(base, V7X_SUBLANES), :] * h
        u_s[pl.ds(base, V7X_SUBLANES), :] = hb
        return hb[V7X_SUBLANES - 1:, :]

    h = lax.fori_loop(0, tc // V7X_SUBLANES, body, h_s[...], unroll=4)
    h_s[...] = h
    hlast_ref[0] = h
    ya_ref[0] = u_s[...] * _gelu_tanh(ga_ref[0].astype(F32))


def _lru_branch(proj, conv_buf, h0, cw, cb, wa, ba, wx, bx, lam):
    bsz, t, _ = proj.shape
    w = LRU_WIDTH
    tc = _row_tile(t)
    const = lambda bi, ti: (0, 0)
    per_b = lambda bi, ti: (bi, 0, 0)
    return pl.pallas_call(
        functools.partial(_lru_kernel, tc=tc),
        grid=(bsz, t // tc),
        in_specs=[pl.BlockSpec((1, tc, w), lambda bi, ti: (bi, ti, 0)),
                  pl.BlockSpec((1, tc, w), lambda bi, ti: (bi, ti, 1)),
                  pl.BlockSpec((1, CONV_WIDTH - 1, w), per_b),
                  pl.BlockSpec((1, 1, w), per_b),
                  pl.BlockSpec((CONV_WIDTH, w), const), pl.BlockSpec((1, w), const),
                  pl.BlockSpec((w, w), const), pl.BlockSpec((1, w), const),
                  pl.BlockSpec((w, w), const), pl.BlockSpec((1, w), const),
                  pl.BlockSpec((1, w), const)],
        out_specs=[pl.BlockSpec((1, tc, w), lambda bi, ti: (bi, ti, 0)),
                   pl.BlockSpec((1, CONV_WIDTH - 1, w), per_b),
                   pl.BlockSpec((1, 1, w), per_b)],
        out_shape=[jax.ShapeDtypeStruct((bsz, t, w), F32),
                   jax.ShapeDtypeStruct((bsz, CONV_WIDTH - 1, w), F32),
                   jax.ShapeDtypeStruct((bsz, 1, w), F32)],
        scratch_shapes=[pltpu.VMEM((V7X_SUBLANES, w), F32), pltpu.VMEM((1, w), F32),
                        pltpu.VMEM((tc, w), F32), pltpu.VMEM((tc, w), F32)],
        compiler_params=_params(("parallel", "arbitrary")),
        name="lru_branch",
    )(proj, proj, conv_buf, h0, cw, cb, wa, ba, wx, bx, lam)


def _ret_kernel(q_ref, k_ref, v_ref, g_ref, cos_ref, sin_ref, s0_ref, gng_ref, gnb_ref,
                yb_ref, slast_ref, s_s, *, c, nch):
    t = pl.program_id(1)

    @pl.when(t == 0)
    def _():
        s_s[...] = s0_ref[0]

    half = RET_HEAD_DIM // 2
    ri = lax.broadcasted_iota(jnp.int32, (c, c), 0)
    ci = lax.broadcasted_iota(jnp.int32, (c, c), 1)
    diff = (ri - ci).astype(F32)
    it = lax.broadcasted_iota(jnp.int32, (c, 1), 0).astype(F32)
    log_g = [math.log1p(-(2.0 ** (-5.0 - h))) for h in range(RET_HEADS)]
    dmask = [jnp.where(diff >= 0, jnp.exp(lg * jnp.maximum(diff, 0.0)), 0.0) for lg in log_g]
    xi = [jnp.exp(lg * (it + 1.0)) for lg in log_g]
    zeta = [jnp.exp(lg * (c - 1.0 - it)) for lg in log_g]

    units = [(ch, h) for ch in range(nch) for h in range(RET_HEADS)]
    qr, inner, kv = {}, {}, {}
    for ch, h in units:
        rows = slice(ch * c, (ch + 1) * c)
        sl = slice(h * RET_HEAD_DIM, (h + 1) * RET_HEAD_DIM)
        cos2 = cos_ref[rows, :]
        sin2 = sin_ref[rows, :]
        qh = q_ref[0, rows, sl].astype(F32)
        kh = k_ref[0, rows, sl].astype(F32)
        vh = v_ref[0, rows, sl].astype(BF16)
        qr[ch, h] = (qh * cos2 + pltpu.roll(qh, half, 1) * sin2).astype(BF16)
        kr = (kh * cos2 + pltpu.roll(kh, half, 1) * sin2) * (RET_HEAD_DIM ** -0.5)
        scores = _dot_nt(qr[ch, h], kr) * dmask[h]
        inner[ch, h] = _dot(scores, vh)
        kv[ch, h] = _dot_tn(kr * zeta[h], vh)
    for h in range(RET_HEADS):
        sl = slice(h * RET_HEAD_DIM, (h + 1) * RET_HEAD_DIM)
        s = s_s[h]
        outs = []
        for ch in range(nch):
            outs.append(inner[ch, h] + _dot(qr[ch, h], s) * xi[h])
            s = math.exp(log_g[h] * c) * s + kv[ch, h]
        s_s[h] = s
        on = _layer_norm(jnp.concatenate(outs, axis=0), gng_ref[:, sl], gnb_ref[:, sl], LN_EPS)
        gh = g_ref[0, :, sl].astype(F32)
        yb_ref[0, :, sl] = on * (gh * _sigmoid(gh))
    slast_ref[0] = s_s[...]


def _ret_branch(proj, cos2, sin2, s0, gn_g, gn_b, c):
    bsz, t, _ = proj.shape
    w = RET_WIDTH
    dk = RET_HEAD_DIM
    nch = RET_CHUNKS_PER_STEP if t % (RET_CHUNKS_PER_STEP * c) == 0 else 1
    const = lambda bi, ti: (0, 0)
    per_b = lambda bi, ti: (bi, 0, 0, 0)
    col = lambda j: (lambda bi, ti: (bi, ti, j))
    tc = nch * c
    return pl.pallas_call(
        functools.partial(_ret_kernel, c=c, nch=nch),
        grid=(bsz, t // tc),
        in_specs=[pl.BlockSpec((1, tc, w), col(2)), pl.BlockSpec((1, tc, w), col(3)),
                  pl.BlockSpec((1, tc, w), col(4)), pl.BlockSpec((1, tc, w), col(5)),
                  pl.BlockSpec((tc, dk), lambda bi, ti: (ti, 0)), pl.BlockSpec((tc, dk), lambda bi, ti: (ti, 0)),
                  pl.BlockSpec((1, RET_HEADS, dk, dk), per_b),
                  pl.BlockSpec((1, w), const), pl.BlockSpec((1, w), const)],
        out_specs=[pl.BlockSpec((1, tc, w), lambda bi, ti: (bi, ti, 0)),
                   pl.BlockSpec((1, RET_HEADS, dk, dk), per_b)],
        out_shape=[jax.ShapeDtypeStruct((bsz, t, w), F32),
                   jax.ShapeDtypeStruct((bsz, RET_HEADS, dk, dk), F32)],
        scratch_shapes=[pltpu.VMEM((RET_HEADS, dk, dk), F32)],
        compiler_params=_params(("parallel", "arbitrary")),
        name="ret_branch",
    )(proj, proj, proj, proj, cos2, sin2, s0, gn_g, gn_b)


def _rwkv_proj_kernel(x_ref, sh_ref, mu_ref, wrkv_ref, w0_ref, w1_ref, w2_ref, a0_ref, a1_ref, a2_ref,
                      g1_ref, g2_ref, kk_ref, ka_ref, rk_ref, sel_ref, selt_ref,
                      r_o, lw_o, k_o, v_o, kk_o, b_o, g_o, bonus_o, prev_s, *, tm):
    t = pl.program_id(1)

    @pl.when(t == 0)
    def _():
        prev_s[...] = jnp.zeros(prev_s.shape, F32)
        prev_s[V7X_SUBLANES - 1:, :] = sh_ref[0]

    x = x_ref[0]
    ext = jnp.concatenate([prev_s[...], x], axis=0)
    xp = pltpu.roll(ext, 1, 0)[V7X_SUBLANES:, :]
    prev_s[...] = x[tm - V7X_SUBLANES:, :]
    dx = xp - x

    def mix(p):
        return (x + dx * mu_ref[p:p + 1, :]).astype(BF16)

    r = jnp.dot(mix(0), wrkv_ref[0], preferred_element_type=F32)
    k = jnp.dot(mix(1), wrkv_ref[1], preferred_element_type=F32)
    v = jnp.dot(mix(2), wrkv_ref[2], preferred_element_type=F32)
    wl = _dot(jnp.tanh(jnp.dot(mix(3), w1_ref[...], preferred_element_type=F32)), w2_ref[...])
    w = -_softplus(-(w0_ref[...] + wl)) - 0.5
    al = _dot(jnp.dot(mix(4), a1_ref[...], preferred_element_type=F32), a2_ref[...])
    iclr = _sigmoid(a0_ref[...] + al)
    gate = _dot(_sigmoid(jnp.dot(mix(5), g1_ref[...], preferred_element_type=F32)), g2_ref[...])

    kk = k * kk_ref[...]
    sq = jnp.dot((kk * kk).astype(BF16), sel_ref[...], preferred_element_type=F32)
    inv_norm = 1.0 / jnp.maximum(jnp.sqrt(sq), 1e-12)
    kk = kk * _head_bcast(inv_norm, selt_ref)
    k2 = k * (1.0 + (iclr - 1.0) * ka_ref[...])

    r_o[0] = r
    lw_o[0] = -jnp.exp(w)
    k_o[0] = k2
    v_o[0] = v.astype(BF16)
    kk_o[0] = kk
    b_o[0] = kk * iclr
    g_o[0] = gate
    bonus_o[0] = _seg_sum(r * k2 * rk_ref[...], sel_ref, selt_ref) * v


def _rwkv_proj(x, shift, mu, wrkv, w0, w1, w2, a0, a1, a2, g1, g2, k_k, k_a, r_k, sel, selt):
    bsz, t, d = x.shape
    tm = _row_tile(t, RWKV_PROJ_ROW_TILE)
    const2 = lambda bi, ti: (0, 0)
    const3 = lambda bi, ti: (0, 0, 0)
    full2 = lambda a: pl.BlockSpec(a.shape, const2)
    tile = pl.BlockSpec((1, tm, d), lambda bi, ti: (bi, ti, 0))
    out = jax.ShapeDtypeStruct((bsz, t, d), F32)
    return pl.pallas_call(
        functools.partial(_rwkv_proj_kernel, tm=tm),
        grid=(bsz, t // tm),
        in_specs=[tile, pl.BlockSpec((1, 1, d), lambda bi, ti: (bi, 0, 0)), full2(mu),
                  pl.BlockSpec(wrkv.shape, const3), full2(w0), full2(w1), full2(w2), full2(a0), full2(a1),
                  full2(a2), full2(g1), full2(g2), full2(k_k), full2(k_a), full2(r_k), full2(sel), full2(selt)],
        out_specs=[tile] * 8,
        out_shape=[out, out, out, jax.ShapeDtypeStruct((bsz, t, d), BF16), out, out, out, out],
        scratch_shapes=[pltpu.VMEM((V7X_SUBLANES, d), F32)],
        compiler_params=_params(("parallel", "arbitrary")),
        name="rwkv_proj",
    )(x, shift, mu, wrkv, w0, w1, w2, a0, a1, a2, g1, g2, k_k, k_a, r_k, sel, selt)


def _wkv_kernel(r_ref, lw_ref, k_ref, v_ref, kk_ref, b_ref, s0_ref, o_ref, slast_ref,
                s_s, o0_s, rhat_s, gmat_s, hmat_s, *, c, nch):
    t = pl.program_id(1)
    last = pl.num_programs(1) - 1
    n = RWKV_HEAD
    block_refs = (r_ref, lw_ref, k_ref, v_ref, kk_ref, b_ref, o_ref, s_s, o0_s, rhat_s, gmat_s, hmat_s)

    @pl.when(t == 0)
    def _():
        for h in range(RWKV_HEADS):
            s_s[:, h * n:(h + 1) * n] = s0_ref[0, h]
        o0_s[...] = jnp.zeros(o0_s.shape, o0_s.dtype)
        rhat_s[...] = jnp.zeros(rhat_s.shape, rhat_s.dtype)
        gmat_s[...] = jnp.zeros(gmat_s.shape, gmat_s.dtype)
        hmat_s[...] = jnp.zeros(hmat_s.shape, hmat_s.dtype)

    @pl.when(t < last)
    def _():
        _wkv_block(*block_refs, c=c, nch=nch, prepare=True)

    @pl.when(t == last)
    def _():
        _wkv_block(*block_refs, c=c, nch=nch, prepare=False)
        for h in range(RWKV_HEADS):
            slast_ref[0, h] = s_s[:, h * n:(h + 1) * n]


def _wkv_block(r_ref, lw_ref, k_ref, v_ref, kk_ref, b_ref, o_ref, s_s, o0_s, rhat_s, gmat_s, hmat_s, *,
               c, nch, prepare):
    t = pl.program_id(1)
    n = RWKV_HEAD
    g = WKV_GROUP
    gl = g * n
    gc = g * c
    ngroups = RWKV_HEADS // g

    iota = lambda shape, dim: lax.broadcasted_iota(jnp.int32, shape, dim)
    head_k = iota((1, gl), 1) // n
    head_j = iota((1, gc), 1) // c
    row_t = iota((c, gc), 0)
    col_j = iota((c, gc), 1) % c
    strict = row_t > col_j
    lower = row_t >= col_j
    hc = c // 2
    half_j = col_j // hc
    half_j = half_j[:1]
    blk_j = iota((1, gc), 1) // hc
    eye_h = jnp.where(iota((hc, gc), 0) == iota((hc, gc), 1) % hc, 1.0, 0.0).astype(F32)
    blk = (iota((gl, gl), 0) // n) == (iota((gl, gl), 1) // n)
    eye_l = iota((gl, gl), 0) == iota((gl, gl), 1)

    def bd(z, lane_head):
        zb = z.astype(BF16)
        return jnp.concatenate([jnp.where(lane_head == h, zb, jnp.zeros_like(zb)) for h in range(g)], axis=0)

    def bd_half(z):
        zb = z.astype(BF16)
        return jnp.concatenate([jnp.where(blk_j == q, zb, jnp.zeros_like(zb)) for q in range(2 * g)], axis=0)

    def mm(a, b):
        return jnp.dot(a.astype(BF16), b, preferred_element_type=F32)

    def mm_nt(a, b):
        return lax.dot_general(a.astype(BF16), b, _NT, preferred_element_type=F32)

    started = t > 0
    group_lanes = [slice(gi * gl, (gi + 1) * gl) for gi in range(ngroups)]
    state = [s_s[:, lanes] for lanes in group_lanes]

    def recurrence_step(ch):
        rows = slice(ch * c, (ch + 1) * c)
        for gi, lanes in enumerate(group_lanes):
            s = state[gi]
            o_ref[0, rows, lanes] = o0_s[rows, lanes] + mm_nt(rhat_s[rows, lanes], bd(s, head_k))
            s = mm(s, gmat_s[ch * ngroups + gi]) + hmat_s[ch, :, lanes]
            if ch == nch - 1:
                s_s[:, lanes] = jnp.where(started, s, s_s[:, lanes])
            state[gi] = s

    if not prepare:
        for ch in range(nch):
            recurrence_step(ch)
        return

    units = [(ch, gi) for ch in range(nch) for gi in range(ngroups)]
    u = {}
    for ch in range(nch):
        rows = slice(ch * c, (ch + 1) * c)
        lw = lw_ref[0, rows, :]
        cum = _cumsum_rows(lw)
        cum_last = cum[c - 1:c, :]
        e_neg = jnp.exp(-cum)
        e_end = jnp.exp(cum_last - cum)
        kk = kk_ref[0, rows, :]
        b = b_ref[0, rows, :]
        k = k_ref[0, rows, :]
        full = dict(a_t=kk * jnp.exp(cum - lw), b_t=b * e_neg, k_t=k * e_neg, r_t=r_ref[0, rows, :] * jnp.exp(cum),
                    b_e=b * e_end, k_e=k * e_end, d_end=jnp.exp(cum_last), v=v_ref[0, rows, :])
        for gi in range(ngroups):
            lanes = slice(gi * gl, (gi + 1) * gl)
            u[ch, gi] = {name: val[:, lanes] for name, val in full.items()}

    for key in units:
        d = u[key]
        x2 = jnp.concatenate([d['a_t'], d['r_t']], axis=0).astype(BF16)
        mb = mm_nt(x2, bd(d['b_t'], head_k))
        mk = mm_nt(x2, bd(d['k_t'], head_k))
        d['p_rb'] = jnp.where(lower, mb[c:], 0.0).astype(BF16)
        mk_lo = jnp.concatenate([jnp.where(strict, mk[:c], 0.0), jnp.where(lower, mk[c:], 0.0)], axis=0)
        wo = mm(mk_lo, bd(d['v'], head_k))
        d['w_h'] = wo[:c]
        d['o0'] = wo[c:]
        nmat = -jnp.where(strict, mb[:c], 0.0)
        d['n_low'] = jnp.where(jnp.logical_and(row_t >= hc, half_j == 0), nmat, 0.0)
        d['pw'] = jnp.where(half_j == 0, nmat[:hc], nmat[hc:])
        d['tinv'] = eye_h + d['pw']
        if key[1] == ngroups - 1:
            recurrence_step(key[0])
    nsq = int(math.log2(hc)) - 1
    for level in range(nsq):
        for key in units:
            d = u[key]
            pwb = d['pw'].astype(BF16)
            w = bd_half(pwb)
            if level == 0:
                d['pw'] = mm(pwb, w)
            else:
                res = mm(jnp.concatenate([pwb, d['tinv'].astype(BF16)], axis=0), w)
                d['pw'] = res[:hc]
                d['tinv'] = d['tinv'] + res[hc:]
    for key in units:
        d = u[key]
        d['tinv'] = d['tinv'] + mm(d['tinv'], bd_half(d['pw']))
    for key in units:
        d = u[key]
        t_lo = jnp.where(half_j == 1, d['tinv'], 0.0)
        t_diag = jnp.concatenate([jnp.where(half_j == 0, d['tinv'], 0.0), t_lo], axis=0)
        corr = mm(mm(t_lo, bd(d['n_low'], head_j)), bd(t_diag, head_j))
        d['tinv'] = t_diag + jnp.concatenate([jnp.zeros((hc, gc), F32), corr], axis=0)
    for key in units:
        d = u[key]
        tb = d['tinv'].astype(BF16)
        d['a_hat'] = mm(tb, bd(d['a_t'], head_k))
        d['w_hat'] = mm(tb, bd(d['w_h'], head_k))
    for key in units:
        d = u[key]
        d['r_hat'] = d['r_t'] - mm(d['p_rb'], bd(d['a_hat'], head_k))
        d['o0'] = d['o0'] - mm(d['p_rb'], bd(d['w_hat'], head_k))
        b_e = d['b_e'].astype(BF16)
        gfull = lax.dot_general(d['a_hat'].astype(BF16), b_e, _TN, preferred_element_type=F32)
        d['gmat'] = (jnp.where(eye_l, d['d_end'], 0.0) - jnp.where(blk, gfull, 0.0)).astype(BF16)
        vw = jnp.concatenate([d['v'].astype(BF16), d['w_hat'].astype(BF16)], axis=0)
        kb = jnp.concatenate([d['k_e'].astype(BF16), -b_e], axis=0)
        zf = lax.dot_general(vw, kb, _TN, preferred_element_type=F32)
        hmat = jnp.where(head_k == 0, zf[:n], 0.0)
        for h in range(1, g):
            hmat = hmat + jnp.where(head_k == h, zf[h * n:(h + 1) * n], 0.0)
        d['hmat'] = hmat

    for (ch, gi) in units:
        d = u[ch, gi]
        lanes = slice(gi * gl, (gi + 1) * gl)
        o0_s[ch * c:(ch + 1) * c, lanes] = d['o0']
        rhat_s[ch * c:(ch + 1) * c, lanes] = d['r_hat'].astype(BF16)
        gmat_s[ch * ngroups + gi] = d['gmat']
        hmat_s[ch, :, lanes] = d['hmat']


def _wkv_scan(r, lw, k, v, kk, b, s0, c):
    bsz, t, d = r.shape
    n = RWKV_HEAD
    nch = WKV_CHUNKS_PER_STEP if t % (WKV_CHUNKS_PER_STEP * c) == 0 else 1
    nsteps = t // (nch * c)
    g = WKV_GROUP
    tile_in = pl.BlockSpec((1, nch * c, d), lambda bi, ti: (bi, jnp.minimum(ti, nsteps - 1), 0))
    tile_out = pl.BlockSpec((1, nch * c, d), lambda bi, ti: (bi, jnp.maximum(ti - 1, 0), 0))
    st = pl.BlockSpec((1, RWKV_HEADS, n, n), lambda bi, ti: (bi, 0, 0, 0))
    return pl.pallas_call(
        functools.partial(_wkv_kernel, c=c, nch=nch),
        grid=(bsz, nsteps + 1),
        in_specs=[tile_in] * 6 + [st],
        out_specs=[tile_out, st],
        out_shape=[jax.ShapeDtypeStruct((bsz, t, d), F32), jax.ShapeDtypeStruct(s0.shape, F32)],
        scratch_shapes=[pltpu.VMEM((n, d), F32), pltpu.VMEM((nch * c, d), F32), pltpu.VMEM((nch * c, d), BF16),
                        pltpu.VMEM((nch * (RWKV_HEADS // g), g * n, g * n), BF16), pltpu.VMEM((nch, n, d), F32)],
        compiler_params=_params(("parallel", "arbitrary")),
        name="wkv_scan",
    )(r, lw, k, v, kk, b, s0)


def _wkv_out_kernel(x_ref, o_ref, bonus_ref, gate_ref, gng_ref, gnb_ref, sel_ref, selt_ref, w_ref,
                    g_ref, b_ref, y_ref):
    o = o_ref[...]
    inv_n = 1.0 / RWKV_HEAD
    mu = _seg_sum(o, sel_ref, selt_ref) * inv_n
    oc = o - mu
    var = _seg_sum(oc * oc, sel_ref, selt_ref) * inv_n
    on = oc * lax.rsqrt(var + RWKV_GN_EPS) * gng_ref[...] + gnb_ref[...]
    z = (on + bonus_ref[...]) * gate_ref[...]
    y = jnp.dot(z.astype(BF16), w_ref[...], preferred_element_type=F32)
    y_ref[...] = _layer_norm(ALPHA * x_ref[...] + y, g_ref[...], b_ref[...], LN_EPS)


def _wkv_out_postnorm(x, o, bonus, gate, gn_g, gn_b, sel, selt, w, g, b):
    n, d = x.shape
    tm = _row_tile(n)
    row = pl.BlockSpec((tm, d), lambda i: (i, 0))
    const = lambda i: (0, 0)
    full = lambda a: pl.BlockSpec(a.shape, const)
    return pl.pallas_call(
        _wkv_out_kernel,
        grid=(n // tm,),
        in_specs=[row, row, row, row, full(gn_g), full(gn_b), full(sel), full(selt), full(w), full(g), full(b)],
        out_specs=row,
        out_shape=jax.ShapeDtypeStruct((n, d), F32),
        compiler_params=_params(("parallel",)),
        name="wkv_out_postnorm",
    )(x, o, bonus, gate, gn_g, gn_b, sel, selt, w, g, b)


def _prep_weights(ln_g, ln_b, ffn_up, ffn_down, xa_q, xa_o, l0_w_in, l0_conv_w, l0_conv_b, l0_lru_wa,
                  l0_lru_ba, l0_lru_wx, l0_lru_bx, l0_lru_lambda, l0_ret_gn_g, l0_ret_gn_b, l0_w_out,
                  l1_mu, l1_w_rkv, l1_w0, l1_w1, l1_w2, l1_a0, l1_a1, l1_a2, l1_g1, l1_g2, l1_k_k, l1_k_a,
                  l1_r_k, l1_gn_g, l1_gn_b, l1_w_out):
    d = D_MODEL
    row = lambda a: a.reshape(1, -1).astype(F32)

    def block_diag(w):
        eye = jnp.eye(LRU_BLOCKS, dtype=w.dtype)
        return jnp.einsum('gij,gh->gihj', w, eye).reshape(LRU_WIDTH, LRU_WIDTH).astype(BF16)

    head_of_col = jnp.arange(d) // RWKV_HEAD
    sel = (head_of_col[:, None] == jnp.arange(SEL_WIDTH)[None, :]).astype(BF16)
    sel_row = jnp.arange(SEL_WIDTH)
    selt = ((sel_row[:, None] % RWKV_HEADS == head_of_col[None, :])
            & (sel_row[:, None] < 2 * RWKV_HEADS)).astype(BF16)
    return dict(
        ln_g=ln_g, ln_b=ln_b, ffn_up=ffn_up.astype(BF16), ffn_down=ffn_down.astype(BF16),
        xa_q=(xa_q * (MEM_HEAD_DIM ** -0.5)).astype(BF16), xa_o=xa_o.astype(BF16),
        w_in=l0_w_in.astype(BF16), conv_w=l0_conv_w, conv_b=row(l0_conv_b),
        lru_wa=block_diag(l0_lru_wa), lru_ba=row(l0_lru_ba), lru_wx=block_diag(l0_lru_wx),
        lru_bx=row(l0_lru_bx), lru_lam=row(l0_lru_lambda),
        ret_gn_g=row(l0_ret_gn_g), ret_gn_b=row(l0_ret_gn_b),
        w_out_a=l0_w_out[:LRU_WIDTH].astype(BF16), w_out_b=l0_w_out[LRU_WIDTH:].astype(BF16),
        mu=l1_mu, w_rkv=l1_w_rkv.astype(BF16), w0=row(l1_w0), w1=l1_w1.astype(BF16), w2=l1_w2.astype(BF16),
        a0=row(l1_a0), a1=l1_a1.astype(BF16), a2=l1_a2.astype(BF16), g1=l1_g1.astype(BF16),
        g2=l1_g2.astype(BF16), k_k=row(l1_k_k), k_a=row(l1_k_a), r_k=row(l1_r_k),
        gn_g=row(l1_gn_g), gn_b=row(l1_gn_b), w_out_c=l1_w_out.astype(BF16), sel=sel, selt=selt,
    )


def _rotary_tables(pos):
    half = RET_HEAD_DIM // 2
    inv_freq = ROPE_BASE ** (-jnp.arange(half, dtype=F32) / half)
    ang = pos.astype(F32)[:, None] * inv_freq[None, :]
    cos = jnp.cos(ang)
    sin = jnp.sin(ang)
    return jnp.concatenate([cos, cos], axis=-1), jnp.concatenate([-sin, sin], axis=-1)


def _run_trunk(x, pos, mem_k, mem_v, states, p):
    bsz, t, d = x.shape
    n = bsz * t
    (conv_buf, h0, s_ret), (shift, s_wkv) = states
    lng = lambda l, j: p['ln_g'][l, j].reshape(1, d)
    lnb = lambda l, j: p['ln_b'][l, j].reshape(1, d)
    flat = lambda a: a.reshape(n, a.shape[-1])
    chunk = min(CHUNK, t)
    ffn = lambda xx, l, j, nj: _ffn_postnorm(xx, p['ffn_up'], p['ffn_down'], l, j, lng(l, nj), lnb(l, nj))

    x = ffn(flat(x), 0, 0, 0)
    proj = _matmul(x, p['w_in'], BF16).reshape(bsz, t, -1)
    ya, new_buf, h_last = _lru_branch(proj, conv_buf, h0.reshape(bsz, 1, LRU_WIDTH), p['conv_w'], p['conv_b'],
                                      p['lru_wa'], p['lru_ba'], p['lru_wx'], p['lru_bx'], p['lru_lam'])
    cos2, sin2 = _rotary_tables(pos)
    yb, s_ret_new = _ret_branch(proj, cos2, sin2, s_ret, p['ret_gn_g'], p['ret_gn_b'], chunk)
    x = _ab_out_postnorm(x, flat(ya), flat(yb), p['w_out_a'], p['w_out_b'], lng(0, 1), lnb(0, 1))
    x = _xattn_postnorm(x.reshape(bsz, t, d), mem_k, mem_v, p['xa_q'], p['xa_o'], 0, lng(0, 2), lnb(0, 2))
    x = ffn(flat(x), 0, 1, 3)

    x = ffn(x, 1, 0, 0)
    x3 = x.reshape(bsz, t, d)
    r, lw, k2, v, kk, b, gate, bonus = _rwkv_proj(
        x3, shift, p['mu'], p['w_rkv'], p['w0'], p['w1'], p['w2'], p['a0'], p['a1'], p['a2'], p['g1'], p['g2'],
        p['k_k'], p['k_a'], p['r_k'], p['sel'], p['selt'])
    o, s_wkv_new = _wkv_scan(r, lw, k2, v, kk, b, s_wkv, chunk)
    new_shift = x3[:, t - 1:, :]
    x = _wkv_out_postnorm(x, flat(o), flat(bonus), flat(gate), p['gn_g'], p['gn_b'], p['sel'], p['selt'],
                          p['w_out_c'], lng(1, 1), lnb(1, 1))
    x = _xattn_postnorm(x.reshape(bsz, t, d), mem_k, mem_v, p['xa_q'], p['xa_o'], 1, lng(1, 2), lnb(1, 2))
    x = ffn(flat(x), 1, 1, 3)
    new_states = ((new_buf, h_last.reshape(bsz, LRU_WIDTH), s_ret_new), (new_shift, s_wkv_new))
    return x.reshape(bsz, t, d), new_states


def kernel(x_prompt, x_sample, mem_prompt, state_conv0, state_lru0, state_ret0, state_shift1, state_wkv1,
           cache_mem_k, cache_mem_v, ln_g, ln_b, ffn_up, ffn_down, xa_q, xa_k, xa_v, xa_o,
           l0_w_in, l0_conv_w, l0_conv_b, l0_lru_wa, l0_lru_ba, l0_lru_wx, l0_lru_bx, l0_lru_lambda,
           l0_ret_gn_g, l0_ret_gn_b, l0_w_out, l1_mu, l1_w_rkv, l1_w0, l1_w1, l1_w2, l1_a0, l1_a1, l1_a2,
           l1_g1, l1_g2, l1_k_k, l1_k_a, l1_r_k, l1_gn_g, l1_gn_b, l1_w_out):
    d = D_MODEL
    p = _prep_weights(ln_g, ln_b, ffn_up, ffn_down, xa_q, xa_o, l0_w_in, l0_conv_w, l0_conv_b, l0_lru_wa,
                      l0_lru_ba, l0_lru_wx, l0_lru_bx, l0_lru_lambda, l0_ret_gn_g, l0_ret_gn_b, l0_w_out,
                      l1_mu, l1_w_rkv, l1_w0, l1_w1, l1_w2, l1_a0, l1_a1, l1_a2, l1_g1, l1_g2, l1_k_k, l1_k_a,
                      l1_r_k, l1_gn_g, l1_gn_b, l1_w_out)

    bp, tp, _ = x_prompt.shape
    dt = x_prompt.dtype
    mem_k_p, mem_kh_p = _mem_proj(mem_prompt, xa_k.astype(BF16))
    mem_v_p, mem_vh_p = _mem_proj(mem_prompt, xa_v.astype(BF16))
    zero_states = ((jnp.zeros((bp, CONV_WIDTH - 1, LRU_WIDTH), dt), jnp.zeros((bp, LRU_WIDTH), dt),
                    jnp.zeros((bp, RET_HEADS, RET_HEAD_DIM, RET_HEAD_DIM), dt)),
                   (jnp.zeros((bp, 1, d), dt), jnp.zeros((bp, RWKV_HEADS, RWKV_HEAD, RWKV_HEAD), dt)))
    y_prompt, st_p = _run_trunk(x_prompt, jnp.arange(tp, dtype=jnp.int32), mem_kh_p, mem_vh_p, zero_states, p)

    bs, ts, _ = x_sample.shape
    pos_s = PAST_LEN + jnp.arange(ts, dtype=jnp.int32)
    sample_states = ((state_conv0, state_lru0, state_ret0), (state_shift1, state_wkv1))
    head_major = lambda a: jnp.transpose(a, (0, 1, 3, 2, 4)).astype(BF16)
    y_sample, st_s = _run_trunk(x_sample, pos_s, head_major(cache_mem_k), head_major(cache_mem_v), sample_states, p)

    (p_conv0, p_lru0, p_ret0), (p_shift1, p_wkv1) = st_p
    (s_conv0, s_lru0, s_ret0), (s_shift1, s_wkv1) = st_s
    return (y_prompt, y_sample, mem_k_p, mem_v_p,
            p_conv0, p_lru0, p_ret0, p_shift1, p_wkv1, s_conv0, s_lru0, s_ret0, s_shift1, s_wkv1)
```

```python
import functools
import math

import jax
import jax.numpy as jnp
from jax import lax
from jax.experimental import pallas as pl
from jax.experimental.pallas import tpu as pltpu

D_MODEL = 1024
DEPTH = 2
PAST_LEN = 4096
CHUNK = 64
N_MEM = 256
MEM_HEADS = 4
MEM_HEAD_DIM = D_MODEL // MEM_HEADS
D_FF = 2816
LRU_WIDTH = 512
LRU_BLOCKS = 8
LRU_BLOCK = LRU_WIDTH // LRU_BLOCKS
CONV_WIDTH = 4
LRU_C = 8.0
RET_HEADS = 4
RET_WIDTH = 512
RET_HEAD_DIM = RET_WIDTH // RET_HEADS
ROPE_BASE = 10000.0
RWKV_HEAD = 64
RWKV_HEADS = D_MODEL // RWKV_HEAD
LN_EPS = 1e-5
RWKV_GN_EPS = 64e-5
ALPHA = (2 * DEPTH) ** 0.25

F32 = jnp.float32
BF16 = jnp.bfloat16

V7X_SUBLANES = 8
V7X_LANES = 128
VMEM_LIMIT_BYTES = 56 * 1024 * 1024
FFN_CHUNK = 256
ROW_TILE = 1024
FFN_ROW_TILE = 1024
RWKV_PROJ_ROW_TILE = 512
SEL_WIDTH = V7X_LANES
WKV_GROUP = 4
WKV_CHUNKS_PER_STEP = 4
RET_CHUNKS_PER_STEP = 8

_NT = (((1,), (1,)), ((), ()))
_TN = (((0,), (0,)), ((), ()))


def _dot(a, b):
    return jnp.dot(a.astype(BF16), b.astype(BF16), preferred_element_type=F32)


def _dot_nt(a, b):
    return lax.dot_general(a.astype(BF16), b.astype(BF16), _NT, preferred_element_type=F32)


def _dot_tn(a, b):
    return lax.dot_general(a.astype(BF16), b.astype(BF16), _TN, preferred_element_type=F32)


def _head_bcast(vals, selt_ref):
    lane = lax.broadcasted_iota(jnp.int32, vals.shape, 1)
    vals = jnp.where(lane < RWKV_HEADS, vals, 0.0)
    hi = vals.astype(BF16).astype(F32)
    packed = hi + pltpu.roll(vals - hi, RWKV_HEADS, 1)
    return jnp.dot(packed.astype(BF16), selt_ref[...], preferred_element_type=F32)


def _seg_sum(z, sel_ref, selt_ref):
    sums = jnp.dot(z.astype(BF16), sel_ref[...], preferred_element_type=F32)
    return _head_bcast(sums, selt_ref)


def _cumsum_rows(x):
    rows = x.shape[0]
    row8 = lax.broadcasted_iota(jnp.int32, x.shape, 0) % V7X_SUBLANES
    s = 1
    while s < V7X_SUBLANES:
        x = jnp.where(row8 >= s, x + pltpu.roll(x, s, 0), x)
        s *= 2
    out = [x[:V7X_SUBLANES]]
    for j in range(1, rows // V7X_SUBLANES):
        out.append(x[j * V7X_SUBLANES:(j + 1) * V7X_SUBLANES] + out[-1][V7X_SUBLANES - 1:])
    return jnp.concatenate(out, axis=0)


def _layer_norm(y, g, b, eps):
    mu = jnp.mean(y, axis=-1, keepdims=True)
    yc = y - mu
    var = jnp.mean(yc * yc, axis=-1, keepdims=True)
    return yc * lax.rsqrt(var + eps) * g + b


def _sigmoid(x):
    return 1.0 / (1.0 + jnp.exp(-x))


def _softplus(x):
    return jnp.maximum(x, 0.0) + jnp.log1p(jnp.exp(-jnp.abs(x)))


def _gelu_tanh(x):
    return 0.5 * x * (1.0 + jnp.tanh(math.sqrt(2.0 / math.pi) * (x + 0.044715 * (x * x * x))))


def _params(sem):
    return pltpu.CompilerParams(dimension_semantics=sem, vmem_limit_bytes=VMEM_LIMIT_BYTES)


def _row_tile(n, pref=ROW_TILE):
    return pref if n % pref == 0 else n


def _ffn_kernel(x_ref, wup_ref, wdn_ref, g_ref, b_ref, o_ref):
    x = x_ref[...]
    xb = x.astype(BF16)
    acc = jnp.zeros(x.shape, F32)
    for lo in range(0, D_FF, FFN_CHUNK):
        hg = jnp.dot(xb, wup_ref[:, lo:lo + FFN_CHUNK], preferred_element_type=F32)
        hu = jnp.dot(xb, wup_ref[:, D_FF + lo:D_FF + lo + FFN_CHUNK], preferred_element_type=F32)
        h = hg * _sigmoid(hg) * hu
        acc = acc + jnp.dot(h.astype(BF16), wdn_ref[lo:lo + FFN_CHUNK, :], preferred_element_type=F32)
    o_ref[...] = _layer_norm(ALPHA * x + 0.5 * acc, g_ref[...], b_ref[...], LN_EPS)


def _ffn_postnorm(x, w_up, w_down, layer, which, g, b):
    n, d = x.shape
    tm = _row_tile(n, FFN_ROW_TILE)
    pick = lambda i: (layer, which, 0, 0)
    return pl.pallas_call(
        _ffn_kernel,
        grid=(n // tm,),
        in_specs=[
            pl.BlockSpec((tm, d), lambda i: (i, 0)),
            pl.BlockSpec((None, None, d, 2 * D_FF), pick, pipeline_mode=pl.Buffered(1)),
            pl.BlockSpec((None, None, D_FF, d), pick, pipeline_mode=pl.Buffered(1)),
            pl.BlockSpec((1, d), lambda i: (0, 0)),
            pl.BlockSpec((1, d), lambda i: (0, 0)),
        ],
        out_specs=pl.BlockSpec((tm, d), lambda i: (i, 0)),
        out_shape=jax.ShapeDtypeStruct((n, d), F32),
        compiler_params=_params(("parallel",)),
        name="ffn_postnorm",
    )(x, w_up, w_down, g, b)


def _mm_kernel(x_ref, w_ref, o_ref):
    o_ref[...] = jnp.dot(x_ref[...].astype(BF16), w_ref[...], preferred_element_type=F32).astype(o_ref.dtype)


def _matmul(x, w, out_dtype):
    n, k = x.shape
    m = w.shape[1]
    tm = _row_tile(n)
    return pl.pallas_call(
        _mm_kernel,
        grid=(n // tm,),
        in_specs=[pl.BlockSpec((tm, k), lambda i: (i, 0)), pl.BlockSpec((k, m), lambda i: (0, 0))],
        out_specs=pl.BlockSpec((tm, m), lambda i: (i, 0)),
        out_shape=jax.ShapeDtypeStruct((n, m), out_dtype),
        compiler_params=_params(("parallel",)),
        name="proj",
    )(x, w)


def _mem_proj_kernel(x_ref, w_ref, o_ref, oh_ref):
    y = jnp.dot(x_ref[...].astype(BF16), w_ref[...], preferred_element_type=F32)
    for h in range(MEM_HEADS):
        yh = y[:, h * MEM_HEAD_DIM:(h + 1) * MEM_HEAD_DIM]
        o_ref[:, h, :] = yh
        oh_ref[h] = yh.astype(BF16)


def _mem_proj(mem, w):
    bsz, m, d = mem.shape
    depth = w.shape[0]
    return pl.pallas_call(
        _mem_proj_kernel,
        grid=(depth, bsz),
        in_specs=[pl.BlockSpec((None, m, d), lambda l, bi: (bi, 0, 0)),
                  pl.BlockSpec((None, d, d), lambda l, bi: (l, 0, 0))],
        out_specs=[pl.BlockSpec((None, None, m, MEM_HEADS, MEM_HEAD_DIM), lambda l, bi: (l, bi, 0, 0, 0)),
                   pl.BlockSpec((None, None, MEM_HEADS, m, MEM_HEAD_DIM), lambda l, bi: (l, bi, 0, 0, 0))],
        out_shape=[jax.ShapeDtypeStruct((depth, bsz, m, MEM_HEADS, MEM_HEAD_DIM), F32),
                   jax.ShapeDtypeStruct((depth, bsz, MEM_HEADS, m, MEM_HEAD_DIM), BF16)],
        compiler_params=_params(("parallel", "parallel")),
        name="mem_proj",
    )(mem, w)


def _ab_out_kernel(x_ref, ya_ref, yb_ref, wa_ref, wb_ref, g_ref, b_ref, o_ref):
    y = (jnp.dot(ya_ref[...].astype(BF16), wa_ref[...], preferred_element_type=F32)
         + jnp.dot(yb_ref[...].astype(BF16), wb_ref[...], preferred_element_type=F32))
    o_ref[...] = _layer_norm(ALPHA * x_ref[...] + y, g_ref[...], b_ref[...], LN_EPS)


def _ab_out_postnorm(x, ya, yb, wa, wb, g, b):
    n, d = x.shape
    ka = ya.shape[1]
    kb = yb.shape[1]
    tm = _row_tile(n)
    row = lambda i: (i, 0)
    const = lambda i: (0, 0)
    return pl.pallas_call(
        _ab_out_kernel,
        grid=(n // tm,),
        in_specs=[pl.BlockSpec((tm, d), row), pl.BlockSpec((tm, ka), row), pl.BlockSpec((tm, kb), row),
                  pl.BlockSpec((ka, d), const), pl.BlockSpec((kb, d), const),
                  pl.BlockSpec((1, d), const), pl.BlockSpec((1, d), const)],
        out_specs=pl.BlockSpec((tm, d), row),
        out_shape=jax.ShapeDtypeStruct((n, d), F32),
        compiler_params=_params(("parallel",)),
        name="ab_out_postnorm",
    )(x, ya, yb, wa, wb, g, b)


def _xattn_kernel(x_ref, k_ref, v_ref, wq_ref, wo_ref, g_ref, b_ref, o_ref):
    x = x_ref[0]
    q = jnp.dot(x.astype(BF16), wq_ref[...], preferred_element_type=F32)
    qb = q.astype(BF16)
    heads = range(MEM_HEADS)
    scores = [lax.dot_general(qb[:, h * MEM_HEAD_DIM:(h + 1) * MEM_HEAD_DIM], k_ref[h], _NT,
                              preferred_element_type=F32) for h in heads]
    probs = []
    for s in scores:
        p = jnp.exp(s - jnp.max(s, axis=-1, keepdims=True))
        probs.append((p * (1.0 / jnp.sum(p, axis=-1, keepdims=True))).astype(BF16))
    o = jnp.concatenate([jnp.dot(probs[h], v_ref[h], preferred_element_type=F32) for h in heads], axis=-1)
    y = jnp.dot(o.astype(BF16), wo_ref[...], preferred_element_type=F32)
    o_ref[0] = _layer_norm(ALPHA * x + y, g_ref[...], b_ref[...], LN_EPS)


def _xattn_postnorm(x, mem_k, mem_v, wq, wo, layer, g, b):
    bsz, t, d = x.shape
    tm = _row_tile(t)
    const = lambda bi, ti: (0, 0)
    mem_spec = pl.BlockSpec((None, None, MEM_HEADS, N_MEM, MEM_HEAD_DIM), lambda bi, ti: (layer, bi, 0, 0, 0))
    w_spec = pl.BlockSpec((None, d, d), lambda bi, ti: (layer, 0, 0))
    return pl.pallas_call(
        _xattn_kernel,
        grid=(bsz, t // tm),
        in_specs=[pl.BlockSpec((1, tm, d), lambda bi, ti: (bi, ti, 0)),
                  mem_spec, mem_spec, w_spec, w_spec,
                  pl.BlockSpec((1, d), const), pl.BlockSpec((1, d), const)],
        out_specs=pl.BlockSpec((1, tm, d), lambda bi, ti: (bi, ti, 0)),
        out_shape=jax.ShapeDtypeStruct((bsz, t, d), F32),
        compiler_params=_params(("parallel", "parallel")),
        name="xattn_postnorm",
    )(x, mem_k, mem_v, wq, wo, g, b)


def _lru_kernel(xa_ref, ga_ref, cbuf_ref, h0_ref, cw_ref, cb_ref, wa_ref, ba_ref, wx_ref, bx_ref, lam_ref,
                ya_ref, nbuf_ref, hlast_ref, prev_s, h_s, a_s, u_s, *, tc):
    t = pl.program_id(1)

    @pl.when(t == 0)
    def _():
        prev_s[...] = jnp.zeros(prev_s.shape, F32)
        prev_s[V7X_SUBLANES - (CONV_WIDTH - 1):, :] = cbuf_ref[0]
        h_s[...] = h0_ref[0]

    xa = xa_ref[0].astype(F32)
    ext = jnp.concatenate([prev_s[...], xa], axis=0)
    xc = cb_ref[...] + xa * cw_ref[CONV_WIDTH - 1:CONV_WIDTH, :]
    for s in range(1, CONV_WIDTH):
        shifted = pltpu.roll(ext, s, 0)[V7X_SUBLANES:, :]
        xc = xc + shifted * cw_ref[CONV_WIDTH - 1 - s:CONV_WIDTH - s, :]
    prev_s[...] = xa[tc - V7X_SUBLANES:, :]
    nbuf_ref[0] = xa[tc - (CONV_WIDTH - 1):, :]

    xcb = xc.astype(BF16)
    r = _sigmoid(jnp.dot(xcb, wa_ref[...], preferred_element_type=F32) + ba_ref[...])
    i = _sigmoid(jnp.dot(xcb, wx_ref[...], preferred_element_type=F32) + bx_ref[...])
    log_a = (-LRU_C * _softplus(-lam_ref[...])) * r
    a = jnp.exp(log_a)
    th = jnp.tanh(log_a)
    u = jnp.sqrt(-2.0 * th / (1.0 - th)) * (i * xc)

    row8 = lax.broadcasted_iota(jnp.int32, a.shape, 0) % V7X_SUBLANES
    s = 1
    while s < V7X_SUBLANES:
        inside = row8 >= s
        u = jnp.where(inside, u + a * pltpu.roll(u, s, 0), u)
        a = jnp.where(inside, a * pltpu.roll(a, s, 0), a)
        s *= 2
    a_s[...] = a
    u_s[...] = u

    def body(j, h):
        base = pl.multiple_of(j * V7X_SUBLANES, V7X_SUBLANES)
        hb = u_s[pl.ds(base, V7X_SUBLANES), :] + a_s[pl.ds(base, V7X_SUBLANES), :] * h
        u_s[pl.ds(base, V7X_SUBLANES), :] = hb
        return hb[V7X_SUBLANES - 1:, :]

    h = lax.fori_loop(0, tc // V7X_SUBLANES, body, h_s[...], unroll=4)
    h_s[...] = h
    hlast_ref[0] = h
    ya_ref[0] = u_s[...] * _gelu_tanh(ga_ref[0].astype(F32))


def _lru_branch(proj, conv_buf, h0, cw, cb, wa, ba, wx, bx, lam):
    bsz, t, _ = proj.shape
    w = LRU_WIDTH
    tc = _row_tile(t)
    const = lambda bi, ti: (0, 0)
    per_b = lambda bi, ti: (bi, 0, 0)
    return pl.pallas_call(
        functools.partial(_lru_kernel, tc=tc),
        grid=(bsz, t // tc),
        in_specs=[pl.BlockSpec((1, tc, w), lambda bi, ti: (bi, ti, 0)),
                  pl.BlockSpec((1, tc, w), lambda bi, ti: (bi, ti, 1)),
                  pl.BlockSpec((1, CONV_WIDTH - 1, w), per_b),
                  pl.BlockSpec((1, 1, w), per_b),
                  pl.BlockSpec((CONV_WIDTH, w), const), pl.BlockSpec((1, w), const),
                  pl.BlockSpec((w, w), const), pl.BlockSpec((1, w), const),
                  pl.BlockSpec((w, w), const), pl.BlockSpec((1, w), const),
                  pl.BlockSpec((1, w), const)],
        out_specs=[pl.BlockSpec((1, tc, w), lambda bi, ti: (bi, ti, 0)),
                   pl.BlockSpec((1, CONV_WIDTH - 1, w), per_b),
                   pl.BlockSpec((1, 1, w), per_b)],
        out_shape=[jax.ShapeDtypeStruct((bsz, t, w), F32),
                   jax.ShapeDtypeStruct((bsz, CONV_WIDTH - 1, w), F32),
                   jax.ShapeDtypeStruct((bsz, 1, w), F32)],
        scratch_shapes=[pltpu.VMEM((V7X_SUBLANES, w), F32), pltpu.VMEM((1, w), F32),
                        pltpu.VMEM((tc, w), F32), pltpu.VMEM((tc, w), F32)],
        compiler_params=_params(("parallel", "arbitrary")),
        name="lru_branch",
    )(proj, proj, conv_buf, h0, cw, cb, wa, ba, wx, bx, lam)


def _ret_kernel(q_ref, k_ref, v_ref, g_ref, cos_ref, sin_ref, s0_ref, gng_ref, gnb_ref,
                yb_ref, slast_ref, s_s, *, c, nch):
    t = pl.program_id(1)

    @pl.when(t == 0)
    def _():
        s_s[...] = s0_ref[0]

    half = RET_HEAD_DIM // 2
    ri = lax.broadcasted_iota(jnp.int32, (c, c), 0)
    ci = lax.broadcasted_iota(jnp.int32, (c, c), 1)
    diff = (ri - ci).astype(F32)
    it = lax.broadcasted_iota(jnp.int32, (c, 1), 0).astype(F32)
    log_g = [math.log1p(-(2.0 ** (-5.0 - h))) for h in range(RET_HEADS)]
    dmask = [jnp.where(diff >= 0, jnp.exp(lg * jnp.maximum(diff, 0.0)), 0.0) for lg in log_g]
    xi = [jnp.exp(lg * (it + 1.0)) for lg in log_g]
    zeta = [jnp.exp(lg * (c - 1.0 - it)) for lg in log_g]

    units = [(ch, h) for ch in range(nch) for h in range(RET_HEADS)]
    qr, inner, kv = {}, {}, {}
    for ch, h in units:
        rows = slice(ch * c, (ch + 1) * c)
        sl = slice(h * RET_HEAD_DIM, (h + 1) * RET_HEAD_DIM)
        cos2 = cos_ref[rows, :]
        sin2 = sin_ref[rows, :]
        qh = q_ref[0, rows, sl].astype(F32)
        kh = k_ref[0, rows, sl].astype(F32)
        vh = v_ref[0, rows, sl].astype(BF16)
        qr[ch, h] = (qh * cos2 + pltpu.roll(qh, half, 1) * sin2).astype(BF16)
        kr = (kh * cos2 + pltpu.roll(kh, half, 1) * sin2) * (RET_HEAD_DIM ** -0.5)
        scores = _dot_nt(qr[ch, h], kr) * dmask[h]
        inner[ch, h] = _dot(scores, vh)
        kv[ch, h] = _dot_tn(kr * zeta[h], vh)
    for h in range(RET_HEADS):
        sl = slice(h * RET_HEAD_DIM, (h + 1) * RET_HEAD_DIM)
        s = s_s[h]
        outs = []
        for ch in range(nch):
            outs.append(inner[ch, h] + _dot(qr[ch, h], s) * xi[h])
            s = math.exp(log_g[h] * c) * s + kv[ch, h]
        s_s[h] = s
        on = _layer_norm(jnp.concatenate(outs, axis=0), gng_ref[:, sl], gnb_ref[:, sl], LN_EPS)
        gh = g_ref[0, :, sl].astype(F32)
        yb_ref[0, :, sl] = on * (gh * _sigmoid(gh))
    slast_ref[0] = s_s[...]


def _ret_branch(proj, cos2, sin2, s0, gn_g, gn_b, c):
    bsz, t, _ = proj.shape
    w = RET_WIDTH
    dk = RET_HEAD_DIM
    nch = RET_CHUNKS_PER_STEP if t % (RET_CHUNKS_PER_STEP * c) == 0 else 1
    const = lambda bi, ti: (0, 0)
    per_b = lambda bi, ti: (bi, 0, 0, 0)
    col = lambda j: (lambda bi, ti: (bi, ti, j))
    tc = nch * c
    return pl.pallas_call(
        functools.partial(_ret_kernel, c=c, nch=nch),
        grid=(bsz, t // tc),
        in_specs=[pl.BlockSpec((1, tc, w), col(2)), pl.BlockSpec((1, tc, w), col(3)),
                  pl.BlockSpec((1, tc, w), col(4)), pl.BlockSpec((1, tc, w), col(5)),
                  pl.BlockSpec((tc, dk), lambda bi, ti: (ti, 0)), pl.BlockSpec((tc, dk), lambda bi, ti: (ti, 0)),
                  pl.BlockSpec((1, RET_HEADS, dk, dk), per_b),
                  pl.BlockSpec((1, w), const), pl.BlockSpec((1, w), const)],
        out_specs=[pl.BlockSpec((1, tc, w), lambda bi, ti: (bi, ti, 0)),
                   pl.BlockSpec((1, RET_HEADS, dk, dk), per_b)],
        out_shape=[jax.ShapeDtypeStruct((bsz, t, w), F32),
                   jax.ShapeDtypeStruct((bsz, RET_HEADS, dk, dk), F32)],
        scratch_shapes=[pltpu.VMEM((RET_HEADS, dk, dk), F32)],
        compiler_params=_params(("parallel", "arbitrary")),
        name="ret_branch",
    )(proj, proj, proj, proj, cos2, sin2, s0, gn_g, gn_b)


def _rwkv_proj_kernel(x_ref, sh_ref, mu_ref, wrkv_ref, w0_ref, w1_ref, w2_ref, a0_ref, a1_ref, a2_ref,
                      g1_ref, g2_ref, kk_ref, ka_ref, rk_ref, sel_ref, selt_ref,
                      r_o, lw_o, k_o, v_o, kk_o, b_o, g_o, bonus_o, prev_s, *, tm):
    t = pl.program_id(1)

    @pl.when(t == 0)
    def _():
        prev_s[...] = jnp.zeros(prev_s.shape, F32)
        prev_s[V7X_SUBLANES - 1:, :] = sh_ref[0]

    x = x_ref[0]
    ext = jnp.concatenate([prev_s[...], x], axis=0)
    xp = pltpu.roll(ext, 1, 0)[V7X_SUBLANES:, :]
    prev_s[...] = x[tm - V7X_SUBLANES:, :]
    dx = xp - x

    def mix(p):
        return (x + dx * mu_ref[p:p + 1, :]).astype(BF16)

    r = jnp.dot(mix(0), wrkv_ref[0], preferred_element_type=F32)
    k = jnp.dot(mix(1), wrkv_ref[1], preferred_element_type=F32)
    v = jnp.dot(mix(2), wrkv_ref[2], preferred_element_type=F32)
    wl = _dot(jnp.tanh(jnp.dot(mix(3), w1_ref[...], preferred_element_type=F32)), w2_ref[...])
    w = -_softplus(-(w0_ref[...] + wl)) - 0.5
    al = _dot(jnp.dot(mix(4), a1_ref[...], preferred_element_type=F32), a2_ref[...])
    iclr = _sigmoid(a0_ref[...] + al)
    gate = _dot(_sigmoid(jnp.dot(mix(5), g1_ref[...], preferred_element_type=F32)), g2_ref[...])

    kk = k * kk_ref[...]
    sq = jnp.dot((kk * kk).astype(BF16), sel_ref[...], preferred_element_type=F32)
    inv_norm = 1.0 / jnp.maximum(jnp.sqrt(sq), 1e-12)
    kk = kk * _head_bcast(inv_norm, selt_ref)
    k2 = k * (1.0 + (iclr - 1.0) * ka_ref[...])

    r_o[0] = r
    lw_o[0] = -jnp.exp(w)
    k_o[0] = k2
    v_o[0] = v.astype(BF16)
    kk_o[0] = kk
    b_o[0] = kk * iclr
    g_o[0] = gate
    bonus_o[0] = _seg_sum(r * k2 * rk_ref[...], sel_ref, selt_ref) * v


def _rwkv_proj(x, shift, mu, wrkv, w0, w1, w2, a0, a1, a2, g1, g2, k_k, k_a, r_k, sel, selt):
    bsz, t, d = x.shape
    tm = _row_tile(t, RWKV_PROJ_ROW_TILE)
    const2 = lambda bi, ti: (0, 0)
    const3 = lambda bi, ti: (0, 0, 0)
    full2 = lambda a: pl.BlockSpec(a.shape, const2)
    tile = pl.BlockSpec((1, tm, d), lambda bi, ti: (bi, ti, 0))
    out = jax.ShapeDtypeStruct((bsz, t, d), F32)
    return pl.pallas_call(
        functools.partial(_rwkv_proj_kernel, tm=tm),
        grid=(bsz, t // tm),
        in_specs=[tile, pl.BlockSpec((1, 1, d), lambda bi, ti: (bi, 0, 0)), full2(mu),
                  pl.BlockSpec(wrkv.shape, const3), full2(w0), full2(w1), full2(w2), full2(a0), full2(a1),
                  full2(a2), full2(g1), full2(g2), full2(k_k), full2(k_a), full2(r_k), full2(sel), full2(selt)],
        out_specs=[tile] * 8,
        out_shape=[out, out, out, jax.ShapeDtypeStruct((bsz, t, d), BF16), out, out, out, out],
        scratch_shapes=[pltpu.VMEM((V7X_SUBLANES, d), F32)],
        compiler_params=_params(("parallel", "arbitrary")),
        name="rwkv_proj",
    )(x, shift, mu, wrkv, w0, w1, w2, a0, a1, a2, g1, g2, k_k, k_a, r_k, sel, selt)


def _wkv_kernel(r_ref, lw_ref, k_ref, v_ref, kk_ref, b_ref, s0_ref, o_ref, slast_ref,
                s_s, o0_s, rhat_s, gmat_s, hmat_s, *, c, nch):
    t = pl.program_id(1)
    last = pl.num_programs(1) - 1
    n = RWKV_HEAD
    block_refs = (r_ref, lw_ref, k_ref, v_ref, kk_ref, b_ref, o_ref, s_s, o0_s, rhat_s, gmat_s, hmat_s)

    @pl.when(t == 0)
    def _():
        for h in range(RWKV_HEADS):
            s_s[:, h * n:(h + 1) * n] = s0_ref[0, h]
        o0_s[...] = jnp.zeros(o0_s.shape, o0_s.dtype)
        rhat_s[...] = jnp.zeros(rhat_s.shape, rhat_s.dtype)
        gmat_s[...] = jnp.zeros(gmat_s.shape, gmat_s.dtype)
        hmat_s[...] = jnp.zeros(hmat_s.shape, hmat_s.dtype)

    @pl.when(t < last)
    def _():
        _wkv_block(*block_refs, c=c, nch=nch, prepare=True)

    @pl.when(t == last)
    def _():
        _wkv_block(*block_refs, c=c, nch=nch, prepare=False)
        for h in range(RWKV_HEADS):
            slast_ref[0, h] = s_s[:, h * n:(h + 1) * n]


def _wkv_block(r_ref, lw_ref, k_ref, v_ref, kk_ref, b_ref, o_ref, s_s, o0_s, rhat_s, gmat_s, hmat_s, *,
               c, nch, prepare):
    t = pl.program_id(1)
    n = RWKV_HEAD
    g = WKV_GROUP
    gl = g * n
    gc = g * c
    ngroups = RWKV_HEADS // g

    iota = lambda shape, dim: lax.broadcasted_iota(jnp.int32, shape, dim)
    head_k = iota((1, gl), 1) // n
    head_j = iota((1, gc), 1) // c
    row_t = iota((c, gc), 0)
    col_j = iota((c, gc), 1) % c
    strict = row_t > col_j
    lower = row_t >= col_j
    hc = c // 2
    half_j = col_j // hc
    half_j = half_j[:1]
    blk_j = iota((1, gc), 1) // hc
    eye_h = jnp.where(iota((hc, gc), 0) == iota((hc, gc), 1) % hc, 1.0, 0.0).astype(F32)
    blk = (iota((gl, gl), 0) // n) == (iota((gl, gl), 1) // n)
    eye_l = iota((gl, gl), 0) == iota((gl, gl), 1)

    def bd(z, lane_head):
        zb = z.astype(BF16)
        return jnp.concatenate([jnp.where(lane_head == h, zb, jnp.zeros_like(zb)) for h in range(g)], axis=0)

    def bd_half(z):
        zb = z.astype(BF16)
        return jnp.concatenate([jnp.where(blk_j == q, zb, jnp.zeros_like(zb)) for q in range(2 * g)], axis=0)

    def mm(a, b):
        return jnp.dot(a.astype(BF16), b, preferred_element_type=F32)

    def mm_nt(a, b):
        return lax.dot_general(a.astype(BF16), b, _NT, preferred_element_type=F32)

    started = t > 0
    group_lanes = [slice(gi * gl, (gi + 1) * gl) for gi in range(ngroups)]
    state = [s_s[:, lanes] for lanes in group_lanes]

    def recurrence_step(ch):
        rows = slice(ch * c, (ch + 1) * c)
        for gi, lanes in enumerate(group_lanes):
            s = state[gi]
            o_ref[0, rows, lanes] = o0_s[rows, lanes] + mm_nt(rhat_s[rows, lanes], bd(s, head_k))
            s = mm(s, gmat_s[ch * ngroups + gi]) + hmat_s[ch, :, lanes]
            if ch == nch - 1:
                s_s[:, lanes] = jnp.where(started, s, s_s[:, lanes])
            state[gi] = s

    if not prepare:
        for ch in range(nch):
            recurrence_step(ch)
        return

    units = [(ch, gi) for ch in range(nch) for gi in range(ngroups)]
    u = {}
    for ch in range(nch):
        rows = slice(ch * c, (ch + 1) * c)
        lw = lw_ref[0, rows, :]
        cum = _cumsum_rows(lw)
        cum_last = cum[c - 1:c, :]
        e_neg = jnp.exp(-cum)
        e_end = jnp.exp(cum_last - cum)
        kk = kk_ref[0, rows, :]
        b = b_ref[0, rows, :]
        k = k_ref[0, rows, :]
        full = dict(a_t=kk * jnp.exp(cum - lw), b_t=b * e_neg, k_t=k * e_neg, r_t=r_ref[0, rows, :] * jnp.exp(cum),
                    b_e=b * e_end, k_e=k * e_end, d_end=jnp.exp(cum_last), v=v_ref[0, rows, :])
        for gi in range(ngroups):
            lanes = slice(gi * gl, (gi + 1) * gl)
            u[ch, gi] = {name: val[:, lanes] for name, val in full.items()}

    for key in units:
        d = u[key]
        x2 = jnp.concatenate([d['a_t'], d['r_t']], axis=0).astype(BF16)
        d['mb'] = mm_nt(x2, bd(d['b_t'], head_k))
        d['mk'] = mm_nt(x2, bd(d['k_t'], head_k))
    for key in units:
        d = u[key]
        mb = d['mb']
        mk = d['mk']
        d['p_rb'] = jnp.where(lower, mb[c:], 0.0).astype(BF16)
        mk_lo = jnp.concatenate([jnp.where(strict, mk[:c], 0.0), jnp.where(lower, mk[c:], 0.0)], axis=0)
        wo = mm(mk_lo, bd(d['v'], head_k))
        d['w_h'] = wo[:c]
        d['o0'] = wo[c:]
        nmat = -jnp.where(strict, mb[:c], 0.0)
        d['n_low'] = jnp.where(jnp.logical_and(row_t >= hc, half_j == 0), nmat, 0.0)
        d['pw'] = jnp.where(half_j == 0, nmat[:hc], nmat[hc:])
        d['tinv'] = eye_h + d['pw']
        if key[1] == ngroups - 1:
            recurrence_step(key[0])
    nsq = int(math.log2(hc)) - 1
    for level in range(nsq):
        for key in units:
            d = u[key]
            pwb = d['pw'].astype(BF16)
            w = bd_half(pwb)
            if level == 0:
                d['pw'] = mm(pwb, w)
            else:
                res = mm(jnp.concatenate([pwb, d['tinv'].astype(BF16)], axis=0), w)
                d['pw'] = res[:hc]
                d['tinv'] = d['tinv'] + res[hc:]
    for key in units:
        d = u[key]
        d['tinv'] = d['tinv'] + mm(d['tinv'], bd_half(d['pw']))
    for key in units:
        d = u[key]
        t_lo = jnp.where(half_j == 1, d['tinv'], 0.0)
        t_diag = jnp.concatenate([jnp.where(half_j == 0, d['tinv'], 0.0), t_lo], axis=0)
        corr = mm(mm(t_lo, bd(d['n_low'], head_j)), bd(t_diag, head_j))
        d['tinv'] = t_diag + jnp.concatenate([jnp.zeros((hc, gc), F32), corr], axis=0)
    for key in units:
        d = u[key]
        tb = d['tinv'].astype(BF16)
        d['a_hat'] = mm(tb, bd(d['a_t'], head_k))
        d['w_hat'] = mm(tb, bd(d['w_h'], head_k))
    for key in units:
        d = u[key]
        d['r_hat'] = d['r_t'] - mm(d['p_rb'], bd(d['a_hat'], head_k))
        d['o0'] = d['o0'] - mm(d['p_rb'], bd(d['w_hat'], head_k))
        b_e = d['b_e'].astype(BF16)
        gfull = lax.dot_general(d['a_hat'].astype(BF16), b_e, _TN, preferred_element_type=F32)
        d['gmat'] = (jnp.where(eye_l, d['d_end'], 0.0) - jnp.where(blk, gfull, 0.0)).astype(BF16)
        vw = jnp.concatenate([d['v'].astype(BF16), d['w_hat'].astype(BF16)], axis=0)
        kb = jnp.concatenate([d['k_e'].astype(BF16), -b_e], axis=0)
        zf = lax.dot_general(vw, kb, _TN, preferred_element_type=F32)
        hmat = jnp.where(head_k == 0, zf[:n], 0.0)
        for h in range(1, g):
            hmat = hmat + jnp.where(head_k == h, zf[h * n:(h + 1) * n], 0.0)
        d['hmat'] = hmat

    for (ch, gi) in units:
        d = u[ch, gi]
        lanes = slice(gi * gl, (gi + 1) * gl)
        o0_s[ch * c:(ch + 1) * c, lanes] = d['o0']
        rhat_s[ch * c:(ch + 1) * c, lanes] = d['r_hat'].astype(BF16)
        gmat_s[ch * ngroups + gi] = d['gmat']
        hmat_s[ch, :, lanes] = d['hmat']


def _wkv_scan(r, lw, k, v, kk, b, s0, c):
    bsz, t, d = r.shape
    n = RWKV_HEAD
    nch = WKV_CHUNKS_PER_STEP if t % (WKV_CHUNKS_PER_STEP * c) == 0 else 1
    nsteps = t // (nch * c)
    g = WKV_GROUP
    tile_in = pl.BlockSpec((1, nch * c, d), lambda bi, ti: (bi, jnp.minimum(ti, nsteps - 1), 0))
    tile_out = pl.BlockSpec((1, nch * c, d), lambda bi, ti: (bi, jnp.maximum(ti - 1, 0), 0))
    st = pl.BlockSpec((1, RWKV_HEADS, n, n), lambda bi, ti: (bi, 0, 0, 0))
    return pl.pallas_call(
        functools.partial(_wkv_kernel, c=c, nch=nch),
        grid=(bsz, nsteps + 1),
        in_specs=[tile_in] * 6 + [st],
        out_specs=[tile_out, st],
        out_shape=[jax.ShapeDtypeStruct((bsz, t, d), F32), jax.ShapeDtypeStruct(s0.shape, F32)],
        scratch_shapes=[pltpu.VMEM((n, d), F32), pltpu.VMEM((nch * c, d), F32), pltpu.VMEM((nch * c, d), BF16),
                        pltpu.VMEM((nch * (RWKV_HEADS // g), g * n, g * n), BF16), pltpu.VMEM((nch, n, d), F32)],
        compiler_params=_params(("parallel", "arbitrary")),
        name="wkv_scan",
    )(r, lw, k, v, kk, b, s0)


def _wkv_out_kernel(x_ref, o_ref, bonus_ref, gate_ref, gng_ref, gnb_ref, sel_ref, selt_ref, w_ref,
                    g_ref, b_ref, y_ref):
    o = o_ref[...]
    inv_n = 1.0 / RWKV_HEAD
    mu = _seg_sum(o, sel_ref, selt_ref) * inv_n
    oc = o - mu
    var = _seg_sum(oc * oc, sel_ref, selt_ref) * inv_n
    on = oc * lax.rsqrt(var + RWKV_GN_EPS) * gng_ref[...] + gnb_ref[...]
    z = (on + bonus_ref[...]) * gate_ref[...]
    y = jnp.dot(z.astype(BF16), w_ref[...], preferred_element_type=F32)
    y_ref[...] = _layer_norm(ALPHA * x_ref[...] + y, g_ref[...], b_ref[...], LN_EPS)


def _wkv_out_postnorm(x, o, bonus, gate, gn_g, gn_b, sel, selt, w, g, b):
    n, d = x.shape
    tm = _row_tile(n)
    row = pl.BlockSpec((tm, d), lambda i: (i, 0))
    const = lambda i: (0, 0)
    full = lambda a: pl.BlockSpec(a.shape, const)
    return pl.pallas_call(
        _wkv_out_kernel,
        grid=(n // tm,),
        in_specs=[row, row, row, row, full(gn_g), full(gn_b), full(sel), full(selt), full(w), full(g), full(b)],
        out_specs=row,
        out_shape=jax.ShapeDtypeStruct((n, d), F32),
        compiler_params=_params(("parallel",)),
        name="wkv_out_postnorm",
    )(x, o, bonus, gate, gn_g, gn_b, sel, selt, w, g, b)


def _prep_weights(ln_g, ln_b, ffn_up, ffn_down, xa_q, xa_o, l0_w_in, l0_conv_w, l0_conv_b, l0_lru_wa,
                  l0_lru_ba, l0_lru_wx, l0_lru_bx, l0_lru_lambda, l0_ret_gn_g, l0_ret_gn_b, l0_w_out,
                  l1_mu, l1_w_rkv, l1_w0, l1_w1, l1_w2, l1_a0, l1_a1, l1_a2, l1_g1, l1_g2, l1_k_k, l1_k_a,
                  l1_r_k, l1_gn_g, l1_gn_b, l1_w_out):
    d = D_MODEL
    row = lambda a: a.reshape(1, -1).astype(F32)

    def block_diag(w):
        eye = jnp.eye(LRU_BLOCKS, dtype=w.dtype)
        return jnp.einsum('gij,gh->gihj', w, eye).reshape(LRU_WIDTH, LRU_WIDTH).astype(BF16)

    head_of_col = jnp.arange(d) // RWKV_HEAD
    sel = (head_of_col[:, None] == jnp.arange(SEL_WIDTH)[None, :]).astype(BF16)
    sel_row = jnp.arange(SEL_WIDTH)
    selt = ((sel_row[:, None] % RWKV_HEADS == head_of_col[None, :])
            & (sel_row[:, None] < 2 * RWKV_HEADS)).astype(BF16)
    return dict(
        ln_g=ln_g, ln_b=ln_b, ffn_up=ffn_up.astype(BF16), ffn_down=ffn_down.astype(BF16),
        xa_q=(xa_q * (MEM_HEAD_DIM ** -0.5)).astype(BF16), xa_o=xa_o.astype(BF16),
        w_in=l0_w_in.astype(BF16), conv_w=l0_conv_w, conv_b=row(l0_conv_b),
        lru_wa=block_diag(l0_lru_wa), lru_ba=row(l0_lru_ba), lru_wx=block_diag(l0_lru_wx),
        lru_bx=row(l0_lru_bx), lru_lam=row(l0_lru_lambda),
        ret_gn_g=row(l0_ret_gn_g), ret_gn_b=row(l0_ret_gn_b),
        w_out_a=l0_w_out[:LRU_WIDTH].astype(BF16), w_out_b=l0_w_out[LRU_WIDTH:].astype(BF16),
        mu=l1_mu, w_rkv=l1_w_rkv.astype(BF16), w0=row(l1_w0), w1=l1_w1.astype(BF16), w2=l1_w2.astype(BF16),
        a0=row(l1_a0), a1=l1_a1.astype(BF16), a2=l1_a2.astype(BF16), g1=l1_g1.astype(BF16),
        g2=l1_g2.astype(BF16), k_k=row(l1_k_k), k_a=row(l1_k_a), r_k=row(l1_r_k),
        gn_g=row(l1_gn_g), gn_b=row(l1_gn_b), w_out_c=l1_w_out.astype(BF16), sel=sel, selt=selt,
    )


def _rotary_tables(pos):
    half = RET_HEAD_DIM // 2
    inv_freq = ROPE_BASE ** (-jnp.arange(half, dtype=F32) / half)
    ang = pos.astype(F32)[:, None] * inv_freq[None, :]
    cos = jnp.cos(ang)
    sin = jnp.sin(ang)
    return jnp.concatenate([cos, cos], axis=-1), jnp.concatenate([-sin, sin], axis=-1)


def _run_trunk(x, pos, mem_k, mem_v, states, p):
    bsz, t, d = x.shape
    n = bsz * t
    (conv_buf, h0, s_ret), (shift, s_wkv) = states
    lng = lambda l, j: p['ln_g'][l, j].reshape(1, d)
    lnb = lambda l, j: p['ln_b'][l, j].reshape(1, d)
    flat = lambda a: a.reshape(n, a.shape[-1])
    chunk = min(CHUNK, t)
    ffn = lambda xx, l, j, nj: _ffn_postnorm(xx, p['ffn_up'], p['ffn_down'], l, j, lng(l, nj), lnb(l, nj))

    x = ffn(flat(x), 0, 0, 0)
    proj = _matmul(x, p['w_in'], BF16).reshape(bsz, t, -1)
    ya, new_buf, h_last = _lru_branch(proj, conv_buf, h0.reshape(bsz, 1, LRU_WIDTH), p['conv_w'], p['conv_b'],
                                      p['lru_wa'], p['lru_ba'], p['lru_wx'], p['lru_bx'], p['lru_lam'])
    cos2, sin2 = _rotary_tables(pos)
    yb, s_ret_new = _ret_branch(proj, cos2, sin2, s_ret, p['ret_gn_g'], p['ret_gn_b'], chunk)
    x = _ab_out_postnorm(x, flat(ya), flat(yb), p['w_out_a'], p['w_out_b'], lng(0, 1), lnb(0, 1))
    x = _xattn_postnorm(x.reshape(bsz, t, d), mem_k, mem_v, p['xa_q'], p['xa_o'], 0, lng(0, 2), lnb(0, 2))
    x = ffn(flat(x), 0, 1, 3)

    x = ffn(x, 1, 0, 0)
    x3 = x.reshape(bsz, t, d)
    r, lw, k2, v, kk, b, gate, bonus = _rwkv_proj(
        x3, shift, p['mu'], p['w_rkv'], p['w0'], p['w1'], p['w2'], p['a0'], p['a1'], p['a2'], p['g1'], p['g2'],
        p['k_k'], p['k_a'], p['r_k'], p['sel'], p['selt'])
    o, s_wkv_new = _wkv_scan(r, lw, k2, v, kk, b, s_wkv, chunk)
    new_shift = x3[:, t - 1:, :]
    x = _wkv_out_postnorm(x, flat(o), flat(bonus), flat(gate), p['gn_g'], p['gn_b'], p['sel'], p['selt'],
                          p['w_out_c'], lng(1, 1), lnb(1, 1))
    x = _xattn_postnorm(x.reshape(bsz, t, d), mem_k, mem_v, p['xa_q'], p['xa_o'], 1, lng(1, 2), lnb(1, 2))
    x = ffn(flat(x), 1, 1, 3)
    new_states = ((new_buf, h_last.reshape(bsz, LRU_WIDTH), s_ret_new), (new_shift, s_wkv_new))
    return x.reshape(bsz, t, d), new_states


def kernel(x_prompt, x_sample, mem_prompt, state_conv0, state_lru0, state_ret0, state_shift1, state_wkv1,
           cache_mem_k, cache_mem_v, ln_g, ln_b, ffn_up, ffn_down, xa_q, xa_k, xa_v, xa_o,
           l0_w_in, l0_conv_w, l0_conv_b, l0_lru_wa, l0_lru_ba, l0_lru_wx, l0_lru_bx, l0_lru_lambda,
           l0_ret_gn_g, l0_ret_gn_b, l0_w_out, l1_mu, l1_w_rkv, l1_w0, l1_w1, l1_w2, l1_a0, l1_a1, l1_a2,
           l1_g1, l1_g2, l1_k_k, l1_k_a, l1_r_k, l1_gn_g, l1_gn_b, l1_w_out):
    d = D_MODEL
    p = _prep_weights(ln_g, ln_b, ffn_up, ffn_down, xa_q, xa_o, l0_w_in, l0_conv_w, l0_conv_b, l0_lru_wa,
                      l0_lru_ba, l0_lru_wx, l0_lru_bx, l0_lru_lambda, l0_ret_gn_g, l0_ret_gn_b, l0_w_out,
                      l1_mu, l1_w_rkv, l1_w0, l1_w1, l1_w2, l1_a0, l1_a1, l1_a2, l1_g1, l1_g2, l1_k_k, l1_k_a,
                      l1_r_k, l1_gn_g, l1_gn_b, l1_w_out)

    bp, tp, _ = x_prompt.shape
    dt = x_prompt.dtype
    mem_k_p, mem_kh_p = _mem_proj(mem_prompt, xa_k.astype(BF16))
    mem_v_p, mem_vh_p = _mem_proj(mem_prompt, xa_v.astype(BF16))
    zero_states = ((jnp.zeros((bp, CONV_WIDTH - 1, LRU_WIDTH), dt), jnp.zeros((bp, LRU_WIDTH), dt),
                    jnp.zeros((bp, RET_HEADS, RET_HEAD_DIM, RET_HEAD_DIM), dt)),
                   (jnp.zeros((bp, 1, d), dt), jnp.zeros((bp, RWKV_HEADS, RWKV_HEAD, RWKV_HEAD), dt)))
    y_prompt, st_p = _run_trunk(x_prompt, jnp.arange(tp, dtype=jnp.int32), mem_kh_p, mem_vh_p, zero_states, p)

    bs, ts, _ = x_sample.shape
    pos_s = PAST_LEN + jnp.arange(ts, dtype=jnp.int32)
    sample_states = ((state_conv0, state_lru0, state_ret0), (state_shift1, state_wkv1))
    head_major = lambda a: jnp.transpose(a, (0, 1, 3, 2, 4)).astype(BF16)
    y_sample, st_s = _run_trunk(x_sample, pos_s, head_major(cache_mem_k), head_major(cache_mem_v), sample_states, p)

    (p_conv0, p_lru0, p_ret0), (p_shift1, p_wkv1) = st_p
    (s_conv0, s_lru0, s_ret0), (s_shift1, s_wkv1) = st_s
    return (y_prompt, y_sample, mem_k_p, mem_v_p,
            p_conv0, p_lru0, p_ret0, p_shift1, p_wkv1, s_conv0, s_lru0, s_ret0, s_shift1, s_wkv1)
```

```python
import functools
import math

import jax
import jax.numpy as jnp
from jax import lax
from jax.experimental import pallas as pl
from jax.experimental.pallas import tpu as pltpu

D_MODEL = 1024
DEPTH = 2
PAST_LEN = 4096
CHUNK = 64
N_MEM = 256
MEM_HEADS = 4
MEM_HEAD_DIM = D_MODEL // MEM_HEADS
D_FF = 2816
LRU_WIDTH = 512
LRU_BLOCKS = 8
LRU_BLOCK = LRU_WIDTH // LRU_BLOCKS
CONV_WIDTH = 4
LRU_C = 8.0
RET_HEADS = 4
RET_WIDTH = 512
RET_HEAD_DIM = RET_WIDTH // RET_HEADS
ROPE_BASE = 10000.0
RWKV_HEAD = 64
RWKV_HEADS = D_MODEL // RWKV_HEAD
LN_EPS = 1e-5
RWKV_GN_EPS = 64e-5
ALPHA = (2 * DEPTH) ** 0.25

F32 = jnp.float32
BF16 = jnp.bfloat16

V7X_SUBLANES = 8
V7X_LANES = 128
VMEM_LIMIT_BYTES = 56 * 1024 * 1024
FFN_CHUNK = 256
ROW_TILE = 1024
FFN_ROW_TILE = 1024
RWKV_PROJ_ROW_TILE = 512
SEL_WIDTH = V7X_LANES
WKV_GROUP = 4
WKV_CHUNKS_PER_STEP = 4
RET_CHUNKS_PER_STEP = 8

_NT = (((1,), (1,)), ((), ()))
_TN = (((0,), (0,)), ((), ()))


def _dot(a, b):
    return jnp.dot(a.astype(BF16), b.astype(BF16), preferred_element_type=F32)


def _dot_nt(a, b):
    return lax.dot_general(a.astype(BF16), b.astype(BF16), _NT, preferred_element_type=F32)


def _dot_tn(a, b):
    return lax.dot_general(a.astype(BF16), b.astype(BF16), _TN, preferred_element_type=F32)


def _head_bcast(vals, selt_ref):
    lane = lax.broadcasted_iota(jnp.int32, vals.shape, 1)
    vals = jnp.where(lane < RWKV_HEADS, vals, 0.0)
    hi = vals.astype(BF16).astype(F32)
    packed = hi + pltpu.roll(vals - hi, RWKV_HEADS, 1)
    return jnp.dot(packed.astype(BF16), selt_ref[...], preferred_element_type=F32)


def _seg_sum(z, sel_ref, selt_ref):
    sums = jnp.dot(z.astype(BF16), sel_ref[...], preferred_element_type=F32)
    return _head_bcast(sums, selt_ref)


def _cumsum_rows(x):
    rows = x.shape[0]
    row8 = lax.broadcasted_iota(jnp.int32, x.shape, 0) % V7X_SUBLANES
    s = 1
    while s < V7X_SUBLANES:
        x = jnp.where(row8 >= s, x + pltpu.roll(x, s, 0), x)
        s *= 2
    out = [x[:V7X_SUBLANES]]
    for j in range(1, rows // V7X_SUBLANES):
        out.append(x[j * V7X_SUBLANES:(j + 1) * V7X_SUBLANES] + out[-1][V7X_SUBLANES - 1:])
    return jnp.concatenate(out, axis=0)


def _layer_norm(y, g, b, eps):
    mu = jnp.mean(y, axis=-1, keepdims=True)
    yc = y - mu
    var = jnp.mean(yc * yc, axis=-1, keepdims=True)
    return yc * lax.rsqrt(var + eps) * g + b


def _sigmoid(x):
    return 1.0 / (1.0 + jnp.exp(-x))


def _softplus(x):
    return jnp.maximum(x, 0.0) + jnp.log1p(jnp.exp(-jnp.abs(x)))


def _gelu_tanh(x):
    return 0.5 * x * (1.0 + jnp.tanh(math.sqrt(2.0 / math.pi) * (x + 0.044715 * (x * x * x))))


def _params(sem):
    return pltpu.CompilerParams(dimension_semantics=sem, vmem_limit_bytes=VMEM_LIMIT_BYTES)


def _row_tile(n, pref=ROW_TILE):
    return pref if n % pref == 0 else n


def _ffn_kernel(x_ref, wup_ref, wdn_ref, g_ref, b_ref, o_ref):
    x = x_ref[...]
    xb = x.astype(BF16)
    acc = jnp.zeros(x.shape, F32)
    for lo in range(0, D_FF, FFN_CHUNK):
        hg = jnp.dot(xb, wup_ref[:, lo:lo + FFN_CHUNK], preferred_element_type=F32)
        hu = jnp.dot(xb, wup_ref[:, D_FF + lo:D_FF + lo + FFN_CHUNK], preferred_element_type=F32)
        h = hg * _sigmoid(hg) * hu
        acc = acc + jnp.dot(h.astype(BF16), wdn_ref[lo:lo + FFN_CHUNK, :], preferred_element_type=F32)
    o_ref[...] = _layer_norm(ALPHA * x + 0.5 * acc, g_ref[...], b_ref[...], LN_EPS)


def _ffn_postnorm(x, w_up, w_down, layer, which, g, b):
    n, d = x.shape
    tm = _row_tile(n, FFN_ROW_TILE)
    pick = lambda i: (layer, which, 0, 0)
    return pl.pallas_call(
        _ffn_kernel,
        grid=(n // tm,),
        in_specs=[
            pl.BlockSpec((tm, d), lambda i: (i, 0)),
            pl.BlockSpec((None, None, d, 2 * D_FF), pick, pipeline_mode=pl.Buffered(1)),
            pl.BlockSpec((None, None, D_FF, d), pick, pipeline_mode=pl.Buffered(1)),
            pl.BlockSpec((1, d), lambda i: (0, 0)),
            pl.BlockSpec((1, d), lambda i: (0, 0)),
        ],
        out_specs=pl.BlockSpec((tm, d), lambda i: (i, 0)),
        out_shape=jax.ShapeDtypeStruct((n, d), F32),
        compiler_params=_params(("parallel",)),
        name="ffn_postnorm",
    )(x, w_up, w_down, g, b)


def _mm_kernel(x_ref, w_ref, o_ref):
    o_ref[...] = jnp.dot(x_ref[...].astype(BF16), w_ref[...], preferred_element_type=F32).astype(o_ref.dtype)


def _matmul(x, w, out_dtype):
    n, k = x.shape
    m = w.shape[1]
    tm = _row_tile(n)
    return pl.pallas_call(
        _mm_kernel,
        grid=(n // tm,),
        in_specs=[pl.BlockSpec((tm, k), lambda i: (i, 0)), pl.BlockSpec((k, m), lambda i: (0, 0))],
        out_specs=pl.BlockSpec((tm, m), lambda i: (i, 0)),
        out_shape=jax.ShapeDtypeStruct((n, m), out_dtype),
        compiler_params=_params(("parallel",)),
        name="proj",
    )(x, w)


def _mem_proj_kernel(x_ref, w_ref, o_ref, oh_ref):
    y = jnp.dot(x_ref[...].astype(BF16), w_ref[...], preferred_element_type=F32)
    for h in range(MEM_HEADS):
        yh = y[:, h * MEM_HEAD_DIM:(h + 1) * MEM_HEAD_DIM]
        o_ref[:, h, :] = yh
        oh_ref[h] = yh.astype(BF16)


def _mem_proj(mem, w):
    bsz, m, d = mem.shape
    depth = w.shape[0]
    return pl.pallas_call(
        _mem_proj_kernel,
        grid=(depth, bsz),
        in_specs=[pl.BlockSpec((None, m, d), lambda l, bi: (bi, 0, 0)),
                  pl.BlockSpec((None, d, d), lambda l, bi: (l, 0, 0))],
        out_specs=[pl.BlockSpec((None, None, m, MEM_HEADS, MEM_HEAD_DIM), lambda l, bi: (l, bi, 0, 0, 0)),
                   pl.BlockSpec((None, None, MEM_HEADS, m, MEM_HEAD_DIM), lambda l, bi: (l, bi, 0, 0, 0))],
        out_shape=[jax.ShapeDtypeStruct((depth, bsz, m, MEM_HEADS, MEM_HEAD_DIM), F32),
                   jax.ShapeDtypeStruct((depth, bsz, MEM_HEADS, m, MEM_HEAD_DIM), BF16)],
        compiler_params=_params(("parallel", "parallel")),
        name="mem_proj",
    )(mem, w)


def _ab_out_kernel(x_ref, ya_ref, yb_ref, wa_ref, wb_ref, g_ref, b_ref, o_ref):
    y = (jnp.dot(ya_ref[...].astype(BF16), wa_ref[...], preferred_element_type=F32)
         + jnp.dot(yb_ref[...].astype(BF16), wb_ref[...], preferred_element_type=F32))
    o_ref[...] = _layer_norm(ALPHA * x_ref[...] + y, g_ref[...], b_ref[...], LN_EPS)


def _ab_out_postnorm(x, ya, yb, wa, wb, g, b):
    n, d = x.shape
    ka = ya.shape[1]
    kb = yb.shape[1]
    tm = _row_tile(n)
    row = lambda i: (i, 0)
    const = lambda i: (0, 0)
    return pl.pallas_call(
        _ab_out_kernel,
        grid=(n // tm,),
        in_specs=[pl.BlockSpec((tm, d), row), pl.BlockSpec((tm, ka), row), pl.BlockSpec((tm, kb), row),
                  pl.BlockSpec((ka, d), const), pl.BlockSpec((kb, d), const),
                  pl.BlockSpec((1, d), const), pl.BlockSpec((1, d), const)],
        out_specs=pl.BlockSpec((tm, d), row),
        out_shape=jax.ShapeDtypeStruct((n, d), F32),
        compiler_params=_params(("parallel",)),
        name="ab_out_postnorm",
    )(x, ya, yb, wa, wb, g, b)


def _xattn_kernel(x_ref, k_ref, v_ref, wq_ref, wo_ref, g_ref, b_ref, o_ref):
    x = x_ref[0]
    q = jnp.dot(x.astype(BF16), wq_ref[...], preferred_element_type=F32)
    qb = q.astype(BF16)
    heads = range(MEM_HEADS)
    scores = [lax.dot_general(qb[:, h * MEM_HEAD_DIM:(h + 1) * MEM_HEAD_DIM], k_ref[h], _NT,
                              preferred_element_type=F32) for h in heads]
    probs = []
    for s in scores:
        p = jnp.exp(s - jnp.max(s, axis=-1, keepdims=True))
        probs.append((p * (1.0 / jnp.sum(p, axis=-1, keepdims=True))).astype(BF16))
    o = jnp.concatenate([jnp.dot(probs[h], v_ref[h], preferred_element_type=F32) for h in heads], axis=-1)
    y = jnp.dot(o.astype(BF16), wo_ref[...], preferred_element_type=F32)
    o_ref[0] = _layer_norm(ALPHA * x + y, g_ref[...], b_ref[...], LN_EPS)


def _xattn_postnorm(x, mem_k, mem_v, wq, wo, layer, g, b):
    bsz, t, d = x.shape
    tm = _row_tile(t)
    const = lambda bi, ti: (0, 0)
    mem_spec = pl.BlockSpec((None, None, MEM_HEADS, N_MEM, MEM_HEAD_DIM), lambda bi, ti: (layer, bi, 0, 0, 0))
    w_spec = pl.BlockSpec((None, d, d), lambda bi, ti: (layer, 0, 0))
    return pl.pallas_call(
        _xattn_kernel,
        grid=(bsz, t // tm),
        in_specs=[pl.BlockSpec((1, tm, d), lambda bi, ti: (bi, ti, 0)),
                  mem_spec, mem_spec, w_spec, w_spec,
                  pl.BlockSpec((1, d), const), pl.BlockSpec((1, d), const)],
        out_specs=pl.BlockSpec((1, tm, d), lambda bi, ti: (bi, ti, 0)),
        out_shape=jax.ShapeDtypeStruct((bsz, t, d), F32),
        compiler_params=_params(("parallel", "parallel")),
        name="xattn_postnorm",
    )(x, mem_k, mem_v, wq, wo, g, b)


def _lru_kernel(xa_ref, ga_ref, cbuf_ref, h0_ref, cw_ref, cb_ref, wa_ref, ba_ref, wx_ref, bx_ref, lam_ref,
                ya_ref, nbuf_ref, hlast_ref, prev_s, h_s, a_s, u_s, *, tc):
    t = pl.program_id(1)

    @pl.when(t == 0)
    def _():
        prev_s[...] = jnp.zeros(prev_s.shape, F32)
        prev_s[V7X_SUBLANES - (CONV_WIDTH - 1):, :] = cbuf_ref[0]
        h_s[...] = h0_ref[0]

    xa = xa_ref[0].astype(F32)
    ext = jnp.concatenate([prev_s[...], xa], axis=0)
    xc = cb_ref[...] + xa * cw_ref[CONV_WIDTH - 1:CONV_WIDTH, :]
    for s in range(1, CONV_WIDTH):
        shifted = pltpu.roll(ext, s, 0)[V7X_SUBLANES:, :]
        xc = xc + shifted * cw_ref[CONV_WIDTH - 1 - s:CONV_WIDTH - s, :]
    prev_s[...] = xa[tc - V7X_SUBLANES:, :]
    nbuf_ref[0] = xa[tc - (CONV_WIDTH - 1):, :]

    xcb = xc.astype(BF16)
    r = _sigmoid(jnp.dot(xcb, wa_ref[...], preferred_element_type=F32) + ba_ref[...])
    i = _sigmoid(jnp.dot(xcb, wx_ref[...], preferred_element_type=F32) + bx_ref[...])
    log_a = (-LRU_C * _softplus(-lam_ref[...])) * r
    a = jnp.exp(log_a)
    th = jnp.tanh(log_a)
    u = jnp.sqrt(-2.0 * th / (1.0 - th)) * (i * xc)

    row8 = lax.broadcasted_iota(jnp.int32, a.shape, 0) % V7X_SUBLANES
    s = 1
    while s < V7X_SUBLANES:
        inside = row8 >= s
        u = jnp.where(inside, u + a * pltpu.roll(u, s, 0), u)
        a = jnp.where(inside, a * pltpu.roll(a, s, 0), a)
        s *= 2
    a_s[...] = a
    u_s[...] = u

    def body(j, h):
        base = pl.multiple_of(j * V7X_SUBLANES, V7X_SUBLANES)
        hb = u_s[pl.ds(base, V7X_SUBLANES), :] + a_s[pl.ds(base, V7X_SUBLANES), :] * h
        u_s[pl.ds(base, V7X_SUBLANES), :] = hb
        return hb[V7X_SUBLANES - 1:, :]

    h = lax.fori_loop(0, tc // V7X_SUBLANES, body, h_s[...], unroll=4)
    h_s[...] = h
    hlast_ref[0] = h
    ya_ref[0] = u_s[...] * _gelu_tanh(ga_ref[0].astype(F32))


def _lru_branch(proj, conv_buf, h0, cw, cb, wa, ba, wx, bx, lam):
    bsz, t, _ = proj.shape
    w = LRU_WIDTH
    tc = _row_tile(t)
    const = lambda bi, ti: (0, 0)
    per_b = lambda bi, ti: (bi, 0, 0)
    return pl.pallas_call(
        functools.partial(_lru_kernel, tc=tc),
        grid=(bsz, t // tc),
        in_specs=[pl.BlockSpec((1, tc, w), lambda bi, ti: (bi, ti, 0)),
                  pl.BlockSpec((1, tc, w), lambda bi, ti: (bi, ti, 1)),
                  pl.BlockSpec((1, CONV_WIDTH - 1, w), per_b),
                  pl.BlockSpec((1, 1, w), per_b),
                  pl.BlockSpec((CONV_WIDTH, w), const), pl.BlockSpec((1, w), const),
                  pl.BlockSpec((w, w), const), pl.BlockSpec((1, w), const),
                  pl.BlockSpec((w, w), const), pl.BlockSpec((1, w), const),
                  pl.BlockSpec((1, w), const)],
        out_specs=[pl.BlockSpec((1, tc, w), lambda bi, ti: (bi, ti, 0)),
                   pl.BlockSpec((1, CONV_WIDTH - 1, w), per_b),
                   pl.BlockSpec((1, 1, w), per_b)],
        out_shape=[jax.ShapeDtypeStruct((bsz, t, w), F32),
                   jax.ShapeDtypeStruct((bsz, CONV_WIDTH - 1, w), F32),
                   jax.ShapeDtypeStruct((bsz, 1, w), F32)],
        scratch_shapes=[pltpu.VMEM((V7X_SUBLANES, w), F32), pltpu.VMEM((1, w), F32),
                        pltpu.VMEM((tc, w), F32), pltpu.VMEM((tc, w), F32)],
        compiler_params=_params(("parallel", "arbitrary")),
        name="lru_branch",
    )(proj, proj, conv_buf, h0, cw, cb, wa, ba, wx, bx, lam)


def _ret_kernel(q_ref, k_ref, v_ref, g_ref, cos_ref, sin_ref, s0_ref, gng_ref, gnb_ref,
                yb_ref, slast_ref, s_s, *, c, nch):
    t = pl.program_id(1)

    @pl.when(t == 0)
    def _():
        s_s[...] = s0_ref[0]

    half = RET_HEAD_DIM // 2
    ri = lax.broadcasted_iota(jnp.int32, (c, c), 0)
    ci = lax.broadcasted_iota(jnp.int32, (c, c), 1)
    diff = (ri - ci).astype(F32)
    it = lax.broadcasted_iota(jnp.int32, (c, 1), 0).astype(F32)
    log_g = [math.log1p(-(2.0 ** (-5.0 - h))) for h in range(RET_HEADS)]
    dmask = [jnp.where(diff >= 0, jnp.exp(lg * jnp.maximum(diff, 0.0)), 0.0) for lg in log_g]
    xi = [jnp.exp(lg * (it + 1.0)) for lg in log_g]
    zeta = [jnp.exp(lg * (c - 1.0 - it)) for lg in log_g]

    units = [(ch, h) for ch in range(nch) for h in range(RET_HEADS)]
    qr, inner, kv = {}, {}, {}
    for ch, h in units:
        rows = slice(ch * c, (ch + 1) * c)
        sl = slice(h * RET_HEAD_DIM, (h + 1) * RET_HEAD_DIM)
        cos2 = cos_ref[rows, :]
        sin2 = sin_ref[rows, :]
        qh = q_ref[0, rows, sl].astype(F32)
        kh = k_ref[0, rows, sl].astype(F32)
        vh = v_ref[0, rows, sl].astype(BF16)
        qr[ch, h] = (qh * cos2 + pltpu.roll(qh, half, 1) * sin2).astype(BF16)
        kr = (kh * cos2 + pltpu.roll(kh, half, 1) * sin2) * (RET_HEAD_DIM ** -0.5)
        scores = _dot_nt(qr[ch, h], kr) * dmask[h]
        inner[ch, h] = _dot(scores, vh)
        kv[ch, h] = _dot_tn(kr * zeta[h], vh)
    for h in range(RET_HEADS):
        sl = slice(h * RET_HEAD_DIM, (h + 1) * RET_HEAD_DIM)
        s = s_s[h]
        outs = []
        for ch in range(nch):
            outs.append(inner[ch, h] + _dot(qr[ch, h], s) * xi[h])
            s = math.exp(log_g[h] * c) * s + kv[ch, h]
        s_s[h] = s
        on = _layer_norm(jnp.concatenate(outs, axis=0), gng_ref[:, sl], gnb_ref[:, sl], LN_EPS)
        gh = g_ref[0, :, sl].astype(F32)
        yb_ref[0, :, sl] = on * (gh * _sigmoid(gh))
    slast_ref[0] = s_s[...]


def _ret_branch(proj, cos2, sin2, s0, gn_g, gn_b, c):
    bsz, t, _ = proj.shape
    w = RET_WIDTH
    dk = RET_HEAD_DIM
    nch = RET_CHUNKS_PER_STEP if t % (RET_CHUNKS_PER_STEP * c) == 0 else 1
    const = lambda bi, ti: (0, 0)
    per_b = lambda bi, ti: (bi, 0, 0, 0)
    col = lambda j: (lambda bi, ti: (bi, ti, j))
    tc = nch * c
    return pl.pallas_call(
        functools.partial(_ret_kernel, c=c, nch=nch),
        grid=(bsz, t // tc),
        in_specs=[pl.BlockSpec((1, tc, w), col(2)), pl.BlockSpec((1, tc, w), col(3)),
                  pl.BlockSpec((1, tc, w), col(4)), pl.BlockSpec((1, tc, w), col(5)),
                  pl.BlockSpec((tc, dk), lambda bi, ti: (ti, 0)), pl.BlockSpec((tc, dk), lambda bi, ti: (ti, 0)),
                  pl.BlockSpec((1, RET_HEADS, dk, dk), per_b),
                  pl.BlockSpec((1, w), const), pl.BlockSpec((1, w), const)],
        out_specs=[pl.BlockSpec((1, tc, w), lambda bi, ti: (bi, ti, 0)),
                   pl.BlockSpec((1, RET_HEADS, dk, dk), per_b)],
        out_shape=[jax.ShapeDtypeStruct((bsz, t, w), F32),
                   jax.ShapeDtypeStruct((bsz, RET_HEADS, dk, dk), F32)],
        scratch_shapes=[pltpu.VMEM((RET_HEADS, dk, dk), F32)],
        compiler_params=_params(("parallel", "arbitrary")),
        name="ret_branch",
    )(proj, proj, proj, proj, cos2, sin2, s0, gn_g, gn_b)


def _rwkv_proj_kernel(x_ref, sh_ref, mu_ref, wrkv_ref, w0_ref, w1_ref, w2_ref, a0_ref, a1_ref, a2_ref,
                      g1_ref, g2_ref, kk_ref, ka_ref, rk_ref, sel_ref, selt_ref,
                      r_o, lw_o, k_o, v_o, kk_o, b_o, g_o, bonus_o, prev_s, *, tm):
    t = pl.program_id(1)

    @pl.when(t == 0)
    def _():
        prev_s[...] = jnp.zeros(prev_s.shape, F32)
        prev_s[V7X_SUBLANES - 1:, :] = sh_ref[0]

    x = x_ref[0]
    ext = jnp.concatenate([prev_s[...], x], axis=0)
    xp = pltpu.roll(ext, 1, 0)[V7X_SUBLANES:, :]
    prev_s[...] = x[tm - V7X_SUBLANES:, :]
    dx = xp - x

    def mix(p):
        return (x + dx * mu_ref[p:p + 1, :]).astype(BF16)

    r = jnp.dot(mix(0), wrkv_ref[0], preferred_element_type=F32)
    k = jnp.dot(mix(1), wrkv_ref[1], preferred_element_type=F32)
    v = jnp.dot(mix(2), wrkv_ref[2], preferred_element_type=F32)
    wl = _dot(jnp.tanh(jnp.dot(mix(3), w1_ref[...], preferred_element_type=F32)), w2_ref[...])
    w = -_softplus(-(w0_ref[...] + wl)) - 0.5
    al = _dot(jnp.dot(mix(4), a1_ref[...], preferred_element_type=F32), a2_ref[...])
    iclr = _sigmoid(a0_ref[...] + al)
    gate = _dot(_sigmoid(jnp.dot(mix(5), g1_ref[...], preferred_element_type=F32)), g2_ref[...])

    kk = k * kk_ref[...]
    sq = jnp.dot((kk * kk).astype(BF16), sel_ref[...], preferred_element_type=F32)
    inv_norm = 1.0 / jnp.maximum(jnp.sqrt(sq), 1e-12)
    kk = kk * _head_bcast(inv_norm, selt_ref)
    k2 = k * (1.0 + (iclr - 1.0) * ka_ref[...])

    r_o[0] = r.astype(BF16)
    lw_o[0] = -jnp.exp(w)
    k_o[0] = k2.astype(BF16)
    v_o[0] = v.astype(BF16)
    kk_o[0] = kk.astype(BF16)
    b_o[0] = (kk * iclr).astype(BF16)
    g_o[0] = gate
    bonus_o[0] = _seg_sum(r * k2 * rk_ref[...], sel_ref, selt_ref) * v


def _rwkv_proj(x, shift, mu, wrkv, w0, w1, w2, a0, a1, a2, g1, g2, k_k, k_a, r_k, sel, selt):
    bsz, t, d = x.shape
    tm = _row_tile(t, RWKV_PROJ_ROW_TILE)
    const2 = lambda bi, ti: (0, 0)
    const3 = lambda bi, ti: (0, 0, 0)
    full2 = lambda a: pl.BlockSpec(a.shape, const2)
    tile = pl.BlockSpec((1, tm, d), lambda bi, ti: (bi, ti, 0))
    out = jax.ShapeDtypeStruct((bsz, t, d), F32)
    out_bf = jax.ShapeDtypeStruct((bsz, t, d), BF16)
    return pl.pallas_call(
        functools.partial(_rwkv_proj_kernel, tm=tm),
        grid=(bsz, t // tm),
        in_specs=[tile, pl.BlockSpec((1, 1, d), lambda bi, ti: (bi, 0, 0)), full2(mu),
                  pl.BlockSpec(wrkv.shape, const3), full2(w0), full2(w1), full2(w2), full2(a0), full2(a1),
                  full2(a2), full2(g1), full2(g2), full2(k_k), full2(k_a), full2(r_k), full2(sel), full2(selt)],
        out_specs=[tile] * 8,
        out_shape=[out_bf, out, out_bf, out_bf, out_bf, out_bf, out, out],
        scratch_shapes=[pltpu.VMEM((V7X_SUBLANES, d), F32)],
        compiler_params=_params(("parallel", "arbitrary")),
        name="rwkv_proj",
    )(x, shift, mu, wrkv, w0, w1, w2, a0, a1, a2, g1, g2, k_k, k_a, r_k, sel, selt)


def _wkv_kernel(r_ref, lw_ref, k_ref, v_ref, kk_ref, b_ref, s0_ref, o_ref, slast_ref,
                s_s, o0_s, rhat_s, gmat_s, hmat_s, *, c, nch):
    t = pl.program_id(1)
    last = pl.num_programs(1) - 1
    n = RWKV_HEAD
    block_refs = (r_ref, lw_ref, k_ref, v_ref, kk_ref, b_ref, o_ref, s_s, o0_s, rhat_s, gmat_s, hmat_s)

    @pl.when(t == 0)
    def _():
        for h in range(RWKV_HEADS):
            s_s[:, h * n:(h + 1) * n] = s0_ref[0, h]
        o0_s[...] = jnp.zeros(o0_s.shape, o0_s.dtype)
        rhat_s[...] = jnp.zeros(rhat_s.shape, rhat_s.dtype)
        gmat_s[...] = jnp.zeros(gmat_s.shape, gmat_s.dtype)
        hmat_s[...] = jnp.zeros(hmat_s.shape, hmat_s.dtype)

    @pl.when(t < last)
    def _():
        _wkv_block(*block_refs, c=c, nch=nch, prepare=True)

    @pl.when(t == last)
    def _():
        _wkv_block(*block_refs, c=c, nch=nch, prepare=False)
        for h in range(RWKV_HEADS):
            slast_ref[0, h] = s_s[:, h * n:(h + 1) * n]


def _wkv_block(r_ref, lw_ref, k_ref, v_ref, kk_ref, b_ref, o_ref, s_s, o0_s, rhat_s, gmat_s, hmat_s, *,
               c, nch, prepare):
    t = pl.program_id(1)
    n = RWKV_HEAD
    g = WKV_GROUP
    gl = g * n
    gc = g * c
    ngroups = RWKV_HEADS // g

    iota = lambda shape, dim: lax.broadcasted_iota(jnp.int32, shape, dim)
    head_k = iota((1, gl), 1) // n
    head_j = iota((1, gc), 1) // c
    row_t = iota((c, gc), 0)
    col_j = iota((c, gc), 1) % c
    strict = row_t > col_j
    lower = row_t >= col_j
    hc = c // 2
    half_j = col_j // hc
    half_j = half_j[:1]
    blk_j = iota((1, gc), 1) // hc
    eye_h = jnp.where(iota((hc, gc), 0) == iota((hc, gc), 1) % hc, 1.0, 0.0).astype(F32)
    blk = (iota((gl, gl), 0) // n) == (iota((gl, gl), 1) // n)
    eye_l = iota((gl, gl), 0) == iota((gl, gl), 1)

    def bd(z, lane_head):
        zb = z.astype(BF16)
        return jnp.concatenate([jnp.where(lane_head == h, zb, jnp.zeros_like(zb)) for h in range(g)], axis=0)

    def bd_half(z):
        zb = z.astype(BF16)
        return jnp.concatenate([jnp.where(blk_j == q, zb, jnp.zeros_like(zb)) for q in range(2 * g)], axis=0)

    def mm(a, b):
        return jnp.dot(a.astype(BF16), b, preferred_element_type=F32)

    def mm_nt(a, b):
        return lax.dot_general(a.astype(BF16), b, _NT, preferred_element_type=F32)

    started = t > 0
    group_lanes = [slice(gi * gl, (gi + 1) * gl) for gi in range(ngroups)]
    state = [s_s[:, lanes] for lanes in group_lanes]

    def recurrence_step(ch):
        rows = slice(ch * c, (ch + 1) * c)
        for gi, lanes in enumerate(group_lanes):
            s = state[gi]
            o_ref[0, rows, lanes] = o0_s[rows, lanes] + mm_nt(rhat_s[rows, lanes], bd(s, head_k))
            s = mm(s, gmat_s[ch * ngroups + gi]) + hmat_s[ch, :, lanes]
            if ch == nch - 1:
                s_s[:, lanes] = jnp.where(started, s, s_s[:, lanes])
            state[gi] = s

    if not prepare:
        for ch in range(nch):
            recurrence_step(ch)
        return

    units = [(ch, gi) for ch in range(nch) for gi in range(ngroups)]
    u = {}
    for ch in range(nch):
        rows = slice(ch * c, (ch + 1) * c)
        lw = lw_ref[0, rows, :]
        cum = _cumsum_rows(lw)
        cum_last = cum[c - 1:c, :]
        e_neg = jnp.exp(-cum)
        e_end = jnp.exp(cum_last - cum)
        kk = kk_ref[0, rows, :]
        b = b_ref[0, rows, :]
        k = k_ref[0, rows, :]
        full = dict(a_t=kk * jnp.exp(cum - lw), b_t=b * e_neg, k_t=k * e_neg, r_t=r_ref[0, rows, :] * jnp.exp(cum),
                    b_e=b * e_end, k_e=k * e_end, d_end=jnp.exp(cum_last), v=v_ref[0, rows, :])
        for gi in range(ngroups):
            lanes = slice(gi * gl, (gi + 1) * gl)
            u[ch, gi] = {name: val[:, lanes] for name, val in full.items()}

    for key in units:
        d = u[key]
        x2 = jnp.concatenate([d['a_t'], d['r_t']], axis=0).astype(BF16)
        d['mb'] = mm_nt(x2, bd(d['b_t'], head_k))
        d['mk'] = mm_nt(x2, bd(d['k_t'], head_k))
    for key in units:
        d = u[key]
        mb = d['mb']
        mk = d['mk']
        d['p_rb'] = jnp.where(lower, mb[c:], 0.0).astype(BF16)
        mk_lo = jnp.concatenate([jnp.where(strict, mk[:c], 0.0), jnp.where(lower, mk[c:], 0.0)], axis=0)
        wo = mm(mk_lo, bd(d['v'], head_k))
        d['w_h'] = wo[:c]
        d['o0'] = wo[c:]
        nmat = -jnp.where(strict, mb[:c], 0.0)
        d['n_low'] = jnp.where(jnp.logical_and(row_t >= hc, half_j == 0), nmat, 0.0)
        d['pw'] = jnp.where(half_j == 0, nmat[:hc], nmat[hc:])
        d['tinv'] = eye_h + d['pw']
        if key[1] == ngroups - 1:
            recurrence_step(key[0])
    nsq = int(math.log2(hc)) - 1
    for level in range(nsq):
        for key in units:
            d = u[key]
            pwb = d['pw'].astype(BF16)
            w = bd_half(pwb)
            if level == 0:
                d['pw'] = mm(pwb, w)
            else:
                res = mm(jnp.concatenate([pwb, d['tinv'].astype(BF16)], axis=0), w)
                d['pw'] = res[:hc]
                d['tinv'] = d['tinv'] + res[hc:]
    for key in units:
        d = u[key]
        d['tinv'] = d['tinv'] + mm(d['tinv'], bd_half(d['pw']))
    for key in units:
        d = u[key]
        t_lo = jnp.where(half_j == 1, d['tinv'], 0.0)
        t_diag = jnp.concatenate([jnp.where(half_j == 0, d['tinv'], 0.0), t_lo], axis=0)
        corr = mm(mm(t_lo, bd(d['n_low'], head_j)), bd(t_diag, head_j))
        d['tinv'] = t_diag + jnp.concatenate([jnp.zeros((hc, gc), F32), corr], axis=0)
    for key in units:
        d = u[key]
        tb = d['tinv'].astype(BF16)
        d['a_hat'] = mm(tb, bd(d['a_t'], head_k))
        d['w_hat'] = mm(tb, bd(d['w_h'], head_k))
    for key in units:
        d = u[key]
        d['r_hat'] = d['r_t'] - mm(d['p_rb'], bd(d['a_hat'], head_k))
        d['o0'] = d['o0'] - mm(d['p_rb'], bd(d['w_hat'], head_k))
        b_e = d['b_e'].astype(BF16)
        gfull = lax.dot_general(d['a_hat'].astype(BF16), b_e, _TN, preferred_element_type=F32)
        d['gmat'] = (jnp.where(eye_l, d['d_end'], 0.0) - jnp.where(blk, gfull, 0.0)).astype(BF16)
        vw = jnp.concatenate([d['v'].astype(BF16), d['w_hat'].astype(BF16)], axis=0)
        kb = jnp.concatenate([d['k_e'].astype(BF16), -b_e], axis=0)
        zf = lax.dot_general(vw, kb, _TN, preferred_element_type=F32)
        hmat = jnp.where(head_k == 0, zf[:n], 0.0)
        for h in range(1, g):
            hmat = hmat + jnp.where(head_k == h, zf[h * n:(h + 1) * n], 0.0)
        d['hmat'] = hmat

    for (ch, gi) in units:
        d = u[ch, gi]
        lanes = slice(gi * gl, (gi + 1) * gl)
        o0_s[ch * c:(ch + 1) * c, lanes] = d['o0']
        rhat_s[ch * c:(ch + 1) * c, lanes] = d['r_hat'].astype(BF16)
        gmat_s[ch * ngroups + gi] = d['gmat']
        hmat_s[ch, :, lanes] = d['hmat']


def _wkv_scan(r, lw, k, v, kk, b, s0, c):
    bsz, t, d = r.shape
    n = RWKV_HEAD
    nch = WKV_CHUNKS_PER_STEP if t % (WKV_CHUNKS_PER_STEP * c) == 0 else 1
    nsteps = t // (nch * c)
    g = WKV_GROUP
    tile_in = pl.BlockSpec((1, nch * c, d), lambda bi, ti: (bi, jnp.minimum(ti, nsteps - 1), 0))
    tile_out = pl.BlockSpec((1, nch * c, d), lambda bi, ti: (bi, jnp.maximum(ti - 1, 0), 0))
    st = pl.BlockSpec((1, RWKV_HEADS, n, n), lambda bi, ti: (bi, 0, 0, 0))
    return pl.pallas_call(
        functools.partial(_wkv_kernel, c=c, nch=nch),
        grid=(bsz, nsteps + 1),
        in_specs=[tile_in] * 6 + [st],
        out_specs=[tile_out, st],
        out_shape=[jax.ShapeDtypeStruct((bsz, t, d), F32), jax.ShapeDtypeStruct(s0.shape, F32)],
        scratch_shapes=[pltpu.VMEM((n, d), F32), pltpu.VMEM((nch * c, d), F32), pltpu.VMEM((nch * c, d), BF16),
                        pltpu.VMEM((nch * (RWKV_HEADS // g), g * n, g * n), BF16), pltpu.VMEM((nch, n, d), F32)],
        compiler_params=_params(("parallel", "arbitrary")),
        name="wkv_scan",
    )(r, lw, k, v, kk, b, s0)


def _wkv_out_kernel(x_ref, o_ref, bonus_ref, gate_ref, gng_ref, gnb_ref, sel_ref, selt_ref, w_ref,
                    g_ref, b_ref, y_ref):
    o = o_ref[...]
    inv_n = 1.0 / RWKV_HEAD
    mu = _seg_sum(o, sel_ref, selt_ref) * inv_n
    oc = o - mu
    var = _seg_sum(oc * oc, sel_ref, selt_ref) * inv_n
    on = oc * lax.rsqrt(var + RWKV_GN_EPS) * gng_ref[...] + gnb_ref[...]
    z = (on + bonus_ref[...]) * gate_ref[...]
    y = jnp.dot(z.astype(BF16), w_ref[...], preferred_element_type=F32)
    y_ref[...] = _layer_norm(ALPHA * x_ref[...] + y, g_ref[...], b_ref[...], LN_EPS)


def _wkv_out_postnorm(x, o, bonus, gate, gn_g, gn_b, sel, selt, w, g, b):
    n, d = x.shape
    tm = _row_tile(n)
    row = pl.BlockSpec((tm, d), lambda i: (i, 0))
    const = lambda i: (0, 0)
    full = lambda a: pl.BlockSpec(a.shape, const)
    return pl.pallas_call(
        _wkv_out_kernel,
        grid=(n // tm,),
        in_specs=[row, row, row, row, full(gn_g), full(gn_b), full(sel), full(selt), full(w), full(g), full(b)],
        out_specs=row,
        out_shape=jax.ShapeDtypeStruct((n, d), F32),
        compiler_params=_params(("parallel",)),
        name="wkv_out_postnorm",
    )(x, o, bonus, gate, gn_g, gn_b, sel, selt, w, g, b)


def _prep_weights(ln_g, ln_b, ffn_up, ffn_down, xa_q, xa_o, l0_w_in, l0_conv_w, l0_conv_b, l0_lru_wa,
                  l0_lru_ba, l0_lru_wx, l0_lru_bx, l0_lru_lambda, l0_ret_gn_g, l0_ret_gn_b, l0_w_out,
                  l1_mu, l1_w_rkv, l1_w0, l1_w1, l1_w2, l1_a0, l1_a1, l1_a2, l1_g1, l1_g2, l1_k_k, l1_k_a,
                  l1_r_k, l1_gn_g, l1_gn_b, l1_w_out):
    d = D_MODEL
    row = lambda a: a.reshape(1, -1).astype(F32)

    def block_diag(w):
        eye = jnp.eye(LRU_BLOCKS, dtype=w.dtype)
        return jnp.einsum('gij,gh->gihj', w, eye).reshape(LRU_WIDTH, LRU_WIDTH).astype(BF16)

    head_of_col = jnp.arange(d) // RWKV_HEAD
    sel = (head_of_col[:, None] == jnp.arange(SEL_WIDTH)[None, :]).astype(BF16)
    sel_row = jnp.arange(SEL_WIDTH)
    selt = ((sel_row[:, None] % RWKV_HEADS == head_of_col[None, :])
            & (sel_row[:, None] < 2 * RWKV_HEADS)).astype(BF16)
    return dict(
        ln_g=ln_g, ln_b=ln_b, ffn_up=ffn_up.astype(BF16), ffn_down=ffn_down.astype(BF16),
        xa_q=(xa_q * (MEM_HEAD_DIM ** -0.5)).astype(BF16), xa_o=xa_o.astype(BF16),
        w_in=l0_w_in.astype(BF16), conv_w=l0_conv_w, conv_b=row(l0_conv_b),
        lru_wa=block_diag(l0_lru_wa), lru_ba=row(l0_lru_ba), lru_wx=block_diag(l0_lru_wx),
        lru_bx=row(l0_lru_bx), lru_lam=row(l0_lru_lambda),
        ret_gn_g=row(l0_ret_gn_g), ret_gn_b=row(l0_ret_gn_b),
        w_out_a=l0_w_out[:LRU_WIDTH].astype(BF16), w_out_b=l0_w_out[LRU_WIDTH:].astype(BF16),
        mu=l1_mu, w_rkv=l1_w_rkv.astype(BF16), w0=row(l1_w0), w1=l1_w1.astype(BF16), w2=l1_w2.astype(BF16),
        a0=row(l1_a0), a1=l1_a1.astype(BF16), a2=l1_a2.astype(BF16), g1=l1_g1.astype(BF16),
        g2=l1_g2.astype(BF16), k_k=row(l1_k_k), k_a=row(l1_k_a), r_k=row(l1_r_k),
        gn_g=row(l1_gn_g), gn_b=row(l1_gn_b), w_out_c=l1_w_out.astype(BF16), sel=sel, selt=selt,
    )


def _rotary_tables(pos):
    half = RET_HEAD_DIM // 2
    inv_freq = ROPE_BASE ** (-jnp.arange(half, dtype=F32) / half)
    ang = pos.astype(F32)[:, None] * inv_freq[None, :]
    cos = jnp.cos(ang)
    sin = jnp.sin(ang)
    return jnp.concatenate([cos, cos], axis=-1), jnp.concatenate([-sin, sin], axis=-1)


def _run_trunk(x, pos, mem_k, mem_v, states, p):
    bsz, t, d = x.shape
    n = bsz * t
    (conv_buf, h0, s_ret), (shift, s_wkv) = states
    lng = lambda l, j: p['ln_g'][l, j].reshape(1, d)
    lnb = lambda l, j: p['ln_b'][l, j].reshape(1, d)
    flat = lambda a: a.reshape(n, a.shape[-1])
    chunk = min(CHUNK, t)
    ffn = lambda xx, l, j, nj: _ffn_postnorm(xx, p['ffn_up'], p['ffn_down'], l, j, lng(l, nj), lnb(l, nj))

    x = ffn(flat(x), 0, 0, 0)
    proj = _matmul(x, p['w_in'], BF16).reshape(bsz, t, -1)
    ya, new_buf, h_last = _lru_branch(proj, conv_buf, h0.reshape(bsz, 1, LRU_WIDTH), p['conv_w'], p['conv_b'],
                                      p['lru_wa'], p['lru_ba'], p['lru_wx'], p['lru_bx'], p['lru_lam'])
    cos2, sin2 = _rotary_tables(pos)
    yb, s_ret_new = _ret_branch(proj, cos2, sin2, s_ret, p['ret_gn_g'], p['ret_gn_b'], chunk)
    x = _ab_out_postnorm(x, flat(ya), flat(yb), p['w_out_a'], p['w_out_b'], lng(0, 1), lnb(0, 1))
    x = _xattn_postnorm(x.reshape(bsz, t, d), mem_k, mem_v, p['xa_q'], p['xa_o'], 0, lng(0, 2), lnb(0, 2))
    x = ffn(flat(x), 0, 1, 3)

    x = ffn(x, 1, 0, 0)
    x3 = x.reshape(bsz, t, d)
    r, lw, k2, v, kk, b, gate, bonus = _rwkv_proj(
        x3, shift, p['mu'], p['w_rkv'], p['w0'], p['w1'], p['w2'], p['a0'], p['a1'], p['a2'], p['g1'], p['g2'],
        p['k_k'], p['k_a'], p['r_k'], p['sel'], p['selt'])
    o, s_wkv_new = _wkv_scan(r, lw, k2, v, kk, b, s_wkv, chunk)
    new_shift = x3[:, t - 1:, :]
    x = _wkv_out_postnorm(x, flat(o), flat(bonus), flat(gate), p['gn_g'], p['gn_b'], p['sel'], p['selt'],
                          p['w_out_c'], lng(1, 1), lnb(1, 1))
    x = _xattn_postnorm(x.reshape(bsz, t, d), mem_k, mem_v, p['xa_q'], p['xa_o'], 1, lng(1, 2), lnb(1, 2))
    x = ffn(flat(x), 1, 1, 3)
    new_states = ((new_buf, h_last.reshape(bsz, LRU_WIDTH), s_ret_new), (new_shift, s_wkv_new))
    return x.reshape(bsz, t, d), new_states


def kernel(x_prompt, x_sample, mem_prompt, state_conv0, state_lru0, state_ret0, state_shift1, state_wkv1,
           cache_mem_k, cache_mem_v, ln_g, ln_b, ffn_up, ffn_down, xa_q, xa_k, xa_v, xa_o,
           l0_w_in, l0_conv_w, l0_conv_b, l0_lru_wa, l0_lru_ba, l0_lru_wx, l0_lru_bx, l0_lru_lambda,
           l0_ret_gn_g, l0_ret_gn_b, l0_w_out, l1_mu, l1_w_rkv, l1_w0, l1_w1, l1_w2, l1_a0, l1_a1, l1_a2,
           l1_g1, l1_g2, l1_k_k, l1_k_a, l1_r_k, l1_gn_g, l1_gn_b, l1_w_out):
    d = D_MODEL
    p = _prep_weights(ln_g, ln_b, ffn_up, ffn_down, xa_q, xa_o, l0_w_in, l0_conv_w, l0_conv_b, l0_lru_wa,
                      l0_lru_ba, l0_lru_wx, l0_lru_bx, l0_lru_lambda, l0_ret_gn_g, l0_ret_gn_b, l0_w_out,
                      l1_mu, l1_w_rkv, l1_w0, l1_w1, l1_w2, l1_a0, l1_a1, l1_a2, l1_g1, l1_g2, l1_k_k, l1_k_a,
                      l1_r_k, l1_gn_g, l1_gn_b, l1_w_out)

    bp, tp, _ = x_prompt.shape
    dt = x_prompt.dtype
    mem_k_p, mem_kh_p = _mem_proj(mem_prompt, xa_k.astype(BF16))
    mem_v_p, mem_vh_p = _mem_proj(mem_prompt, xa_v.astype(BF16))
    zero_states = ((jnp.zeros((bp, CONV_WIDTH - 1, LRU_WIDTH), dt), jnp.zeros((bp, LRU_WIDTH), dt),
                    jnp.zeros((bp, RET_HEADS, RET_HEAD_DIM, RET_HEAD_DIM), dt)),
                   (jnp.zeros((bp, 1, d), dt), jnp.zeros((bp, RWKV_HEADS, RWKV_HEAD, RWKV_HEAD), dt)))
    y_prompt, st_p = _run_trunk(x_prompt, jnp.arange(tp, dtype=jnp.int32), mem_kh_p, mem_vh_p, zero_states, p)

    bs, ts, _ = x_sample.shape
    pos_s = PAST_LEN + jnp.arange(ts, dtype=jnp.int32)
    sample_states = ((state_conv0, state_lru0, state_ret0), (state_shift1, state_wkv1))
    head_major = lambda a: jnp.transpose(a, (0, 1, 3, 2, 4)).astype(BF16)
    y_sample, st_s = _run_trunk(x_sample, pos_s, head_major(cache_mem_k), head_major(cache_mem_v), sample_states, p)

    (p_conv0, p_lru0, p_ret0), (p_shift1, p_wkv1) = st_p
    (s_conv0, s_lru0, s_ret0), (s_shift1, s_wkv1) = st_s
    return (y_prompt, y_sample, mem_k_p, mem_v_p,
            p_conv0, p_lru0, p_ret0, p_shift1, p_wkv1, s_conv0, s_lru0, s_ret0, s_shift1, s_wkv1)
```
